```python
import math
import jax, jax.numpy as jnp
from jax import lax
import numpy as np

D_MODEL = 1024
BATCH = 8
SEQ = 4096
DEPTH = 2

D_MIX = D_MODEL
SGU_GROUPS = 8
SGU_GROUP_DIM = 64
SGU_WIDTH = SGU_GROUPS * SGU_GROUP_DIM
CHUNK = 128
MLA_HEADS = 4
QK_NOPE = 128
QK_ROPE = 64
QK_HEAD = QK_NOPE + QK_ROPE
V_HEAD = 128
MLA_WIDTH = MLA_HEADS * V_HEAD
Q_LORA = 256
KV_LORA = 128
D_IN = 2 * SGU_WIDTH + Q_LORA + KV_LORA + QK_ROPE
IN_SPLITS = (SGU_WIDTH, 2 * SGU_WIDTH, 2 * SGU_WIDTH + Q_LORA, 2 * SGU_WIDTH + Q_LORA + KV_LORA)
ROPE_BASE = 10000.0
Q_BLOCK = 128
D_FF = 2816
N_EXPERTS = 8
TOP_K = 2
D_FF_EXPERT = 2816
N_MOD = 6
EPS = 1e-6
N_DENSE_LAYERS = (DEPTH + 1) // 2
N_MOE_LAYERS = DEPTH // 2

kernel_name = "hybrid_sgu_mla_moe_adaln"


def rms_norm(x, g):
    xf = x.astype(jnp.float32)
    y = xf * lax.rsqrt(jnp.mean(xf * xf, axis=-1, keepdims=True) + EPS)
    return (y * g.astype(jnp.float32)).astype(x.dtype)


def layer_norm(x, g):
    xf = x.astype(jnp.float32)
    mu = jnp.mean(xf, axis=-1, keepdims=True)
    var = jnp.mean(jnp.square(xf - mu), axis=-1, keepdims=True)
    y = (xf - mu) * lax.rsqrt(var + EPS)
    return (y * g.astype(jnp.float32)).astype(x.dtype)


def modulate(h, shift, scale):
    return h * (1 + scale[:, None, :]) + shift[:, None, :]


def rope_tables(positions):
    inv_freq = 1.0 / (ROPE_BASE ** (jnp.arange(0, QK_ROPE, 2, dtype=jnp.float32) / QK_ROPE))
    ang = positions.astype(jnp.float32)[..., None] * inv_freq
    return jnp.cos(ang)[:, :, None, :], jnp.sin(ang)[:, :, None, :]


def rope_tail(x, cos, sin):
    x_nope, x_rope = jnp.split(x, [QK_NOPE], axis=-1)
    x1, x2 = jnp.split(x_rope, 2, axis=-1)
    cos = cos.astype(x.dtype)
    sin = sin.astype(x.dtype)
    return jnp.concatenate([x_nope, x1 * cos - x2 * sin, x2 * cos + x1 * sin], axis=-1)


def spatial_gating(zu, zv, norm_g, w_s, b_s):
    B, S, _ = zu.shape
    n = S // CHUNK
    u = jax.nn.gelu(zu)
    v = layer_norm(jax.nn.gelu(zv), norm_g)
    v = v.reshape(B, n, CHUNK, SGU_GROUPS, SGU_GROUP_DIM)
    w_causal = w_s * jnp.tril(jnp.ones((CHUNK, CHUNK), w_s.dtype))
    s = jnp.einsum('gts,bnsgd->bntgd', w_causal, v) + b_s.T[None, None, :, :, None]
    return u * s.reshape(B, S, SGU_WIDTH)


def causal_block_attention(q, k, v):
    B, S, H, Dq = q.shape
    nb = S // Q_BLOCK
    scale = Dq ** -0.5
    qb = q.reshape(B, nb, Q_BLOCK, H, Dq).transpose(1, 0, 3, 2, 4)
    kf = k.astype(jnp.float32)
    k_pos = jnp.arange(S)
    neg = jnp.finfo(jnp.float32).min

    def one_block(args):
        q_blk, blk = args
        s = jnp.einsum('bhqd,bkhd->bhqk', q_blk.astype(jnp.float32), kf) * scale
        q_pos = blk * Q_BLOCK + jnp.arange(Q_BLOCK)
        s = jnp.where(k_pos[None, :] <= q_pos[:, None], s, neg)
        p = jax.nn.softmax(s, axis=-1)
        return jnp.einsum('bhqk,bkhd->bqhd', p.astype(v.dtype), v)

    o = lax.map(one_block, (qb, jnp.arange(nb)))
    return o.transpose(1, 0, 2, 3, 4).reshape(B, S, H, V_HEAD)


def latent_attention(cq, ckv, k_rope, cos, sin, q_lat_norm, kv_lat_norm, w_uq, w_ukv, q_norm, k_norm):
    B, S, _ = cq.shape
    q = (rms_norm(cq, q_lat_norm) @ w_uq).reshape(B, S, MLA_HEADS, QK_HEAD)
    kv = (rms_norm(ckv, kv_lat_norm) @ w_ukv).reshape(B, S, MLA_HEADS, QK_NOPE + V_HEAD)
    k_nope, v = jnp.split(kv, [QK_NOPE], axis=-1)
    k_r = jnp.broadcast_to(k_rope[:, :, None, :], (B, S, MLA_HEADS, QK_ROPE))
    k = jnp.concatenate([k_nope, k_r], axis=-1)
    q = rope_tail(rms_norm(q, q_norm), cos, sin)
    k = rope_tail(rms_norm(k, k_norm), cos, sin)
    o = causal_block_attention(q, k, v)
    return o.reshape(B, S, MLA_WIDTH)


def hybrid_mixer(h, cos, sin, w_in, sgu_norm, sgu_w, sgu_b, q_lat_norm, kv_lat_norm,
                 w_uq, w_ukv, q_norm, k_norm, w_out):
    proj = h @ w_in
    zu, zv, cq, ckv, k_rope = jnp.split(proj, IN_SPLITS, axis=-1)
    a = spatial_gating(zu, zv, sgu_norm, sgu_w, sgu_b)
    m = latent_attention(cq, ckv, k_rope, cos, sin, q_lat_norm, kv_lat_norm,
                         w_uq, w_ukv, q_norm, k_norm)
    return jnp.concatenate([a, m], axis=-1) @ w_out


def swiglu(x, w1, w3, w2):
    return (jax.nn.silu(x @ w1) * (x @ w3)) @ w2


def moe_swiglu(h, router_w, w1, w3, w2):
    B, S, D = h.shape
    xt = h.reshape(-1, D)
    logits = (xt @ router_w).astype(jnp.float32)
    top_val, top_idx = lax.top_k(logits, TOP_K)
    top_w = jax.nn.softmax(top_val, axis=-1)
    gates = jnp.sum(jax.nn.one_hot(top_idx, N_EXPERTS, dtype=jnp.float32) * top_w[..., None],
                    axis=1).astype(h.dtype)
    y = jnp.zeros_like(xt)
    for e in range(N_EXPERTS):
        y = y + gates[:, e:e + 1] * swiglu(xt, w1[e], w3[e], w2[e])
    return y.reshape(B, S, D)


def setup_inputs(seed: int = 0) -> dict:
    key = jax.random.key(seed)
    ks = jax.random.split(key, 32)
    L = DEPTH
    f = jnp.float32

    def nrm(k, shape, scale):
        return jax.random.normal(k, shape, f) * scale

    def gain(k, shape):
        return 1.0 + 0.02 * jax.random.normal(k, shape, f)

    positions = (jnp.arange(SEQ, dtype=jnp.int32)[None, :]
                 + jax.random.randint(ks[2], (BATCH, 1), 0, 1024, dtype=jnp.int32))
    return {
        "x": nrm(ks[0], (BATCH, SEQ, D_MODEL), 1.0),
        "c": nrm(ks[1], (BATCH, D_MODEL), 1.0),
        "positions": positions,
        "ada_w": nrm(ks[3], (L, D_MODEL, N_MOD * D_MODEL), 0.5 * D_MODEL ** -0.5),
        "ada_b": nrm(ks[4], (L, N_MOD * D_MODEL), 0.01),
        "norm_mix": gain(ks[5], (L, D_MODEL)),
        "norm_ffn": gain(ks[6], (L, D_MODEL)),
        "w_in": nrm(ks[7], (L, D_MODEL, D_IN), D_MODEL ** -0.5),
        "sgu_norm": gain(ks[8], (L, SGU_WIDTH)),
        "sgu_w": nrm(ks[9], (L, SGU_GROUPS, CHUNK, CHUNK), CHUNK ** -0.5),
        "sgu_b": 1.0 + 0.1 * jax.random.normal(ks[10], (L, SGU_GROUPS, CHUNK), f),
        "q_lat_norm": gain(ks[11], (L, Q_LORA)),
        "kv_lat_norm": gain(ks[12], (L, KV_LORA)),
        "w_uq": nrm(ks[13], (L, Q_LORA, MLA_HEADS * QK_HEAD), Q_LORA ** -0.5),
        "w_ukv": nrm(ks[14], (L, KV_LORA, MLA_HEADS * (QK_NOPE + V_HEAD)), KV_LORA ** -0.5),
        "q_norm": gain(ks[15], (L, QK_HEAD)),
        "k_norm": gain(ks[16], (L, QK_HEAD)),
        "w_out": nrm(ks[17], (L, D_MIX, D_MODEL), D_MIX ** -0.5),
        "ffn_w1": nrm(ks[18], (N_DENSE_LAYERS, D_MODEL, D_FF), D_MODEL ** -0.5),
        "ffn_w3": nrm(ks[19], (N_DENSE_LAYERS, D_MODEL, D_FF), D_MODEL ** -0.5),
        "ffn_w2": nrm(ks[20], (N_DENSE_LAYERS, D_FF, D_MODEL), D_FF ** -0.5),
        "router_w": nrm(ks[21], (N_MOE_LAYERS, D_MODEL, N_EXPERTS), D_MODEL ** -0.5),
        "moe_w1": nrm(ks[22], (N_MOE_LAYERS, N_EXPERTS, D_MODEL, D_FF_EXPERT), D_MODEL ** -0.5),
        "moe_w3": nrm(ks[23], (N_MOE_LAYERS, N_EXPERTS, D_MODEL, D_FF_EXPERT), D_MODEL ** -0.5),
        "moe_w2": nrm(ks[24], (N_MOE_LAYERS, N_EXPERTS, D_FF_EXPERT, D_MODEL), D_FF_EXPERT ** -0.5),
    }


def reference(x, c, positions, ada_w, ada_b, norm_mix, norm_ffn, w_in, sgu_norm, sgu_w, sgu_b,
              q_lat_norm, kv_lat_norm, w_uq, w_ukv, q_norm, k_norm, w_out,
              ffn_w1, ffn_w3, ffn_w2, router_w, moe_w1, moe_w3, moe_w2):
    cos, sin = rope_tables(positions)
    c_act = jax.nn.silu(c)
    for layer in range(DEPTH):
        mod = c_act @ ada_w[layer] + ada_b[layer]
        sh1, sc1, g1, sh2, sc2, g2 = jnp.split(mod, N_MOD, axis=-1)
        h = modulate(rms_norm(x, norm_mix[layer]), sh1, sc1)
        y = hybrid_mixer(h, cos, sin, w_in[layer], sgu_norm[layer], sgu_w[layer], sgu_b[layer],
                         q_lat_norm[layer], kv_lat_norm[layer], w_uq[layer], w_ukv[layer],
                         q_norm[layer], k_norm[layer], w_out[layer])
        x = x + g1[:, None, :] * y
        h = modulate(rms_norm(x, norm_ffn[layer]), sh2, sc2)
        i = layer // 2
        if layer % 2 == 0:
            y = swiglu(h, ffn_w1[i], ffn_w3[i], ffn_w2[i])
        else:
            y = moe_swiglu(h, router_w[i], moe_w1[i], moe_w3[i], moe_w2[i])
        x = x + g2[:, None, :] * y
    return x
```

```python
import functools
import math

import jax
import jax.numpy as jnp
from jax import lax
from jax.experimental import pallas as pl
from jax.experimental.pallas import tpu as pltpu

F32 = jnp.float32
BF16 = jnp.bfloat16

EPS = 1e-6
ROPE_BASE = 10000.0
SGU_GROUPS = 8
SGU_GROUP_DIM = 64
SGU_WIDTH = SGU_GROUPS * SGU_GROUP_DIM
CHUNK = 128
MLA_HEADS = 4
QK_NOPE = 128
QK_ROPE = 64
QK_HEAD = QK_NOPE + QK_ROPE
QK_PAD = 256
V_HEAD = 128
MLA_WIDTH = MLA_HEADS * V_HEAD
Q_LORA = 256
KV_LORA = 128
N_MOD = 6
N_EXPERTS = 8
TOP_K = 2
LANES = 128
D_IN_PAD = 2 * SGU_WIDTH + Q_LORA + KV_LORA + LANES

MIB = 1024 * 1024

TM_MIX = 512
TQ = 512
TM_FFN = 512
TM_ROUTE = 512
TG = 256
TM_COMB = 256
FF_CHUNK = 1408


def _dot(a, b):
    return jnp.dot(a, b, preferred_element_type=F32)


def _gelu_tanh(x):
    return 0.5 * x * (1.0 + jnp.tanh(math.sqrt(2.0 / math.pi) * (x + 0.044715 * (x * x * x))))


def _silu(x):
    return x * (1.0 / (1.0 + jnp.exp(-x)))


def _rms(x, width):
    return x * lax.rsqrt(jnp.sum(x * x, axis=-1, keepdims=True) * (1.0 / width) + EPS)


def _params(sem, vmem_mib):
    return pltpu.CompilerParams(dimension_semantics=sem, vmem_limit_bytes=vmem_mib * MIB)


def _const_spec(shape):
    nd = len(shape)
    return pl.BlockSpec(shape, lambda *_: (0,) * nd)


def _mod_kernel(c_ref, w_ref, b_ref, o_ref):
    c = c_ref[...]
    ca = _silu(c)
    ca_hi = ca.astype(BF16)
    ca_lo = (ca - ca_hi.astype(F32)).astype(BF16)
    w = w_ref[...]
    w_hi = w.astype(BF16)
    w_lo = (w - w_hi.astype(F32)).astype(BF16)
    acc = _dot(ca_hi, w_hi) + _dot(ca_hi, w_lo) + _dot(ca_lo, w_hi)
    o_ref[...] = acc + b_ref[...]


def _modulation(c, ada_w, ada_b):
    L, D, W = ada_w.shape
    B = c.shape[0]
    rows = 16
    c_pad = jnp.zeros((rows, D), F32).at[:B].set(c)
    tn = 1536
    out = pl.pallas_call(
        _mod_kernel,
        grid=(L, W // tn),
        in_specs=[
            pl.BlockSpec((rows, D), lambda l, j: (0, 0)),
            pl.BlockSpec((None, D, tn), lambda l, j: (l, 0, j)),
            pl.BlockSpec((None, 1, tn), lambda l, j: (l, 0, j)),
        ],
        out_specs=pl.BlockSpec((None, rows, tn), lambda l, j: (l, 0, j)),
        out_shape=jax.ShapeDtypeStruct((L, rows, W), F32),
        compiler_params=_params(("parallel", "parallel"), 40),
        name="adaln_mod",
    )(c_pad, ada_w, ada_b.reshape(L, 1, W))
    return out[:, :B].reshape(L, B, N_MOD, D)


def _rope_kernel(pos_ref, inv_ref, cos_ref, sin_ref):
    pos = pos_ref[0].astype(F32)
    ang = inv_ref[...] * pos
    co = jnp.cos(ang)
    si = jnp.sin(ang)
    z = jnp.zeros((2 * co.shape[0], co.shape[1]), F32)
    cos_ref[...] = jnp.concatenate([co, co, z], axis=0).T
    sin_ref[...] = jnp.concatenate([-si, si, z], axis=0).T


def _rope_tables(positions):
    n = positions.size
    tn = 512
    half = QK_ROPE // 2
    inv_freq = 1.0 / (ROPE_BASE ** (jnp.arange(0, QK_ROPE, 2, dtype=F32) / QK_ROPE))
    pos3 = positions.reshape(n // tn, 1, tn)
    return pl.pallas_call(
        _rope_kernel,
        grid=(n // tn,),
        in_specs=[
            pl.BlockSpec((1, 1, tn), lambda i: (i, 0, 0)),
            pl.BlockSpec((half, 1), lambda i: (0, 0)),
        ],
        out_specs=[pl.BlockSpec((tn, LANES), lambda i: (i, 0))] * 2,
        out_shape=[jax.ShapeDtypeStruct((n, LANES), F32)] * 2,
        compiler_params=_params(("parallel",), 32),
        name="rope_tables",
    )(pos3, inv_freq.reshape(half, 1))


def _rope_rotate(r, cos_t, sin_t, lane):
    partner = jnp.where(lane < QK_ROPE // 2,
                        pltpu.roll(r, LANES - QK_ROPE // 2, 1),
                        pltpu.roll(r, QK_ROPE // 2, 1))
    return r * cos_t + partner * sin_t


def _mixer_pre_kernel(x_ref, mod_ref, nmix_ref, win_ref, sgn_ref, sgw_ref, sgb_ref,
                      qln_ref, kvln_ref, wuq_ref, wuk_ref, wuv_ref, qn_ref, kn_ref,
                      cos_ref, sin_ref, a_ref, q_ref, k_ref, v_ref):
    tm, d = x_ref.shape
    x = x_ref[...]
    shift = mod_ref[0:1, :]
    scale = mod_ref[1:2, :]
    h = _rms(x, d) * nmix_ref[...]
    h = h * (1.0 + scale) + shift
    proj = _dot(h.astype(BF16), win_ref[...])

    o_zv = SGU_WIDTH
    o_cq = 2 * SGU_WIDTH
    o_ckv = o_cq + Q_LORA
    o_kr = o_ckv + KV_LORA

    u = _gelu_tanh(proj[:, :SGU_WIDTH])
    gv = _gelu_tanh(proj[:, o_zv:o_cq])
    mu = jnp.mean(gv, axis=-1, keepdims=True)
    cen = gv - mu
    var = jnp.mean(cen * cen, axis=-1, keepdims=True)
    vn = cen * lax.rsqrt(var + EPS) * sgn_ref[...]

    row = lax.broadcasted_iota(jnp.int32, (CHUNK, CHUNK), 0)
    col = lax.broadcasted_iota(jnp.int32, (CHUNK, CHUNK), 1)
    causal = col <= row
    lane = lax.broadcasted_iota(jnp.int32, (CHUNK, LANES), 1)
    low_half = lane < SGU_GROUP_DIM
    n_pairs = SGU_GROUPS // 2
    wcat = []
    for j in range(n_pairs):
        wa = jnp.where(causal, sgw_ref[2 * j], 0.0)
        wb = jnp.where(causal, sgw_ref[2 * j + 1], 0.0)
        wcat.append(jnp.concatenate([wa, wb], axis=1).astype(BF16))
    for c in range(tm // CHUNK):
        r0 = c * CHUNK
        for j in range(n_pairs):
            l0 = j * LANES
            vb = vn[r0:r0 + CHUNK, l0:l0 + LANES]
            rhs = jnp.concatenate([jnp.where(low_half, vb, 0.0),
                                   jnp.where(low_half, 0.0, vb)], axis=0).astype(BF16)
            s = _dot(wcat[j], rhs) + sgb_ref[:, l0:l0 + LANES]
            a_ref[r0:r0 + CHUNK, l0:l0 + LANES] = (u[r0:r0 + CHUNK, l0:l0 + LANES] * s).astype(BF16)

    lane_t = lax.broadcasted_iota(jnp.int32, (tm, LANES), 1)
    cos_t = cos_ref[...]
    sin_t = sin_ref[...]
    q_scale = QK_HEAD ** -0.5

    cqn = _rms(proj[:, o_cq:o_ckv], Q_LORA) * qln_ref[...]
    qf = _dot(cqn.astype(BF16), wuq_ref[...])
    ckvn = _rms(proj[:, o_ckv:o_kr], KV_LORA) * kvln_ref[...]
    ckvb = ckvn.astype(BF16)
    kf = _dot(ckvb, wuk_ref[...])
    v_ref[...] = _dot(ckvb, wuv_ref[...]).astype(BF16)
    kr = proj[:, o_kr:o_kr + LANES]
    kr_ss = jnp.sum(kr * kr, axis=-1, keepdims=True)

    qn_lo = qn_ref[:, :QK_NOPE]
    qn_hi = qn_ref[:, QK_NOPE:]
    kn_lo = kn_ref[:, :QK_NOPE]
    kn_hi = kn_ref[:, QK_NOPE:]
    for hd in range(MLA_HEADS):
        q_lo = qf[:, hd * QK_PAD:hd * QK_PAD + QK_NOPE]
        q_hi = qf[:, hd * QK_PAD + QK_NOPE:(hd + 1) * QK_PAD]
        ss = jnp.sum(q_lo * q_lo, axis=-1, keepdims=True) + jnp.sum(q_hi * q_hi, axis=-1, keepdims=True)
        rs = lax.rsqrt(ss * (1.0 / QK_HEAD) + EPS) * q_scale
        q_ref[:, hd * QK_PAD:hd * QK_PAD + QK_NOPE] = (q_lo * rs * qn_lo).astype(BF16)
        q_ref[:, hd * QK_PAD + QK_NOPE:(hd + 1) * QK_PAD] = _rope_rotate(
            q_hi * rs * qn_hi, cos_t, sin_t, lane_t).astype(BF16)

        k_lo = kf[:, hd * QK_NOPE:(hd + 1) * QK_NOPE]
        ss = jnp.sum(k_lo * k_lo, axis=-1, keepdims=True) + kr_ss
        rs = lax.rsqrt(ss * (1.0 / QK_HEAD) + EPS)
        k_ref[:, hd * QK_PAD:hd * QK_PAD + QK_NOPE] = (k_lo * rs * kn_lo).astype(BF16)
        k_ref[:, hd * QK_PAD + QK_NOPE:(hd + 1) * QK_PAD] = _rope_rotate(
            kr * rs * kn_hi, cos_t, sin_t, lane_t).astype(BF16)


def _mixer_pre(x2, mod_l, norm_mix, w_in, sgu_norm, sgu_w, sgu_b, q_lat_norm, kv_lat_norm,
               w_uq, w_ukv, q_norm, k_norm, cos_t, sin_t, seq):
    n, d = x2.shape
    tm = TM_MIX
    per_b = seq // tm
    win_p = jnp.pad(w_in, ((0, 0), (0, D_IN_PAD - w_in.shape[1]))).astype(BF16)
    wuq_p = jnp.pad(w_uq.reshape(Q_LORA, MLA_HEADS, QK_HEAD),
                    ((0, 0), (0, 0), (0, QK_PAD - QK_HEAD))).reshape(Q_LORA, MLA_HEADS * QK_PAD).astype(BF16)
    wukv = w_ukv.reshape(KV_LORA, MLA_HEADS, QK_NOPE + V_HEAD)
    wuk = wukv[:, :, :QK_NOPE].reshape(KV_LORA, MLA_HEADS * QK_NOPE).astype(BF16)
    wuv = wukv[:, :, QK_NOPE:].reshape(KV_LORA, MLA_WIDTH).astype(BF16)
    qn_p = jnp.pad(q_norm, (0, QK_PAD - QK_HEAD)).reshape(1, QK_PAD)
    kn_p = jnp.pad(k_norm, (0, QK_PAD - QK_HEAD)).reshape(1, QK_PAD)
    sgb_full = jnp.repeat(sgu_b.T, SGU_GROUP_DIM, axis=1)

    tok = lambda i: (i, 0)
    in_specs = [
        pl.BlockSpec((tm, d), tok),
        pl.BlockSpec((None, N_MOD, d), lambda i: (i // per_b, 0, 0)),
        _const_spec((1, d)),
        _const_spec(win_p.shape),
        _const_spec((1, SGU_WIDTH)),
        _const_spec(sgu_w.shape),
        _const_spec(sgb_full.shape),
        _const_spec((1, Q_LORA)),
        _const_spec((1, KV_LORA)),
        _const_spec(wuq_p.shape),
        _const_spec(wuk.shape),
        _const_spec(wuv.shape),
        _const_spec((1, QK_PAD)),
        _const_spec((1, QK_PAD)),
        pl.BlockSpec((tm, LANES), tok),
        pl.BlockSpec((tm, LANES), tok),
    ]
    out_shape = [
        jax.ShapeDtypeStruct((n, SGU_WIDTH), BF16),
        jax.ShapeDtypeStruct((n, MLA_HEADS * QK_PAD), BF16),
        jax.ShapeDtypeStruct((n, MLA_HEADS * QK_PAD), BF16),
        jax.ShapeDtypeStruct((n, MLA_WIDTH), BF16),
    ]
    out_specs = [pl.BlockSpec((tm, s.shape[1]), tok) for s in out_shape]
    return pl.pallas_call(
        _mixer_pre_kernel,
        grid=(n // tm,),
        in_specs=in_specs,
        out_specs=out_specs,
        out_shape=out_shape,
        compiler_params=_params(("parallel",), 48),
        name="mixer_pre",
    )(x2, mod_l, norm_mix.reshape(1, d), win_p, sgu_norm.reshape(1, SGU_WIDTH), sgu_w, sgb_full,
      q_lat_norm.reshape(1, Q_LORA), kv_lat_norm.reshape(1, KV_LORA), wuq_p, wuk, wuv, qn_p, kn_p,
      cos_t, sin_t)


def _attn_kernel(q_ref, k_ref, v_ref, o_ref):
    tq = q_ref.shape[0]
    qi = pl.program_id(2)
    q = q_ref[...]

    def step(ki, carry, diagonal):
        m, l, acc = carry
        k0 = pl.multiple_of(ki * tq, tq)
        kb = k_ref[pl.ds(k0, tq), :]
        vb = v_ref[pl.ds(k0, tq), :]
        s = lax.dot_general(q, kb, (((1,), (1,)), ((), ())), preferred_element_type=F32)
        if diagonal:
            row = lax.broadcasted_iota(jnp.int32, s.shape, 0)
            col = lax.broadcasted_iota(jnp.int32, s.shape, 1)
            s = jnp.where(col <= row, s, jnp.finfo(F32).min)
        m_new = jnp.maximum(m, jnp.max(s, axis=-1, keepdims=True))
        alpha = jnp.exp(m - m_new)
        p = jnp.exp(s - m_new)
        l_new = alpha * l + jnp.sum(p, axis=-1, keepdims=True)
        acc_new = alpha * acc + _dot(p.astype(BF16), vb)
        return m_new, l_new, acc_new

    init = (jnp.full((tq, 1), jnp.finfo(F32).min, F32),
            jnp.zeros((tq, 1), F32),
            jnp.zeros((tq, V_HEAD), F32))
    carry = lax.fori_loop(0, qi, lambda ki, c: step(ki, c, False), init)
    _, l, acc = step(qi, carry, True)
    o_ref[...] = (acc / l).astype(BF16)


def _attention(q, k, v, batch, seq):
    tq = TQ
    q3 = q.reshape(batch, seq, MLA_HEADS * QK_PAD)
    k3 = k.reshape(batch, seq, MLA_HEADS * QK_PAD)
    v3 = v.reshape(batch, seq, MLA_WIDTH)
    out = pl.pallas_call(
        _attn_kernel,
        grid=(batch, MLA_HEADS, seq // tq),
        in_specs=[
            pl.BlockSpec((None, tq, QK_PAD), lambda b, h, i: (b, i, h)),
            pl.BlockSpec((None, seq, QK_PAD), lambda b, h, i: (b, 0, h)),
            pl.BlockSpec((None, seq, V_HEAD), lambda b, h, i: (b, 0, h)),
        ],
        out_specs=pl.BlockSpec((None, tq, V_HEAD), lambda b, h, i: (b, i, h)),
        out_shape=jax.ShapeDtypeStruct((batch, seq, MLA_WIDTH), BF16),
        compiler_params=_params(("parallel", "parallel", "arbitrary"), 40),
        name="causal_attention",
    )(q3, k3, v3)
    return out.reshape(batch * seq, MLA_WIDTH)


def _mixer_post(x, a, o, mod_ref, wout_ref, nffn_ref):
    half = a.shape[1]
    y = _dot(a, wout_ref[:half, :]) + _dot(o, wout_ref[half:, :])
    x1 = x + mod_ref[2:3, :] * y
    h = _rms(x1, x1.shape[1]) * nffn_ref[...]
    h = h * (1.0 + mod_ref[4:5, :]) + mod_ref[3:4, :]
    return x1, h


def _dense_ffn_kernel(x_ref, a_ref, o_ref, mod_ref, wout_ref, nffn_ref, w1_ref, w3_ref, w2_ref,
                      out_ref):
    x1, h = _mixer_post(x_ref[...], a_ref[...], o_ref[...], mod_ref, wout_ref, nffn_ref)
    hb = h.astype(BF16)
    dff = w1_ref.shape[1]
    acc = jnp.zeros(x1.shape, F32)
    for c0 in range(0, dff, FF_CHUNK):
        h1 = _dot(hb, w1_ref[:, c0:c0 + FF_CHUNK])
        h3 = _dot(hb, w3_ref[:, c0:c0 + FF_CHUNK])
        act = (_silu(h1) * h3).astype(BF16)
        acc = acc + _dot(act, w2_ref[c0:c0 + FF_CHUNK, :])
    out_ref[...] = x1 + mod_ref[5:6, :] * acc


def _dense_layer_tail(x2, a, o, mod_l, w_out, norm_ffn, w1, w3, w2, seq):
    n, d = x2.shape
    tm = TM_FFN
    per_b = seq // tm
    tok = lambda i: (i, 0)
    single = pl.Buffered(1)
    wspec = lambda shape: pl.BlockSpec(shape, lambda i: (0, 0), pipeline_mode=single)
    dff = w1.shape[1]
    return pl.pallas_call(
        _dense_ffn_kernel,
        grid=(n // tm,),
        in_specs=[
            pl.BlockSpec((tm, d), tok),
            pl.BlockSpec((tm, a.shape[1]), tok),
            pl.BlockSpec((tm, o.shape[1]), tok),
            pl.BlockSpec((None, N_MOD, d), lambda i: (i // per_b, 0, 0)),
            wspec((d, d)),
            _const_spec((1, d)),
            wspec((d, dff)),
            wspec((d, dff)),
            wspec((dff, d)),
        ],
        out_specs=pl.BlockSpec((tm, d), tok),
        out_shape=jax.ShapeDtypeStruct((n, d), F32),
        compiler_params=_params(("parallel",), 56),
        name="dense_ffn",
    )(x2, a, o, mod_l, w_out.astype(BF16), norm_ffn.reshape(1, d),
      w1.astype(BF16), w3.astype(BF16), w2.astype(BF16))


def _router_kernel(x_ref, a_ref, o_ref, mod_ref, wout_ref, nffn_ref, rw_ref,
                   x1_ref, h_ref, info_ref, cnt_ref, carry_ref):
    @pl.when(pl.program_id(0) == 0)
    def _():
        carry_ref[...] = jnp.zeros(carry_ref.shape, F32)

    x1, h = _mixer_post(x_ref[...], a_ref[...], o_ref[...], mod_ref, wout_ref, nffn_ref)
    x1_ref[...] = x1
    h_ref[...] = h
    tm = h.shape[0]

    h_hi = h.astype(BF16)
    h_lo = (h - h_hi.astype(F32)).astype(BF16)
    rw = rw_ref[...]
    rw_hi = rw.astype(BF16)
    rw_lo = (rw - rw_hi.astype(F32)).astype(BF16)
    logits = _dot(h_hi, rw_hi) + _dot(h_hi, rw_lo) + _dot(h_lo, rw_hi)
    lt = logits.T[:N_EXPERTS, :]

    eid = lax.broadcasted_iota(jnp.int32, lt.shape, 0)
    m1 = jnp.max(lt, axis=0, keepdims=True)
    i1 = jnp.min(jnp.where(lt == m1, eid, N_EXPERTS), axis=0, keepdims=True)
    rest = jnp.where(eid == i1, -jnp.inf, lt)
    m2 = jnp.max(rest, axis=0, keepdims=True)
    i2 = jnp.min(jnp.where(rest == m2, eid, N_EXPERTS), axis=0, keepdims=True)
    e2 = jnp.exp(m2 - m1)
    g1 = 1.0 / (1.0 + e2)
    g2 = e2 / (1.0 + e2)

    oh1 = (eid == i1).astype(F32)
    oh2 = (eid == i2).astype(F32)
    r_io = lax.broadcasted_iota(jnp.int32, (tm, tm), 0)
    c_io = lax.broadcasted_iota(jnp.int32, (tm, tm), 1)
    before = jnp.where(r_io < c_io, 1.0, 0.0).astype(BF16)
    cnt1 = jnp.sum(oh1, axis=1, keepdims=True)
    cnt2 = jnp.sum(oh2, axis=1, keepdims=True)
    base = carry_ref[:, 0:1]
    rank1_e = _dot(oh1.astype(BF16), before) + base
    rank2_e = _dot(oh2.astype(BF16), before) + base + cnt1
    rank1 = jnp.sum(oh1 * rank1_e, axis=0, keepdims=True)
    rank2 = jnp.sum(oh2 * rank2_e, axis=0, keepdims=True)
    total = base + cnt1 + cnt2
    carry_ref[...] = jnp.broadcast_to(total, carry_ref.shape)
    cnt_ref[...] = jnp.broadcast_to(total, cnt_ref.shape)

    zero = jnp.zeros_like(g1)
    info_ref[0] = jnp.concatenate(
        [i1.astype(F32), i2.astype(F32), g1, g2, rank1, rank2, zero, zero], axis=0)


def _router(x2, a, o, mod_l, w_out, norm_ffn, router_w, seq):
    n, d = x2.shape
    tm = TM_ROUTE
    per_b = seq // tm
    tok = lambda i: (i, 0)
    rw_pad = jnp.pad(router_w, ((0, 0), (0, LANES - N_EXPERTS)))
    return pl.pallas_call(
        _router_kernel,
        grid=(n // tm,),
        in_specs=[
            pl.BlockSpec((tm, d), tok),
            pl.BlockSpec((tm, a.shape[1]), tok),
            pl.BlockSpec((tm, o.shape[1]), tok),
            pl.BlockSpec((None, N_MOD, d), lambda i: (i // per_b, 0, 0)),
            _const_spec((d, d)),
            _const_spec((1, d)),
            _const_spec((d, LANES)),
        ],
        out_specs=[
            pl.BlockSpec((tm, d), tok),
            pl.BlockSpec((tm, d), tok),
            pl.BlockSpec((1, 8, tm), lambda i: (i, 0, 0)),
            pl.BlockSpec((N_EXPERTS, LANES), lambda i: (0, 0)),
        ],
        out_shape=[
            jax.ShapeDtypeStruct((n, d), F32),
            jax.ShapeDtypeStruct((n, d), F32),
            jax.ShapeDtypeStruct((n // tm, 8, tm), F32),
            jax.ShapeDtypeStruct((N_EXPERTS, LANES), F32),
        ],
        scratch_shapes=[pltpu.VMEM((N_EXPERTS, LANES), F32)],
        compiler_params=_params(("arbitrary",), 48),
        name="mixer_post_router",
    )(x2, a, o, mod_l, w_out.astype(BF16), norm_ffn.reshape(1, d), rw_pad)


def _dispatch_kernel(lo_ref, hi_ref, p1_ref, p2_ref, h_hbm, xs_hbm, zbuf, sem, zsem):
    tm = p1_ref.shape[2]
    t0 = pl.program_id(0) * tm

    @pl.when(pl.program_id(0) == 0)
    def _():
        zbuf[...] = jnp.zeros(zbuf.shape, F32)

        def fill(r, _):
            pltpu.make_async_copy(zbuf.at[pl.ds(0, 1), :], xs_hbm.at[pl.ds(r, 1), :], zsem).start()
            return 0

        for g in range(lo_ref.shape[0]):
            lax.fori_loop(lo_ref[g], hi_ref[g], fill, 0)
        slack = N_EXPERTS * TG
        pltpu.make_async_copy(h_hbm.at[pl.ds(0, slack), :], xs_hbm.at[pl.ds(0, slack), :], zsem).wait()

    def issue(t, _):
        src = h_hbm.at[pl.ds(t0 + t, 1), :]
        pltpu.make_async_copy(src, xs_hbm.at[pl.ds(p1_ref[0, 0, t], 1), :], sem).start()
        pltpu.make_async_copy(src, xs_hbm.at[pl.ds(p2_ref[0, 0, t], 1), :], sem).start()
        return 0

    lax.fori_loop(0, tm, issue, 0)
    pltpu.make_async_copy(h_hbm.at[pl.ds(0, 2 * tm), :], xs_hbm.at[pl.ds(0, 2 * tm), :], sem).wait()


def _dispatch(h, pos1, pos2, pad_lo, pad_hi, rows):
    n, d = h.shape
    tm = TM_ROUTE
    idx_spec = pl.BlockSpec((1, 1, tm), lambda i, lo, hi: (i, 0, 0), memory_space=pltpu.SMEM)
    grid_spec = pltpu.PrefetchScalarGridSpec(
        num_scalar_prefetch=2,
        grid=(n // tm,),
        in_specs=[idx_spec, idx_spec, pl.BlockSpec(memory_space=pl.ANY)],
        out_specs=pl.BlockSpec(memory_space=pl.ANY),
        scratch_shapes=[pltpu.VMEM((8, d), F32), pltpu.SemaphoreType.DMA(()),
                        pltpu.SemaphoreType.DMA(())],
    )
    return pl.pallas_call(
        _dispatch_kernel,
        grid_spec=grid_spec,
        out_shape=jax.ShapeDtypeStruct((rows, d), F32),
        compiler_params=pltpu.CompilerParams(dimension_semantics=("arbitrary",),
                                             has_side_effects=True),
        name="moe_dispatch",
    )(pad_lo, pad_hi, pos1.reshape(n // tm, 1, tm), pos2.reshape(n // tm, 1, tm), h)


def _expert_kernel(te_ref, tv_ref, xs_ref, w1_ref, w3_ref, w2_ref, y_ref):
    i = pl.program_id(0)
    valid = tv_ref[i]

    @pl.when(valid > 0)
    def _():
        xb = xs_ref[...].astype(BF16)
        dff = w1_ref.shape[1]
        acc = jnp.zeros(xs_ref.shape, F32)
        for c0 in range(0, dff, FF_CHUNK):
            h1 = _dot(xb, w1_ref[:, c0:c0 + FF_CHUNK])
            h3 = _dot(xb, w3_ref[:, c0:c0 + FF_CHUNK])
            act = (_silu(h1) * h3).astype(BF16)
            acc = acc + _dot(act, w2_ref[c0:c0 + FF_CHUNK, :])
        y_ref[...] = acc

    @pl.when(valid <= 0)
    def _():
        y_ref[...] = jnp.zeros(y_ref.shape, F32)


def _experts(xs, tile_expert, tile_valid, w1, w3, w2):
    rows, d = xs.shape
    tg = TG
    dff = w1.shape[2]
    grid_spec = pltpu.PrefetchScalarGridSpec(
        num_scalar_prefetch=2,
        grid=(rows // tg,),
        in_specs=[
            pl.BlockSpec((tg, d), lambda i, te, tv: (i, 0)),
            pl.BlockSpec((None, d, dff), lambda i, te, tv: (te[i], 0, 0)),
            pl.BlockSpec((None, d, dff), lambda i, te, tv: (te[i], 0, 0)),
            pl.BlockSpec((None, dff, d), lambda i, te, tv: (te[i], 0, 0)),
        ],
        out_specs=pl.BlockSpec((tg, d), lambda i, te, tv: (i, 0)),
    )
    return pl.pallas_call(
        _expert_kernel,
        grid_spec=grid_spec,
        out_shape=jax.ShapeDtypeStruct((rows, d), F32),
        compiler_params=_params(("arbitrary",), 56),
        name="moe_experts",
    )(tile_expert, tile_valid, xs, w1.astype(BF16), w3.astype(BF16), w2.astype(BF16))


def _combine_kernel(p1_ref, p2_ref, x1_ref, gate_ref, mod_ref, y_hbm, out_ref, buf1, buf2, sem):
    tm = x1_ref.shape[0]

    def issue(t, _):
        pltpu.make_async_copy(y_hbm.at[pl.ds(p1_ref[0, 0, t], 1), :], buf1.at[pl.ds(t, 1), :], sem).start()
        pltpu.make_async_copy(y_hbm.at[pl.ds(p2_ref[0, 0, t], 1), :], buf2.at[pl.ds(t, 1), :], sem).start()
        return 0

    lax.fori_loop(0, tm, issue, 0)
    pltpu.make_async_copy(y_hbm.at[pl.ds(0, tm), :], buf1, sem).wait()
    pltpu.make_async_copy(y_hbm.at[pl.ds(0, tm), :], buf2, sem).wait()
    y = gate_ref[:, 2:3] * buf1[...] + gate_ref[:, 3:4] * buf2[...]
    out_ref[...] = x1_ref[...] + mod_ref[5:6, :] * y


def _combine(x1, y, pos1, pos2, gates, mod_l, seq):
    n, d = x1.shape
    tm = TM_COMB
    per_b = seq // tm
    tok = lambda i: (i, 0)
    idx_spec = pl.BlockSpec((1, 1, tm), lambda i: (i, 0, 0), memory_space=pltpu.SMEM)
    return pl.pallas_call(
        _combine_kernel,
        grid=(n // tm,),
        in_specs=[
            idx_spec, idx_spec,
            pl.BlockSpec((tm, d), tok),
            pl.BlockSpec((tm, gates.shape[1]), tok),
            pl.BlockSpec((None, N_MOD, d), lambda i: (i // per_b, 0, 0)),
            pl.BlockSpec(memory_space=pl.ANY),
        ],
        out_specs=pl.BlockSpec((tm, d), tok),
        out_shape=jax.ShapeDtypeStruct((n, d), F32),
        scratch_shapes=[pltpu.VMEM((tm, d), F32), pltpu.VMEM((tm, d), F32),
                        pltpu.SemaphoreType.DMA(())],
        compiler_params=_params(("arbitrary",), 40),
        name="moe_combine",
    )(pos1.reshape(n // tm, 1, tm), pos2.reshape(n // tm, 1, tm), x1, gates, mod_l, y)


def _moe_layer_tail(x2, a, o, mod_l, w_out, norm_ffn, router_w, w1, w3, w2, seq):
    n, d = x2.shape
    x1, h, info, counts = _router(x2, a, o, mod_l, w_out, norm_ffn, router_w, seq)

    info_t = jnp.transpose(info, (0, 2, 1)).reshape(n, 8)
    cnt = counts[:, 0].astype(jnp.int32)
    padded = ((cnt + TG - 1) // TG) * TG
    ends = jnp.cumsum(padded)
    starts = ends - padded
    e1 = info_t[:, 0].astype(jnp.int32)
    e2 = info_t[:, 1].astype(jnp.int32)
    pos1 = starts[e1] + info_t[:, 4].astype(jnp.int32)
    pos2 = starts[e2] + info_t[:, 5].astype(jnp.int32)
    rows = TOP_K * n + N_EXPERTS * TG
    tile_start = jnp.arange(rows // TG, dtype=jnp.int32) * TG
    tile_expert = jnp.minimum(jnp.searchsorted(ends, tile_start, side="right"), N_EXPERTS - 1).astype(jnp.int32)
    tile_valid = jnp.clip(starts[tile_expert] + cnt[tile_expert] - tile_start, 0, TG).astype(jnp.int32)

    pad_lo = jnp.concatenate([starts + cnt, ends[-1:]]).astype(jnp.int32)
    pad_hi = jnp.concatenate([ends, jnp.full((1,), rows, ends.dtype)]).astype(jnp.int32)

    xs = _dispatch(h, pos1, pos2, pad_lo, pad_hi, rows)
    y = _experts(xs, tile_expert, tile_valid, w1, w3, w2)
    return _combine(x1, y, pos1, pos2, info_t, mod_l, seq)


def kernel(x, c, positions, ada_w, ada_b, norm_mix, norm_ffn, w_in, sgu_norm, sgu_w, sgu_b,
           q_lat_norm, kv_lat_norm, w_uq, w_ukv, q_norm, k_norm, w_out,
           ffn_w1, ffn_w3, ffn_w2, router_w, moe_w1, moe_w3, moe_w2):
    batch, seq, d = x.shape
    depth = ada_w.shape[0]
    mod = _modulation(c, ada_w, ada_b)
    cos_t, sin_t = _rope_tables(positions)
    x2 = x.reshape(batch * seq, d)
    for layer in range(depth):
        a, q, k, v = _mixer_pre(x2, mod[layer], norm_mix[layer], w_in[layer], sgu_norm[layer],
                                sgu_w[layer], sgu_b[layer], q_lat_norm[layer], kv_lat_norm[layer],
                                w_uq[layer], w_ukv[layer], q_norm[layer], k_norm[layer],
                                cos_t, sin_t, seq)
        o = _attention(q, k, v, batch, seq)
        i = layer // 2
        if layer % 2 == 0:
            x2 = _dense_layer_tail(x2, a, o, mod[layer], w_out[layer], norm_ffn[layer],
                                   ffn_w1[i], ffn_w3[i], ffn_w2[i], seq)
        else:
            x2 = _moe_layer_tail(x2, a, o, mod[layer], w_out[layer], norm_ffn[layer], router_w[i],
                                 moe_w1[i], moe_w3[i], moe_w2[i], seq)
    return x2.reshape(batch, seq, d)
```

```python
import functools
import math

import jax
import jax.numpy as jnp
from jax import lax
from jax.experimental import pallas as pl
from jax.experimental.pallas import tpu as pltpu

F32 = jnp.float32
BF16 = jnp.bfloat16

EPS = 1e-6
ROPE_BASE = 10000.0
SGU_GROUPS = 8
SGU_GROUP_DIM = 64
SGU_WIDTH = SGU_GROUPS * SGU_GROUP_DIM
CHUNK = 128
MLA_HEADS = 4
QK_NOPE = 128
QK_ROPE = 64
QK_HEAD = QK_NOPE + QK_ROPE
QK_PAD = 256
V_HEAD = 128
MLA_WIDTH = MLA_HEADS * V_HEAD
Q_LORA = 256
KV_LORA = 128
N_MOD = 6
N_EXPERTS = 8
TOP_K = 2
LANES = 128
D_IN_PAD = 2 * SGU_WIDTH + Q_LORA + KV_LORA + LANES

MIB = 1024 * 1024

TM_MIX = 512
TQ = 512
TM_FFN = 512
TM_ROUTE = 512
TG = 256
TM_COMB = 256
FF_CHUNK = 1408


def _dot(a, b):
    return jnp.dot(a, b, preferred_element_type=F32)


def _gelu_tanh(x):
    return 0.5 * x * (1.0 + jnp.tanh(math.sqrt(2.0 / math.pi) * (x + 0.044715 * (x * x * x))))


def _silu(x):
    return x * (1.0 / (1.0 + jnp.exp(-x)))


def _rms(x, width):
    return x * lax.rsqrt(jnp.sum(x * x, axis=-1, keepdims=True) * (1.0 / width) + EPS)


def _params(sem, vmem_mib):
    return pltpu.CompilerParams(dimension_semantics=sem, vmem_limit_bytes=vmem_mib * MIB)


def _const_spec(shape):
    nd = len(shape)
    return pl.BlockSpec(shape, lambda *_: (0,) * nd)


def _mod_kernel(c_ref, w_ref, b_ref, o_ref):
    c = c_ref[...]
    ca = _silu(c)
    ca_hi = ca.astype(BF16)
    ca_lo = (ca - ca_hi.astype(F32)).astype(BF16)
    w = w_ref[...]
    w_hi = w.astype(BF16)
    w_lo = (w - w_hi.astype(F32)).astype(BF16)
    acc = _dot(ca_hi, w_hi) + _dot(ca_hi, w_lo) + _dot(ca_lo, w_hi)
    o_ref[...] = acc + b_ref[...]


def _modulation(c, ada_w, ada_b):
    L, D, W = ada_w.shape
    B = c.shape[0]
    rows = 16
    c_pad = jnp.zeros((rows, D), F32).at[:B].set(c)
    tn = 1536
    out = pl.pallas_call(
        _mod_kernel,
        grid=(L, W // tn),
        in_specs=[
            pl.BlockSpec((rows, D), lambda l, j: (0, 0)),
            pl.BlockSpec((None, D, tn), lambda l, j: (l, 0, j)),
            pl.BlockSpec((None, 1, tn), lambda l, j: (l, 0, j)),
        ],
        out_specs=pl.BlockSpec((None, rows, tn), lambda l, j: (l, 0, j)),
        out_shape=jax.ShapeDtypeStruct((L, rows, W), F32),
        compiler_params=_params(("parallel", "parallel"), 40),
        name="adaln_mod",
    )(c_pad, ada_w, ada_b.reshape(L, 1, W))
    return out[:, :B].reshape(L, B, N_MOD, D)


def _rope_kernel(pos_ref, inv_ref, cos_ref, sin_ref):
    pos = pos_ref[0].astype(F32)
    ang = inv_ref[...] * pos
    co = jnp.cos(ang)
    si = jnp.sin(ang)
    z = jnp.zeros((2 * co.shape[0], co.shape[1]), F32)
    cos_ref[...] = jnp.concatenate([co, co, z], axis=0).T
    sin_ref[...] = jnp.concatenate([-si, si, z], axis=0).T


def _rope_tables(positions):
    n = positions.size
    tn = 512
    half = QK_ROPE // 2
    inv_freq = 1.0 / (ROPE_BASE ** (jnp.arange(0, QK_ROPE, 2, dtype=F32) / QK_ROPE))
    pos3 = positions.reshape(n // tn, 1, tn)
    return pl.pallas_call(
        _rope_kernel,
        grid=(n // tn,),
        in_specs=[
            pl.BlockSpec((1, 1, tn), lambda i: (i, 0, 0)),
            pl.BlockSpec((half, 1), lambda i: (0, 0)),
        ],
        out_specs=[pl.BlockSpec((tn, LANES), lambda i: (i, 0))] * 2,
        out_shape=[jax.ShapeDtypeStruct((n, LANES), F32)] * 2,
        compiler_params=_params(("parallel",), 32),
        name="rope_tables",
    )(pos3, inv_freq.reshape(half, 1))


def _rope_rotate(r, cos_t, sin_t, lane):
    partner = jnp.where(lane < QK_ROPE // 2,
                        pltpu.roll(r, LANES - QK_ROPE // 2, 1),
                        pltpu.roll(r, QK_ROPE // 2, 1))
    return r * cos_t + partner * sin_t


def _mixer_pre_kernel(x_ref, mod_ref, nmix_ref, win_ref, sgn_ref, sgw_ref, sgb_ref,
                      qln_ref, kvln_ref, wuq_ref, wuk_ref, wuv_ref, qn_ref, kn_ref,
                      cos_ref, sin_ref, a_ref, q_ref, k_ref, v_ref):
    tm, d = x_ref.shape
    x = x_ref[...]
    shift = mod_ref[0:1, :]
    scale = mod_ref[1:2, :]
    h = _rms(x, d) * nmix_ref[...]
    h = h * (1.0 + scale) + shift
    proj = _dot(h.astype(BF16), win_ref[...])

    o_zv = SGU_WIDTH
    o_cq = 2 * SGU_WIDTH
    o_ckv = o_cq + Q_LORA
    o_kr = o_ckv + KV_LORA

    u = _gelu_tanh(proj[:, :SGU_WIDTH])
    gv = _gelu_tanh(proj[:, o_zv:o_cq])
    mu = jnp.mean(gv, axis=-1, keepdims=True)
    cen = gv - mu
    var = jnp.mean(cen * cen, axis=-1, keepdims=True)
    vn = cen * lax.rsqrt(var + EPS) * sgn_ref[...]

    row = lax.broadcasted_iota(jnp.int32, (CHUNK, CHUNK), 0)
    col = lax.broadcasted_iota(jnp.int32, (CHUNK, CHUNK), 1)
    causal = col <= row
    lane = lax.broadcasted_iota(jnp.int32, (CHUNK, LANES), 1)
    low_half = lane < SGU_GROUP_DIM
    n_pairs = SGU_GROUPS // 2
    wcat = []
    for j in range(n_pairs):
        wa = jnp.where(causal, sgw_ref[2 * j], 0.0)
        wb = jnp.where(causal, sgw_ref[2 * j + 1], 0.0)
        wcat.append(jnp.concatenate([wa, wb], axis=1).astype(BF16))
    for c in range(tm // CHUNK):
        r0 = c * CHUNK
        for j in range(n_pairs):
            l0 = j * LANES
            vb = vn[r0:r0 + CHUNK, l0:l0 + LANES]
            rhs = jnp.concatenate([jnp.where(low_half, vb, 0.0),
                                   jnp.where(low_half, 0.0, vb)], axis=0).astype(BF16)
            s = _dot(wcat[j], rhs) + sgb_ref[:, l0:l0 + LANES]
            a_ref[r0:r0 + CHUNK, l0:l0 + LANES] = (u[r0:r0 + CHUNK, l0:l0 + LANES] * s).astype(BF16)

    lane_t = lax.broadcasted_iota(jnp.int32, (tm, LANES), 1)
    cos_t = cos_ref[...]
    sin_t = sin_ref[...]
    q_scale = QK_HEAD ** -0.5

    cqn = _rms(proj[:, o_cq:o_ckv], Q_LORA) * qln_ref[...]
    qf = _dot(cqn.astype(BF16), wuq_ref[...])
    ckvn = _rms(proj[:, o_ckv:o_kr], KV_LORA) * kvln_ref[...]
    ckvb = ckvn.astype(BF16)
    kf = _dot(ckvb, wuk_ref[...])
    v_ref[...] = _dot(ckvb, wuv_ref[...]).astype(BF16)
    kr = proj[:, o_kr:o_kr + LANES]
    kr_ss = jnp.sum(kr * kr, axis=-1, keepdims=True)

    qn_lo = qn_ref[:, :QK_NOPE]
    qn_hi = qn_ref[:, QK_NOPE:]
    kn_lo = kn_ref[:, :QK_NOPE]
    kn_hi = kn_ref[:, QK_NOPE:]
    for hd in range(MLA_HEADS):
        q_lo = qf[:, hd * QK_PAD:hd * QK_PAD + QK_NOPE]
        q_hi = qf[:, hd * QK_PAD + QK_NOPE:(hd + 1) * QK_PAD]
        ss = jnp.sum(q_lo * q_lo, axis=-1, keepdims=True) + jnp.sum(q_hi * q_hi, axis=-1, keepdims=True)
        rs = lax.rsqrt(ss * (1.0 / QK_HEAD) + EPS) * q_scale
        q_ref[:, hd * QK_PAD:hd * QK_PAD + QK_NOPE] = (q_lo * rs * qn_lo).astype(BF16)
        q_ref[:, hd * QK_PAD + QK_NOPE:(hd + 1) * QK_PAD] = _rope_rotate(
            q_hi * rs * qn_hi, cos_t, sin_t, lane_t).astype(BF16)

        k_lo = kf[:, hd * QK_NOPE:(hd + 1) * QK_NOPE]
        ss = jnp.sum(k_lo * k_lo, axis=-1, keepdims=True) + kr_ss
        rs = lax.rsqrt(ss * (1.0 / QK_HEAD) + EPS)
        k_ref[:, hd * QK_PAD:hd * QK_PAD + QK_NOPE] = (k_lo * rs * kn_lo).astype(BF16)
        k_ref[:, hd * QK_PAD + QK_NOPE:(hd + 1) * QK_PAD] = _rope_rotate(
            kr * rs * kn_hi, cos_t, sin_t, lane_t).astype(BF16)


def _mixer_pre(x2, mod_l, norm_mix, w_in, sgu_norm, sgu_w, sgu_b, q_lat_norm, kv_lat_norm,
               w_uq, w_ukv, q_norm, k_norm, cos_t, sin_t, seq):
    n, d = x2.shape
    tm = TM_MIX
    per_b = seq // tm
    win_p = jnp.pad(w_in, ((0, 0), (0, D_IN_PAD - w_in.shape[1]))).astype(BF16)
    wuq_p = jnp.pad(w_uq.reshape(Q_LORA, MLA_HEADS, QK_HEAD),
                    ((0, 0), (0, 0), (0, QK_PAD - QK_HEAD))).reshape(Q_LORA, MLA_HEADS * QK_PAD).astype(BF16)
    wukv = w_ukv.reshape(KV_LORA, MLA_HEADS, QK_NOPE + V_HEAD)
    wuk = wukv[:, :, :QK_NOPE].reshape(KV_LORA, MLA_HEADS * QK_NOPE).astype(BF16)
    wuv = wukv[:, :, QK_NOPE:].reshape(KV_LORA, MLA_WIDTH).astype(BF16)
    qn_p = jnp.pad(q_norm, (0, QK_PAD - QK_HEAD)).reshape(1, QK_PAD)
    kn_p = jnp.pad(k_norm, (0, QK_PAD - QK_HEAD)).reshape(1, QK_PAD)
    sgb_full = jnp.repeat(sgu_b.T, SGU_GROUP_DIM, axis=1)

    tok = lambda i: (i, 0)
    in_specs = [
        pl.BlockSpec((tm, d), tok),
        pl.BlockSpec((None, N_MOD, d), lambda i: (i // per_b, 0, 0)),
        _const_spec((1, d)),
        _const_spec(win_p.shape),
        _const_spec((1, SGU_WIDTH)),
        _const_spec(sgu_w.shape),
        _const_spec(sgb_full.shape),
        _const_spec((1, Q_LORA)),
        _const_spec((1, KV_LORA)),
        _const_spec(wuq_p.shape),
        _const_spec(wuk.shape),
        _const_spec(wuv.shape),
        _const_spec((1, QK_PAD)),
        _const_spec((1, QK_PAD)),
        pl.BlockSpec((tm, LANES), tok),
        pl.BlockSpec((tm, LANES), tok),
    ]
    out_shape = [
        jax.ShapeDtypeStruct((n, SGU_WIDTH), BF16),
        jax.ShapeDtypeStruct((n, MLA_HEADS * QK_PAD), BF16),
        jax.ShapeDtypeStruct((n, MLA_HEADS * QK_PAD), BF16),
        jax.ShapeDtypeStruct((n, MLA_WIDTH), BF16),
    ]
    out_specs = [pl.BlockSpec((tm, s.shape[1]), tok) for s in out_shape]
    return pl.pallas_call(
        _mixer_pre_kernel,
        grid=(n // tm,),
        in_specs=in_specs,
        out_specs=out_specs,
        out_shape=out_shape,
        compiler_params=_params(("parallel",), 48),
        name="mixer_pre",
    )(x2, mod_l, norm_mix.reshape(1, d), win_p, sgu_norm.reshape(1, SGU_WIDTH), sgu_w, sgb_full,
      q_lat_norm.reshape(1, Q_LORA), kv_lat_norm.reshape(1, KV_LORA), wuq_p, wuk, wuv, qn_p, kn_p,
      cos_t, sin_t)


def _attn_kernel(q_ref, k_ref, v_ref, o_ref):
    tq = q_ref.shape[0]
    qi = pl.program_id(2)
    q = q_ref[...]

    def step(ki, carry, diagonal):
        m, l, acc = carry
        k0 = pl.multiple_of(ki * tq, tq)
        kb = k_ref[pl.ds(k0, tq), :]
        vb = v_ref[pl.ds(k0, tq), :]
        s = lax.dot_general(q, kb, (((1,), (1,)), ((), ())), preferred_element_type=F32)
        if diagonal:
            row = lax.broadcasted_iota(jnp.int32, s.shape, 0)
            col = lax.broadcasted_iota(jnp.int32, s.shape, 1)
            s = jnp.where(col <= row, s, jnp.finfo(F32).min)
        m_new = jnp.maximum(m, jnp.max(s, axis=-1, keepdims=True))
        alpha = jnp.exp(m - m_new)
        p = jnp.exp(s - m_new)
        l_new = alpha * l + jnp.sum(p, axis=-1, keepdims=True)
        acc_new = alpha * acc + _dot(p.astype(BF16), vb)
        return m_new, l_new, acc_new

    init = (jnp.full((tq, 1), jnp.finfo(F32).min, F32),
            jnp.zeros((tq, 1), F32),
            jnp.zeros((tq, V_HEAD), F32))
    carry = lax.fori_loop(0, qi, lambda ki, c: step(ki, c, False), init)
    _, l, acc = step(qi, carry, True)
    o_ref[...] = (acc / l).astype(BF16)


def _attention(q, k, v, batch, seq):
    tq = TQ
    q3 = q.reshape(batch, seq, MLA_HEADS * QK_PAD)
    k3 = k.reshape(batch, seq, MLA_HEADS * QK_PAD)
    v3 = v.reshape(batch, seq, MLA_WIDTH)
    out = pl.pallas_call(
        _attn_kernel,
        grid=(batch, MLA_HEADS, seq // tq),
        in_specs=[
            pl.BlockSpec((None, tq, QK_PAD), lambda b, h, i: (b, i, h)),
            pl.BlockSpec((None, seq, QK_PAD), lambda b, h, i: (b, 0, h)),
            pl.BlockSpec((None, seq, V_HEAD), lambda b, h, i: (b, 0, h)),
        ],
        out_specs=pl.BlockSpec((None, tq, V_HEAD), lambda b, h, i: (b, i, h)),
        out_shape=jax.ShapeDtypeStruct((batch, seq, MLA_WIDTH), BF16),
        compiler_params=_params(("parallel", "parallel", "arbitrary"), 40),
        name="causal_attention",
    )(q3, k3, v3)
    return out.reshape(batch * seq, MLA_WIDTH)


def _mixer_post(x, a, o, mod_ref, wout_ref, nffn_ref):
    half = a.shape[1]
    y = _dot(a, wout_ref[:half, :]) + _dot(o, wout_ref[half:, :])
    x1 = x + mod_ref[2:3, :] * y
    h = _rms(x1, x1.shape[1]) * nffn_ref[...]
    h = h * (1.0 + mod_ref[4:5, :]) + mod_ref[3:4, :]
    return x1, h


def _dense_ffn_kernel(x_ref, a_ref, o_ref, mod_ref, wout_ref, nffn_ref, w1_ref, w3_ref, w2_ref,
                      out_ref):
    x1, h = _mixer_post(x_ref[...], a_ref[...], o_ref[...], mod_ref, wout_ref, nffn_ref)
    hb = h.astype(BF16)
    dff = w1_ref.shape[1]
    acc = jnp.zeros(x1.shape, F32)
    for c0 in range(0, dff, FF_CHUNK):
        h1 = _dot(hb, w1_ref[:, c0:c0 + FF_CHUNK])
        h3 = _dot(hb, w3_ref[:, c0:c0 + FF_CHUNK])
        act = (_silu(h1) * h3).astype(BF16)
        acc = acc + _dot(act, w2_ref[c0:c0 + FF_CHUNK, :])
    out_ref[...] = x1 + mod_ref[5:6, :] * acc


def _dense_layer_tail(x2, a, o, mod_l, w_out, norm_ffn, w1, w3, w2, seq):
    n, d = x2.shape
    tm = TM_FFN
    per_b = seq // tm
    tok = lambda i: (i, 0)
    single = pl.Buffered(1)
    wspec = lambda shape: pl.BlockSpec(shape, lambda i: (0, 0), pipeline_mode=single)
    dff = w1.shape[1]
    return pl.pallas_call(
        _dense_ffn_kernel,
        grid=(n // tm,),
        in_specs=[
            pl.BlockSpec((tm, d), tok),
            pl.BlockSpec((tm, a.shape[1]), tok),
            pl.BlockSpec((tm, o.shape[1]), tok),
            pl.BlockSpec((None, N_MOD, d), lambda i: (i // per_b, 0, 0)),
            wspec((d, d)),
            _const_spec((1, d)),
            wspec((d, dff)),
            wspec((d, dff)),
            wspec((dff, d)),
        ],
        out_specs=pl.BlockSpec((tm, d), tok),
        out_shape=jax.ShapeDtypeStruct((n, d), F32),
        compiler_params=_params(("parallel",), 56),
        name="dense_ffn",
    )(x2, a, o, mod_l, w_out.astype(BF16), norm_ffn.reshape(1, d),
      w1.astype(BF16), w3.astype(BF16), w2.astype(BF16))


def _router_kernel(x_ref, a_ref, o_ref, mod_ref, wout_ref, nffn_ref, rw_ref,
                   x1_ref, h_ref, info_ref, cnt_ref, carry_ref):
    @pl.when(pl.program_id(0) == 0)
    def _():
        carry_ref[...] = jnp.zeros(carry_ref.shape, F32)

    x1, h = _mixer_post(x_ref[...], a_ref[...], o_ref[...], mod_ref, wout_ref, nffn_ref)
    x1_ref[...] = x1
    h_ref[...] = h
    tm = h.shape[0]

    h_hi = h.astype(BF16)
    h_lo = (h - h_hi.astype(F32)).astype(BF16)
    rw = rw_ref[...]
    rw_hi = rw.astype(BF16)
    rw_lo = (rw - rw_hi.astype(F32)).astype(BF16)
    logits = _dot(h_hi, rw_hi) + _dot(h_hi, rw_lo) + _dot(h_lo, rw_hi)
    lt = logits.T[:N_EXPERTS, :]

    eid = lax.broadcasted_iota(jnp.int32, lt.shape, 0)
    m1 = jnp.max(lt, axis=0, keepdims=True)
    i1 = jnp.min(jnp.where(lt == m1, eid, N_EXPERTS), axis=0, keepdims=True)
    rest = jnp.where(eid == i1, -jnp.inf, lt)
    m2 = jnp.max(rest, axis=0, keepdims=True)
    i2 = jnp.min(jnp.where(rest == m2, eid, N_EXPERTS), axis=0, keepdims=True)
    e2 = jnp.exp(m2 - m1)
    g1 = 1.0 / (1.0 + e2)
    g2 = e2 / (1.0 + e2)

    oh1 = (eid == i1).astype(F32)
    oh2 = (eid == i2).astype(F32)
    r_io = lax.broadcasted_iota(jnp.int32, (tm, tm), 0)
    c_io = lax.broadcasted_iota(jnp.int32, (tm, tm), 1)
    before = jnp.where(r_io < c_io, 1.0, 0.0).astype(BF16)
    cnt1 = jnp.sum(oh1, axis=1, keepdims=True)
    cnt2 = jnp.sum(oh2, axis=1, keepdims=True)
    base = carry_ref[:, 0:1]
    rank1_e = _dot(oh1.astype(BF16), before) + base
    rank2_e = _dot(oh2.astype(BF16), before) + base + cnt1
    rank1 = jnp.sum(oh1 * rank1_e, axis=0, keepdims=True)
    rank2 = jnp.sum(oh2 * rank2_e, axis=0, keepdims=True)
    total = base + cnt1 + cnt2
    carry_ref[...] = jnp.broadcast_to(total, carry_ref.shape)
    cnt_ref[...] = jnp.broadcast_to(total, cnt_ref.shape)

    zero = jnp.zeros_like(g1)
    info_ref[0] = jnp.concatenate(
        [i1.astype(F32), i2.astype(F32), g1, g2, rank1, rank2, zero, zero], axis=0)


def _router(x2, a, o, mod_l, w_out, norm_ffn, router_w, seq):
    n, d = x2.shape
    tm = TM_ROUTE
    per_b = seq // tm
    tok = lambda i: (i, 0)
    rw_pad = jnp.pad(router_w, ((0, 0), (0, LANES - N_EXPERTS)))
    return pl.pallas_call(
        _router_kernel,
        grid=(n // tm,),
        in_specs=[
            pl.BlockSpec((tm, d), tok),
            pl.BlockSpec((tm, a.shape[1]), tok),
            pl.BlockSpec((tm, o.shape[1]), tok),
            pl.BlockSpec((None, N_MOD, d), lambda i: (i // per_b, 0, 0)),
            _const_spec((d, d)),
            _const_spec((1, d)),
            _const_spec((d, LANES)),
        ],
        out_specs=[
            pl.BlockSpec((tm, d), tok),
            pl.BlockSpec((tm, d), tok),
            pl.BlockSpec((1, 8, tm), lambda i: (i, 0, 0)),
            pl.BlockSpec((N_EXPERTS, LANES), lambda i: (0, 0)),
        ],
        out_shape=[
            jax.ShapeDtypeStruct((n, d), F32),
            jax.ShapeDtypeStruct((n, d), F32),
            jax.ShapeDtypeStruct((n // tm, 8, tm), F32),
            jax.ShapeDtypeStruct((N_EXPERTS, LANES), F32),
        ],
        scratch_shapes=[pltpu.VMEM((N_EXPERTS, LANES), F32)],
        compiler_params=_params(("arbitrary",), 48),
        name="mixer_post_router",
    )(x2, a, o, mod_l, w_out.astype(BF16), norm_ffn.reshape(1, d), rw_pad)


def _dispatch_kernel(lo_ref, hi_ref, p1_ref, p2_ref, h_ref, xs_hbm, zbuf, sem, zsem):
    tm = p1_ref.shape[2]

    @pl.when(pl.program_id(0) == 0)
    def _():
        zbuf[...] = jnp.zeros(zbuf.shape, F32)

        def fill(r, _):
            pltpu.make_async_copy(zbuf.at[pl.ds(0, 1), :], xs_hbm.at[pl.ds(r, 1), :], zsem).start()
            return 0

        for g in range(lo_ref.shape[0]):
            lax.fori_loop(lo_ref[g], hi_ref[g], fill, 0)
        slack = N_EXPERTS * TG
        pltpu.make_async_copy(xs_hbm.at[pl.ds(0, slack), :], xs_hbm.at[pl.ds(0, slack), :], zsem).wait()

    def issue(t, _):
        src = h_ref.at[pl.ds(t, 1), :]
        pltpu.make_async_copy(src, xs_hbm.at[pl.ds(p1_ref[0, 0, t], 1), :], sem).start()
        pltpu.make_async_copy(src, xs_hbm.at[pl.ds(p2_ref[0, 0, t], 1), :], sem).start()
        return 0

    lax.fori_loop(0, tm, issue, 0)
    for _ in range(TOP_K):
        pltpu.make_async_copy(h_ref, xs_hbm.at[pl.ds(0, tm), :], sem).wait()


def _dispatch(h, pos1, pos2, pad_lo, pad_hi, rows):
    n, d = h.shape
    tm = TM_ROUTE
    idx_spec = pl.BlockSpec((1, 1, tm), lambda i, lo, hi: (i, 0, 0), memory_space=pltpu.SMEM)
    grid_spec = pltpu.PrefetchScalarGridSpec(
        num_scalar_prefetch=2,
        grid=(n // tm,),
        in_specs=[idx_spec, idx_spec, pl.BlockSpec((tm, d), lambda i, lo, hi: (i, 0))],
        out_specs=pl.BlockSpec(memory_space=pl.ANY),
        scratch_shapes=[pltpu.VMEM((8, d), F32), pltpu.SemaphoreType.DMA(()),
                        pltpu.SemaphoreType.DMA(())],
    )
    return pl.pallas_call(
        _dispatch_kernel,
        grid_spec=grid_spec,
        out_shape=jax.ShapeDtypeStruct((rows, d), F32),
        compiler_params=pltpu.CompilerParams(dimension_semantics=("arbitrary",),
                                             has_side_effects=True),
        name="moe_dispatch",
    )(pad_lo, pad_hi, pos1.reshape(n // tm, 1, tm), pos2.reshape(n // tm, 1, tm), h)


def _expert_kernel(te_ref, tv_ref, xs_ref, w1_ref, w3_ref, w2_ref, y_ref):
    i = pl.program_id(0)
    valid = tv_ref[i]

    @pl.when(valid > 0)
    def _():
        xb = xs_ref[...].astype(BF16)
        dff = w1_ref.shape[1]
        acc = jnp.zeros(xs_ref.shape, F32)
        for c0 in range(0, dff, FF_CHUNK):
            h1 = _dot(xb, w1_ref[:, c0:c0 + FF_CHUNK])
            h3 = _dot(xb, w3_ref[:, c0:c0 + FF_CHUNK])
            act = (_silu(h1) * h3).astype(BF16)
            acc = acc + _dot(act, w2_ref[c0:c0 + FF_CHUNK, :])
        y_ref[...] = acc

    @pl.when(valid <= 0)
    def _():
        y_ref[...] = jnp.zeros(y_ref.shape, F32)


def _experts(xs, tile_expert, tile_valid, w1, w3, w2):
    rows, d = xs.shape
    tg = TG
    dff = w1.shape[2]
    grid_spec = pltpu.PrefetchScalarGridSpec(
        num_scalar_prefetch=2,
        grid=(rows // tg,),
        in_specs=[
            pl.BlockSpec((tg, d), lambda i, te, tv: (i, 0)),
            pl.BlockSpec((None, d, dff), lambda i, te, tv: (te[i], 0, 0)),
            pl.BlockSpec((None, d, dff), lambda i, te, tv: (te[i], 0, 0)),
            pl.BlockSpec((None, dff, d), lambda i, te, tv: (te[i], 0, 0)),
        ],
        out_specs=pl.BlockSpec((tg, d), lambda i, te, tv: (i, 0)),
    )
    return pl.pallas_call(
        _expert_kernel,
        grid_spec=grid_spec,
        out_shape=jax.ShapeDtypeStruct((rows, d), F32),
        compiler_params=_params(("arbitrary",), 56),
        name="moe_experts",
    )(tile_expert, tile_valid, xs, w1.astype(BF16), w3.astype(BF16), w2.astype(BF16))


def _combine_kernel(p1_ref, p2_ref, x1_ref, gate_ref, mod_ref, y_hbm, out_ref, buf1, buf2, sem):
    tm = x1_ref.shape[0]

    def issue(t, _):
        pltpu.make_async_copy(y_hbm.at[pl.ds(p1_ref[0, 0, t], 1), :], buf1.at[pl.ds(t, 1), :], sem).start()
        pltpu.make_async_copy(y_hbm.at[pl.ds(p2_ref[0, 0, t], 1), :], buf2.at[pl.ds(t, 1), :], sem).start()
        return 0

    lax.fori_loop(0, tm, issue, 0)
    pltpu.make_async_copy(y_hbm.at[pl.ds(0, tm), :], buf1, sem).wait()
    pltpu.make_async_copy(y_hbm.at[pl.ds(0, tm), :], buf2, sem).wait()
    y = gate_ref[:, 2:3] * buf1[...] + gate_ref[:, 3:4] * buf2[...]
    out_ref[...] = x1_ref[...] + mod_ref[5:6, :] * y


def _combine(x1, y, pos1, pos2, gates, mod_l, seq):
    n, d = x1.shape
    tm = TM_COMB
    per_b = seq // tm
    tok = lambda i: (i, 0)
    idx_spec = pl.BlockSpec((1, 1, tm), lambda i: (i, 0, 0), memory_space=pltpu.SMEM)
    return pl.pallas_call(
        _combine_kernel,
        grid=(n // tm,),
        in_specs=[
            idx_spec, idx_spec,
            pl.BlockSpec((tm, d), tok),
            pl.BlockSpec((tm, gates.shape[1]), tok),
            pl.BlockSpec((None, N_MOD, d), lambda i: (i // per_b, 0, 0)),
            pl.BlockSpec(memory_space=pl.ANY),
        ],
        out_specs=pl.BlockSpec((tm, d), tok),
        out_shape=jax.ShapeDtypeStruct((n, d), F32),
        scratch_shapes=[pltpu.VMEM((tm, d), F32), pltpu.VMEM((tm, d), F32),
                        pltpu.SemaphoreType.DMA(())],
        compiler_params=_params(("arbitrary",), 40),
        name="moe_combine",
    )(pos1.reshape(n // tm, 1, tm), pos2.reshape(n // tm, 1, tm), x1, gates, mod_l, y)


def _moe_layer_tail(x2, a, o, mod_l, w_out, norm_ffn, router_w, w1, w3, w2, seq):
    n, d = x2.shape
    x1, h, info, counts = _router(x2, a, o, mod_l, w_out, norm_ffn, router_w, seq)

    info_t = jnp.transpose(info, (0, 2, 1)).reshape(n, 8)
    cnt = counts[:, 0].astype(jnp.int32)
    padded = ((cnt + TG - 1) // TG) * TG
    ends = jnp.cumsum(padded)
    starts = ends - padded
    e1 = info_t[:, 0].astype(jnp.int32)
    e2 = info_t[:, 1].astype(jnp.int32)
    pos1 = starts[e1] + info_t[:, 4].astype(jnp.int32)
    pos2 = starts[e2] + info_t[:, 5].astype(jnp.int32)
    rows = TOP_K * n + N_EXPERTS * TG
    tile_start = jnp.arange(rows // TG, dtype=jnp.int32) * TG
    tile_expert = jnp.minimum(jnp.sum((ends[None, :] <= tile_start[:, None]).astype(jnp.int32), axis=1),
                              N_EXPERTS - 1)
    tile_valid = jnp.clip(starts[tile_expert] + cnt[tile_expert] - tile_start, 0, TG).astype(jnp.int32)

    pad_lo = jnp.concatenate([starts + cnt, ends[-1:]]).astype(jnp.int32)
    pad_hi = jnp.concatenate([ends, jnp.full((1,), rows, ends.dtype)]).astype(jnp.int32)

    xs = _dispatch(h, pos1, pos2, pad_lo, pad_hi, rows)
    y = _experts(xs, tile_expert, tile_valid, w1, w3, w2)
    return _combine(x1, y, pos1, pos2, info_t, mod_l, seq)


def kernel(x, c, positions, ada_w, ada_b, norm_mix, norm_ffn, w_in, sgu_norm, sgu_w, sgu_b,
           q_lat_norm, kv_lat_norm, w_uq, w_ukv, q_norm, k_norm, w_out,
           ffn_w1, ffn_w3, ffn_w2, router_w, moe_w1, moe_w3, moe_w2):
    batch, seq, d = x.shape
    depth = ada_w.shape[0]
    mod = _modulation(c, ada_w, ada_b)
    cos_t, sin_t = _rope_tables(positions)
    x2 = x.reshape(batch * seq, d)
    for layer in range(depth):
        a, q, k, v = _mixer_pre(x2, mod[layer], norm_mix[layer], w_in[layer], sgu_norm[layer],
                                sgu_w[layer], sgu_b[layer], q_lat_norm[layer], kv_lat_norm[layer],
                                w_uq[layer], w_ukv[layer], q_norm[layer], k_norm[layer],
                                cos_t, sin_t, seq)
        o = _attention(q, k, v, batch, seq)
        i = layer // 2
        if layer % 2 == 0:
            x2 = _dense_layer_tail(x2, a, o, mod[layer], w_out[layer], norm_ffn[layer],
                                   ffn_w1[i], ffn_w3[i], ffn_w2[i], seq)
        else:
            x2 = _moe_layer_tail(x2, a, o, mod[layer], w_out[layer], norm_ffn[layer], router_w[i],
                                 moe_w1[i], moe_w3[i], moe_w2[i], seq)
    return x2.reshape(batch, seq, d)
```

```python
import functools
import math

import jax
import jax.numpy as jnp
from jax import lax
from jax.experimental import pallas as pl
from jax.experimental.pallas import tpu as pltpu

F32 = jnp.float32
BF16 = jnp.bfloat16

EPS = 1e-6
ROPE_BASE = 10000.0
SGU_GROUPS = 8
SGU_GROUP_DIM = 64
SGU_WIDTH = SGU_GROUPS * SGU_GROUP_DIM
CHUNK = 128
MLA_HEADS = 4
QK_NOPE = 128
QK_ROPE = 64
QK_HEAD = QK_NOPE + QK_ROPE
QK_PAD = 256
V_HEAD = 128
MLA_WIDTH = MLA_HEADS * V_HEAD
Q_LORA = 256
KV_LORA = 128
N_MOD = 6
N_EXPERTS = 8
TOP_K = 2
LANES = 128
D_IN_PAD = 2 * SGU_WIDTH + Q_LORA + KV_LORA + LANES

MIB = 1024 * 1024

TM_MIX = 512
TQ = 4096
ATTN_BAND = 512
TM_FFN = 512
TM_ROUTE = 512
TG = 256
TM_COMB = 256
FF_CHUNK = 1408


def _dot(a, b):
    return jnp.dot(a, b, preferred_element_type=F32)


def _gelu_tanh(x):
    return 0.5 * x * (1.0 + jnp.tanh(math.sqrt(2.0 / math.pi) * (x + 0.044715 * (x * x * x))))


def _silu(x):
    return x * (1.0 / (1.0 + jnp.exp(-x)))


def _rms(x, width):
    return x * lax.rsqrt(jnp.sum(x * x, axis=-1, keepdims=True) * (1.0 / width) + EPS)


def _params(sem, vmem_mib):
    return pltpu.CompilerParams(dimension_semantics=sem, vmem_limit_bytes=vmem_mib * MIB)


def _const_spec(shape):
    nd = len(shape)
    return pl.BlockSpec(shape, lambda *_: (0,) * nd)


def _mod_kernel(c_ref, w_ref, b_ref, o_ref):
    c = c_ref[...]
    ca = _silu(c)
    ca_hi = ca.astype(BF16)
    ca_lo = (ca - ca_hi.astype(F32)).astype(BF16)
    w = w_ref[...]
    w_hi = w.astype(BF16)
    w_lo = (w - w_hi.astype(F32)).astype(BF16)
    acc = _dot(ca_hi, w_hi) + _dot(ca_hi, w_lo) + _dot(ca_lo, w_hi)
    o_ref[...] = acc + b_ref[...]


def _modulation(c, ada_w, ada_b):
    L, D, W = ada_w.shape
    B = c.shape[0]
    rows = 16
    c_pad = jnp.zeros((rows, D), F32).at[:B].set(c)
    tn = 1536
    out = pl.pallas_call(
        _mod_kernel,
        grid=(L, W // tn),
        in_specs=[
            pl.BlockSpec((rows, D), lambda l, j: (0, 0)),
            pl.BlockSpec((None, D, tn), lambda l, j: (l, 0, j)),
            pl.BlockSpec((None, 1, tn), lambda l, j: (l, 0, j)),
        ],
        out_specs=pl.BlockSpec((None, rows, tn), lambda l, j: (l, 0, j)),
        out_shape=jax.ShapeDtypeStruct((L, rows, W), F32),
        compiler_params=_params(("parallel", "parallel"), 40),
        name="adaln_mod",
    )(c_pad, ada_w, ada_b.reshape(L, 1, W))
    return out[:, :B].reshape(L, B, N_MOD, D)


def _rope_kernel(pos_ref, inv_ref, cos_ref, sin_ref):
    pos = pos_ref[0].astype(F32)
    ang = inv_ref[...] * pos
    co = jnp.cos(ang)
    si = jnp.sin(ang)
    z = jnp.zeros((2 * co.shape[0], co.shape[1]), F32)
    cos_ref[...] = jnp.concatenate([co, co, z], axis=0).T
    sin_ref[...] = jnp.concatenate([-si, si, z], axis=0).T


def _rope_tables(positions):
    n = positions.size
    tn = 512
    half = QK_ROPE // 2
    inv_freq = 1.0 / (ROPE_BASE ** (jnp.arange(0, QK_ROPE, 2, dtype=F32) / QK_ROPE))
    pos3 = positions.reshape(n // tn, 1, tn)
    return pl.pallas_call(
        _rope_kernel,
        grid=(n // tn,),
        in_specs=[
            pl.BlockSpec((1, 1, tn), lambda i: (i, 0, 0)),
            pl.BlockSpec((half, 1), lambda i: (0, 0)),
        ],
        out_specs=[pl.BlockSpec((tn, LANES), lambda i: (i, 0))] * 2,
        out_shape=[jax.ShapeDtypeStruct((n, LANES), F32)] * 2,
        compiler_params=_params(("parallel",), 32),
        name="rope_tables",
    )(pos3, inv_freq.reshape(half, 1))


def _rope_rotate(r, cos_t, sin_t, lane):
    partner = jnp.where(lane < QK_ROPE // 2,
                        pltpu.roll(r, LANES - QK_ROPE // 2, 1),
                        pltpu.roll(r, QK_ROPE // 2, 1))
    return r * cos_t + partner * sin_t


def _mixer_pre_kernel(x_ref, mod_ref, nmix_ref, win_ref, sgn_ref, sgw_ref, sgb_ref,
                      qln_ref, kvln_ref, wuq_ref, wuk_ref, wuv_ref, qn_ref, kn_ref,
                      cos_ref, sin_ref, a_ref, q_ref, k_ref, v_ref):
    tm, d = x_ref.shape
    x = x_ref[...]
    shift = mod_ref[0:1, :]
    scale = mod_ref[1:2, :]
    h = _rms(x, d) * nmix_ref[...]
    h = h * (1.0 + scale) + shift
    proj = _dot(h.astype(BF16), win_ref[...])

    o_zv = SGU_WIDTH
    o_cq = 2 * SGU_WIDTH
    o_ckv = o_cq + Q_LORA
    o_kr = o_ckv + KV_LORA

    u = _gelu_tanh(proj[:, :SGU_WIDTH])
    gv = _gelu_tanh(proj[:, o_zv:o_cq])
    mu = jnp.mean(gv, axis=-1, keepdims=True)
    cen = gv - mu
    var = jnp.mean(cen * cen, axis=-1, keepdims=True)
    vn = cen * lax.rsqrt(var + EPS) * sgn_ref[...]

    row = lax.broadcasted_iota(jnp.int32, (CHUNK, CHUNK), 0)
    col = lax.broadcasted_iota(jnp.int32, (CHUNK, CHUNK), 1)
    causal = col <= row
    lane = lax.broadcasted_iota(jnp.int32, (CHUNK, LANES), 1)
    low_half = lane < SGU_GROUP_DIM
    n_pairs = SGU_GROUPS // 2
    wcat = []
    for j in range(n_pairs):
        wa = jnp.where(causal, sgw_ref[2 * j], 0.0)
        wb = jnp.where(causal, sgw_ref[2 * j + 1], 0.0)
        wcat.append(jnp.concatenate([wa, wb], axis=1).astype(BF16))
    for c in range(tm // CHUNK):
        r0 = c * CHUNK
        for j in range(n_pairs):
            l0 = j * LANES
            vb = vn[r0:r0 + CHUNK, l0:l0 + LANES]
            rhs = jnp.concatenate([jnp.where(low_half, vb, 0.0),
                                   jnp.where(low_half, 0.0, vb)], axis=0).astype(BF16)
            s = _dot(wcat[j], rhs) + sgb_ref[:, l0:l0 + LANES]
            a_ref[r0:r0 + CHUNK, l0:l0 + LANES] = (u[r0:r0 + CHUNK, l0:l0 + LANES] * s).astype(BF16)

    lane_t = lax.broadcasted_iota(jnp.int32, (tm, LANES), 1)
    cos_t = cos_ref[...]
    sin_t = sin_ref[...]
    q_scale = QK_HEAD ** -0.5 * math.log2(math.e)

    cqn = _rms(proj[:, o_cq:o_ckv], Q_LORA) * qln_ref[...]
    qf = _dot(cqn.astype(BF16), wuq_ref[...])
    ckvn = _rms(proj[:, o_ckv:o_kr], KV_LORA) * kvln_ref[...]
    ckvb = ckvn.astype(BF16)
    kf = _dot(ckvb, wuk_ref[...])
    v_ref[...] = _dot(ckvb, wuv_ref[...]).astype(BF16)
    kr = proj[:, o_kr:o_kr + LANES]
    kr_ss = jnp.sum(kr * kr, axis=-1, keepdims=True)

    qn_lo = qn_ref[:, :QK_NOPE]
    qn_hi = qn_ref[:, QK_NOPE:]
    kn_lo = kn_ref[:, :QK_NOPE]
    kn_hi = kn_ref[:, QK_NOPE:]
    for hd in range(MLA_HEADS):
        q_lo = qf[:, hd * QK_PAD:hd * QK_PAD + QK_NOPE]
        q_hi = qf[:, hd * QK_PAD + QK_NOPE:(hd + 1) * QK_PAD]
        ss = jnp.sum(q_lo * q_lo, axis=-1, keepdims=True) + jnp.sum(q_hi * q_hi, axis=-1, keepdims=True)
        rs = lax.rsqrt(ss * (1.0 / QK_HEAD) + EPS) * q_scale
        q_ref[:, hd * QK_PAD:hd * QK_PAD + QK_NOPE] = (q_lo * rs * qn_lo).astype(BF16)
        q_ref[:, hd * QK_PAD + QK_NOPE:(hd + 1) * QK_PAD] = _rope_rotate(
            q_hi * rs * qn_hi, cos_t, sin_t, lane_t).astype(BF16)

        k_lo = kf[:, hd * QK_NOPE:(hd + 1) * QK_NOPE]
        ss = jnp.sum(k_lo * k_lo, axis=-1, keepdims=True) + kr_ss
        rs = lax.rsqrt(ss * (1.0 / QK_HEAD) + EPS)
        k_ref[:, hd * QK_PAD:hd * QK_PAD + QK_NOPE] = (k_lo * rs * kn_lo).astype(BF16)
        k_ref[:, hd * QK_PAD + QK_NOPE:(hd + 1) * QK_PAD] = _rope_rotate(
            kr * rs * kn_hi, cos_t, sin_t, lane_t).astype(BF16)


def _mixer_pre(x2, mod_l, norm_mix, w_in, sgu_norm, sgu_w, sgu_b, q_lat_norm, kv_lat_norm,
               w_uq, w_ukv, q_norm, k_norm, cos_t, sin_t, seq):
    n, d = x2.shape
    tm = TM_MIX
    per_b = seq // tm
    win_p = jnp.pad(w_in, ((0, 0), (0, D_IN_PAD - w_in.shape[1]))).astype(BF16)
    wuq_p = jnp.pad(w_uq.reshape(Q_LORA, MLA_HEADS, QK_HEAD),
                    ((0, 0), (0, 0), (0, QK_PAD - QK_HEAD))).reshape(Q_LORA, MLA_HEADS * QK_PAD).astype(BF16)
    wukv = w_ukv.reshape(KV_LORA, MLA_HEADS, QK_NOPE + V_HEAD)
    wuk = wukv[:, :, :QK_NOPE].reshape(KV_LORA, MLA_HEADS * QK_NOPE).astype(BF16)
    wuv = wukv[:, :, QK_NOPE:].reshape(KV_LORA, MLA_WIDTH).astype(BF16)
    qn_p = jnp.pad(q_norm, (0, QK_PAD - QK_HEAD)).reshape(1, QK_PAD)
    kn_p = jnp.pad(k_norm, (0, QK_PAD - QK_HEAD)).reshape(1, QK_PAD)
    sgb_full = jnp.repeat(sgu_b.T, SGU_GROUP_DIM, axis=1)

    tok = lambda i: (i, 0)
    in_specs = [
        pl.BlockSpec((tm, d), tok),
        pl.BlockSpec((None, N_MOD, d), lambda i: (i // per_b, 0, 0)),
        _const_spec((1, d)),
        _const_spec(win_p.shape),
        _const_spec((1, SGU_WIDTH)),
        _const_spec(sgu_w.shape),
        _const_spec(sgb_full.shape),
        _const_spec((1, Q_LORA)),
        _const_spec((1, KV_LORA)),
        _const_spec(wuq_p.shape),
        _const_spec(wuk.shape),
        _const_spec(wuv.shape),
        _const_spec((1, QK_PAD)),
        _const_spec((1, QK_PAD)),
        pl.BlockSpec((tm, LANES), tok),
        pl.BlockSpec((tm, LANES), tok),
    ]
    out_shape = [
        jax.ShapeDtypeStruct((n, SGU_WIDTH), BF16),
        jax.ShapeDtypeStruct((n, MLA_HEADS * QK_PAD), BF16),
        jax.ShapeDtypeStruct((n, MLA_HEADS * QK_PAD), BF16),
        jax.ShapeDtypeStruct((n, MLA_WIDTH), BF16),
    ]
    out_specs = [pl.BlockSpec((tm, s.shape[1]), tok) for s in out_shape]
    return pl.pallas_call(
        _mixer_pre_kernel,
        grid=(n // tm,),
        in_specs=in_specs,
        out_specs=out_specs,
        out_shape=out_shape,
        compiler_params=_params(("parallel",), 48),
        name="mixer_pre",
    )(x2, mod_l, norm_mix.reshape(1, d), win_p, sgu_norm.reshape(1, SGU_WIDTH), sgu_w, sgb_full,
      q_lat_norm.reshape(1, Q_LORA), kv_lat_norm.reshape(1, KV_LORA), wuq_p, wuk, wuv, qn_p, kn_p,
      cos_t, sin_t)


def _attn_kernel(q_ref, k_ref, v_ref, o_ref):
    tq = q_ref.shape[0]
    qi = pl.program_id(2)
    neg = jnp.finfo(F32).min

    def update(q, kb, vb, m, l, acc, mask_from):
        s = lax.dot_general(q, kb, (((1,), (1,)), ((), ())), preferred_element_type=F32)
        if mask_from is not None:
            row = lax.broadcasted_iota(jnp.int32, s.shape, 0)
            col = lax.broadcasted_iota(jnp.int32, s.shape, 1)
            s = jnp.where(col <= row + mask_from, s, neg)
        m_new = jnp.maximum(m, jnp.max(s, axis=-1, keepdims=True))
        alpha = jnp.exp2(m - m_new)
        p = jnp.exp2(s - m_new)
        l_new = alpha * l + jnp.sum(p, axis=-1, keepdims=True)
        acc_new = alpha * acc + _dot(p.astype(BF16), vb)
        return m_new, l_new, acc_new

    def full_step(ki, carry):
        k0 = pl.multiple_of(ki * tq, tq)
        return update(q_ref[...], k_ref[pl.ds(k0, tq), :], v_ref[pl.ds(k0, tq), :], *carry, None)

    init = (jnp.full((tq, 1), neg, F32), jnp.zeros((tq, 1), F32), jnp.zeros((tq, V_HEAD), F32))
    m, l, acc = lax.fori_loop(0, qi, full_step, init)
    k0 = pl.multiple_of(qi * tq, tq)
    for r in range(tq // ATTN_BAND):
        rows = slice(r * ATTN_BAND, (r + 1) * ATTN_BAND)
        nk = (r + 1) * ATTN_BAND
        _, lr, ar = update(q_ref[rows, :], k_ref[pl.ds(k0, nk), :], v_ref[pl.ds(k0, nk), :],
                           m[rows], l[rows], acc[rows], r * ATTN_BAND)
        o_ref[rows, :] = (ar / lr).astype(BF16)


def _attention(q, k, v, batch, seq):
    tq = min(TQ, seq)
    q3 = q.reshape(batch, seq, MLA_HEADS * QK_PAD)
    k3 = k.reshape(batch, seq, MLA_HEADS * QK_PAD)
    v3 = v.reshape(batch, seq, MLA_WIDTH)
    out = pl.pallas_call(
        _attn_kernel,
        grid=(batch, MLA_HEADS, seq // tq),
        in_specs=[
            pl.BlockSpec((None, tq, QK_PAD), lambda b, h, i: (b, i, h)),
            pl.BlockSpec((None, seq, QK_PAD), lambda b, h, i: (b, 0, h)),
            pl.BlockSpec((None, seq, V_HEAD), lambda b, h, i: (b, 0, h)),
        ],
        out_specs=pl.BlockSpec((None, tq, V_HEAD), lambda b, h, i: (b, i, h)),
        out_shape=jax.ShapeDtypeStruct((batch, seq, MLA_WIDTH), BF16),
        compiler_params=_params(("parallel", "parallel", "arbitrary"), 56),
        name="causal_attention",
    )(q3, k3, v3)
    return out.reshape(batch * seq, MLA_WIDTH)


def _mixer_post(x, a, o, mod_ref, wout_ref, nffn_ref):
    half = a.shape[1]
    y = _dot(a, wout_ref[:half, :]) + _dot(o, wout_ref[half:, :])
    x1 = x + mod_ref[2:3, :] * y
    h = _rms(x1, x1.shape[1]) * nffn_ref[...]
    h = h * (1.0 + mod_ref[4:5, :]) + mod_ref[3:4, :]
    return x1, h


def _dense_ffn_kernel(x_ref, a_ref, o_ref, mod_ref, wout_ref, nffn_ref, w1_ref, w3_ref, w2_ref,
                      out_ref):
    x1, h = _mixer_post(x_ref[...], a_ref[...], o_ref[...], mod_ref, wout_ref, nffn_ref)
    hb = h.astype(BF16)
    dff = w1_ref.shape[1]
    acc = jnp.zeros(x1.shape, F32)
    for c0 in range(0, dff, FF_CHUNK):
        h1 = _dot(hb, w1_ref[:, c0:c0 + FF_CHUNK])
        h3 = _dot(hb, w3_ref[:, c0:c0 + FF_CHUNK])
        act = (_silu(h1) * h3).astype(BF16)
        acc = acc + _dot(act, w2_ref[c0:c0 + FF_CHUNK, :])
    out_ref[...] = x1 + mod_ref[5:6, :] * acc


def _dense_layer_tail(x2, a, o, mod_l, w_out, norm_ffn, w1, w3, w2, seq):
    n, d = x2.shape
    tm = TM_FFN
    per_b = seq // tm
    tok = lambda i: (i, 0)
    single = pl.Buffered(1)
    wspec = lambda shape: pl.BlockSpec(shape, lambda i: (0, 0), pipeline_mode=single)
    dff = w1.shape[1]
    return pl.pallas_call(
        _dense_ffn_kernel,
        grid=(n // tm,),
        in_specs=[
            pl.BlockSpec((tm, d), tok),
            pl.BlockSpec((tm, a.shape[1]), tok),
            pl.BlockSpec((tm, o.shape[1]), tok),
            pl.BlockSpec((None, N_MOD, d), lambda i: (i // per_b, 0, 0)),
            wspec((d, d)),
            _const_spec((1, d)),
            wspec((d, dff)),
            wspec((d, dff)),
            wspec((dff, d)),
        ],
        out_specs=pl.BlockSpec((tm, d), tok),
        out_shape=jax.ShapeDtypeStruct((n, d), F32),
        compiler_params=_params(("parallel",), 56),
        name="dense_ffn",
    )(x2, a, o, mod_l, w_out.astype(BF16), norm_ffn.reshape(1, d),
      w1.astype(BF16), w3.astype(BF16), w2.astype(BF16))


def _router_kernel(x_ref, a_ref, o_ref, mod_ref, wout_ref, nffn_ref, rw_ref,
                   x1_ref, h_ref, info_ref, cnt_ref, carry_ref):
    @pl.when(pl.program_id(0) == 0)
    def _():
        carry_ref[...] = jnp.zeros(carry_ref.shape, F32)

    x1, h = _mixer_post(x_ref[...], a_ref[...], o_ref[...], mod_ref, wout_ref, nffn_ref)
    x1_ref[...] = x1
    h_ref[...] = h
    tm = h.shape[0]

    h_hi = h.astype(BF16)
    h_lo = (h - h_hi.astype(F32)).astype(BF16)
    rw = rw_ref[...]
    rw_hi = rw.astype(BF16)
    rw_lo = (rw - rw_hi.astype(F32)).astype(BF16)
    logits = _dot(h_hi, rw_hi) + _dot(h_hi, rw_lo) + _dot(h_lo, rw_hi)
    lt = logits.T[:N_EXPERTS, :]

    eid = lax.broadcasted_iota(jnp.int32, lt.shape, 0)
    m1 = jnp.max(lt, axis=0, keepdims=True)
    i1 = jnp.min(jnp.where(lt == m1, eid, N_EXPERTS), axis=0, keepdims=True)
    rest = jnp.where(eid == i1, -jnp.inf, lt)
    m2 = jnp.max(rest, axis=0, keepdims=True)
    i2 = jnp.min(jnp.where(rest == m2, eid, N_EXPERTS), axis=0, keepdims=True)
    e2 = jnp.exp(m2 - m1)
    g1 = 1.0 / (1.0 + e2)
    g2 = e2 / (1.0 + e2)

    oh1 = (eid == i1).astype(F32)
    oh2 = (eid == i2).astype(F32)
    r_io = lax.broadcasted_iota(jnp.int32, (tm, tm), 0)
    c_io = lax.broadcasted_iota(jnp.int32, (tm, tm), 1)
    before = jnp.where(r_io < c_io, 1.0, 0.0).astype(BF16)
    cnt1 = jnp.sum(oh1, axis=1, keepdims=True)
    cnt2 = jnp.sum(oh2, axis=1, keepdims=True)
    base = carry_ref[:, 0:1]
    rank1_e = _dot(oh1.astype(BF16), before) + base
    rank2_e = _dot(oh2.astype(BF16), before) + base + cnt1
    rank1 = jnp.sum(oh1 * rank1_e, axis=0, keepdims=True)
    rank2 = jnp.sum(oh2 * rank2_e, axis=0, keepdims=True)
    total = base + cnt1 + cnt2
    carry_ref[...] = jnp.broadcast_to(total, carry_ref.shape)
    cnt_ref[...] = jnp.broadcast_to(total, cnt_ref.shape)

    zero = jnp.zeros_like(g1)
    info_ref[0] = jnp.concatenate(
        [i1.astype(F32), i2.astype(F32), g1, g2, rank1, rank2, zero, zero], axis=0)


def _router(x2, a, o, mod_l, w_out, norm_ffn, router_w, seq):
    n, d = x2.shape
    tm = TM_ROUTE
    per_b = seq // tm
    tok = lambda i: (i, 0)
    rw_pad = jnp.pad(router_w, ((0, 0), (0, LANES - N_EXPERTS)))
    return pl.pallas_call(
        _router_kernel,
        grid=(n // tm,),
        in_specs=[
            pl.BlockSpec((tm, d), tok),
            pl.BlockSpec((tm, a.shape[1]), tok),
            pl.BlockSpec((tm, o.shape[1]), tok),
            pl.BlockSpec((None, N_MOD, d), lambda i: (i // per_b, 0, 0)),
            _const_spec((d, d)),
            _const_spec((1, d)),
            _const_spec((d, LANES)),
        ],
        out_specs=[
            pl.BlockSpec((tm, d), tok),
            pl.BlockSpec((tm, d), tok),
            pl.BlockSpec((1, 8, tm), lambda i: (i, 0, 0)),
            pl.BlockSpec((N_EXPERTS, LANES), lambda i: (0, 0)),
        ],
        out_shape=[
            jax.ShapeDtypeStruct((n, d), F32),
            jax.ShapeDtypeStruct((n, d), F32),
            jax.ShapeDtypeStruct((n // tm, 8, tm), F32),
            jax.ShapeDtypeStruct((N_EXPERTS, LANES), F32),
        ],
        scratch_shapes=[pltpu.VMEM((N_EXPERTS, LANES), F32)],
        compiler_params=_params(("arbitrary",), 48),
        name="mixer_post_router",
    )(x2, a, o, mod_l, w_out.astype(BF16), norm_ffn.reshape(1, d), rw_pad)


def _dispatch_kernel(lo_ref, hi_ref, p1_ref, p2_ref, h_ref, xs_hbm, zbuf, sem, zsem):
    tm = p1_ref.shape[2]

    @pl.when(pl.program_id(0) == 0)
    def _():
        zbuf[...] = jnp.zeros(zbuf.shape, F32)

        def fill(r, _):
            pltpu.make_async_copy(zbuf.at[pl.ds(0, 1), :], xs_hbm.at[pl.ds(r, 1), :], zsem).start()
            return 0

        for g in range(lo_ref.shape[0]):
            lax.fori_loop(lo_ref[g], hi_ref[g], fill, 0)
        slack = N_EXPERTS * TG
        pltpu.make_async_copy(xs_hbm.at[pl.ds(0, slack), :], xs_hbm.at[pl.ds(0, slack), :], zsem).wait()

    def issue(t, _):
        src = h_ref.at[pl.ds(t, 1), :]
        pltpu.make_async_copy(src, xs_hbm.at[pl.ds(p1_ref[0, 0, t], 1), :], sem).start()
        pltpu.make_async_copy(src, xs_hbm.at[pl.ds(p2_ref[0, 0, t], 1), :], sem).start()
        return 0

    lax.fori_loop(0, tm, issue, 0, unroll=8)
    for _ in range(TOP_K):
        pltpu.make_async_copy(h_ref, xs_hbm.at[pl.ds(0, tm), :], sem).wait()


def _dispatch(h, pos1, pos2, pad_lo, pad_hi, rows):
    n, d = h.shape
    tm = TM_ROUTE
    idx_spec = pl.BlockSpec((1, 1, tm), lambda i, lo, hi: (i, 0, 0), memory_space=pltpu.SMEM)
    grid_spec = pltpu.PrefetchScalarGridSpec(
        num_scalar_prefetch=2,
        grid=(n // tm,),
        in_specs=[idx_spec, idx_spec, pl.BlockSpec((tm, d), lambda i, lo, hi: (i, 0))],
        out_specs=pl.BlockSpec(memory_space=pl.ANY),
        scratch_shapes=[pltpu.VMEM((8, d), F32), pltpu.SemaphoreType.DMA(()),
                        pltpu.SemaphoreType.DMA(())],
    )
    return pl.pallas_call(
        _dispatch_kernel,
        grid_spec=grid_spec,
        out_shape=jax.ShapeDtypeStruct((rows, d), F32),
        compiler_params=pltpu.CompilerParams(dimension_semantics=("arbitrary",),
                                             has_side_effects=True),
        name="moe_dispatch",
    )(pad_lo, pad_hi, pos1.reshape(n // tm, 1, tm), pos2.reshape(n // tm, 1, tm), h)


def _expert_kernel(te_ref, tv_ref, xs_ref, w1_ref, w3_ref, w2_ref, y_ref):
    i = pl.program_id(0)
    valid = tv_ref[i]

    @pl.when(valid > 0)
    def _():
        xb = xs_ref[...].astype(BF16)
        dff = w1_ref.shape[1]
        acc = jnp.zeros(xs_ref.shape, F32)
        for c0 in range(0, dff, FF_CHUNK):
            h1 = _dot(xb, w1_ref[:, c0:c0 + FF_CHUNK])
            h3 = _dot(xb, w3_ref[:, c0:c0 + FF_CHUNK])
            act = (_silu(h1) * h3).astype(BF16)
            acc = acc + _dot(act, w2_ref[c0:c0 + FF_CHUNK, :])
        y_ref[...] = acc

    @pl.when(valid <= 0)
    def _():
        y_ref[...] = jnp.zeros(y_ref.shape, F32)


def _experts(xs, tile_expert, tile_valid, w1, w3, w2):
    rows, d = xs.shape
    tg = TG
    dff = w1.shape[2]
    grid_spec = pltpu.PrefetchScalarGridSpec(
        num_scalar_prefetch=2,
        grid=(rows // tg,),
        in_specs=[
            pl.BlockSpec((tg, d), lambda i, te, tv: (i, 0)),
            pl.BlockSpec((None, d, dff), lambda i, te, tv: (te[i], 0, 0)),
            pl.BlockSpec((None, d, dff), lambda i, te, tv: (te[i], 0, 0)),
            pl.BlockSpec((None, dff, d), lambda i, te, tv: (te[i], 0, 0)),
        ],
        out_specs=pl.BlockSpec((tg, d), lambda i, te, tv: (i, 0)),
    )
    return pl.pallas_call(
        _expert_kernel,
        grid_spec=grid_spec,
        out_shape=jax.ShapeDtypeStruct((rows, d), F32),
        compiler_params=_params(("arbitrary",), 56),
        name="moe_experts",
    )(tile_expert, tile_valid, xs, w1.astype(BF16), w3.astype(BF16), w2.astype(BF16))


def _combine_kernel(p1_ref, p2_ref, x1_ref, gate_ref, mod_ref, y_hbm, out_ref, buf1, buf2, sem):
    tm = x1_ref.shape[0]

    def issue(t, _):
        pltpu.make_async_copy(y_hbm.at[pl.ds(p1_ref[0, 0, t], 1), :], buf1.at[pl.ds(t, 1), :], sem).start()
        pltpu.make_async_copy(y_hbm.at[pl.ds(p2_ref[0, 0, t], 1), :], buf2.at[pl.ds(t, 1), :], sem).start()
        return 0

    lax.fori_loop(0, tm, issue, 0, unroll=8)
    pltpu.make_async_copy(y_hbm.at[pl.ds(0, tm), :], buf1, sem).wait()
    pltpu.make_async_copy(y_hbm.at[pl.ds(0, tm), :], buf2, sem).wait()
    y = gate_ref[:, 2:3] * buf1[...] + gate_ref[:, 3:4] * buf2[...]
    out_ref[...] = x1_ref[...] + mod_ref[5:6, :] * y


def _combine(x1, y, pos1, pos2, gates, mod_l, seq):
    n, d = x1.shape
    tm = TM_COMB
    per_b = seq // tm
    tok = lambda i: (i, 0)
    idx_spec = pl.BlockSpec((1, 1, tm), lambda i: (i, 0, 0), memory_space=pltpu.SMEM)
    return pl.pallas_call(
        _combine_kernel,
        grid=(n // tm,),
        in_specs=[
            idx_spec, idx_spec,
            pl.BlockSpec((tm, d), tok),
            pl.BlockSpec((tm, gates.shape[1]), tok),
            pl.BlockSpec((None, N_MOD, d), lambda i: (i // per_b, 0, 0)),
            pl.BlockSpec(memory_space=pl.ANY),
        ],
        out_specs=pl.BlockSpec((tm, d), tok),
        out_shape=jax.ShapeDtypeStruct((n, d), F32),
        scratch_shapes=[pltpu.VMEM((tm, d), F32), pltpu.VMEM((tm, d), F32),
                        pltpu.SemaphoreType.DMA(())],
        compiler_params=_params(("arbitrary",), 40),
        name="moe_combine",
    )(pos1.reshape(n // tm, 1, tm), pos2.reshape(n // tm, 1, tm), x1, gates, mod_l, y)


def _moe_layer_tail(x2, a, o, mod_l, w_out, norm_ffn, router_w, w1, w3, w2, seq):
    n, d = x2.shape
    x1, h, info, counts = _router(x2, a, o, mod_l, w_out, norm_ffn, router_w, seq)

    info_t = jnp.transpose(info, (0, 2, 1)).reshape(n, 8)
    cnt = counts[:, 0].astype(jnp.int32)
    padded = ((cnt + TG - 1) // TG) * TG
    ends = jnp.cumsum(padded)
    starts = ends - padded
    e1 = info_t[:, 0].astype(jnp.int32)
    e2 = info_t[:, 1].astype(jnp.int32)
    pos1 = starts[e1] + info_t[:, 4].astype(jnp.int32)
    pos2 = starts[e2] + info_t[:, 5].astype(jnp.int32)
    rows = TOP_K * n + N_EXPERTS * TG
    tile_start = jnp.arange(rows // TG, dtype=jnp.int32) * TG
    tile_expert = jnp.minimum(jnp.sum((ends[None, :] <= tile_start[:, None]).astype(jnp.int32), axis=1),
                              N_EXPERTS - 1)
    tile_valid = jnp.clip(starts[tile_expert] + cnt[tile_expert] - tile_start, 0, TG).astype(jnp.int32)

    pad_lo = jnp.concatenate([starts + cnt, ends[-1:]]).astype(jnp.int32)
    pad_hi = jnp.concatenate([ends, jnp.full((1,), rows, ends.dtype)]).astype(jnp.int32)

    xs = _dispatch(h, pos1, pos2, pad_lo, pad_hi, rows)
    y = _experts(xs, tile_expert, tile_valid, w1, w3, w2)
    return _combine(x1, y, pos1, pos2, info_t, mod_l, seq)


def kernel(x, c, positions, ada_w, ada_b, norm_mix, norm_ffn, w_in, sgu_norm, sgu_w, sgu_b,
           q_lat_norm, kv_lat_norm, w_uq, w_ukv, q_norm, k_norm, w_out,
           ffn_w1, ffn_w3, ffn_w2, router_w, moe_w1, moe_w3, moe_w2):
    batch, seq, d = x.shape
    depth = ada_w.shape[0]
    mod = _modulation(c, ada_w, ada_b)
    cos_t, sin_t = _rope_tables(positions)
    x2 = x.reshape(batch * seq, d)
    for layer in range(depth):
        a, q, k, v = _mixer_pre(x2, mod[layer], norm_mix[layer], w_in[layer], sgu_norm[layer],
                                sgu_w[layer], sgu_b[layer], q_lat_norm[layer], kv_lat_norm[layer],
                                w_uq[layer], w_ukv[layer], q_norm[layer], k_norm[layer],
                                cos_t, sin_t, seq)
        o = _attention(q, k, v, batch, seq)
        i = layer // 2
        if layer % 2 == 0:
            x2 = _dense_layer_tail(x2, a, o, mod[layer], w_out[layer], norm_ffn[layer],
                                   ffn_w1[i], ffn_w3[i], ffn_w2[i], seq)
        else:
            x2 = _moe_layer_tail(x2, a, o, mod[layer], w_out[layer], norm_ffn[layer], router_w[i],
                                 moe_w1[i], moe_w3[i], moe_w2[i], seq)
    return x2.reshape(batch, seq, d)
```

```python
import functools
import math

import jax
import jax.numpy as jnp
from jax import lax
from jax.experimental import pallas as pl
from jax.experimental.pallas import tpu as pltpu

F32 = jnp.float32
BF16 = jnp.bfloat16

EPS = 1e-6
ROPE_BASE = 10000.0
SGU_GROUPS = 8
SGU_GROUP_DIM = 64
SGU_WIDTH = SGU_GROUPS * SGU_GROUP_DIM
CHUNK = 128
MLA_HEADS = 4
QK_NOPE = 128
QK_ROPE = 64
QK_HEAD = QK_NOPE + QK_ROPE
QK_PAD = 256
V_HEAD = 128
MLA_WIDTH = MLA_HEADS * V_HEAD
Q_LORA = 256
KV_LORA = 128
N_MOD = 6
N_EXPERTS = 8
TOP_K = 2
LANES = 128
D_IN_PAD = 2 * SGU_WIDTH + Q_LORA + KV_LORA + LANES

MIB = 1024 * 1024

TM_MIX = 512
TQ = 4096
ATTN_BAND = 512
TM_FFN = 512
TM_ROUTE = 512
TG = 256
TM_COMB = 256
FF_CHUNK = 1408


def _dot(a, b):
    return jnp.dot(a, b, preferred_element_type=F32)


def _gelu_tanh(x):
    return 0.5 * x * (1.0 + jnp.tanh(math.sqrt(2.0 / math.pi) * (x + 0.044715 * (x * x * x))))


def _silu(x):
    return x * (1.0 / (1.0 + jnp.exp(-x)))


def _rms(x, width):
    return x * lax.rsqrt(jnp.sum(x * x, axis=-1, keepdims=True) * (1.0 / width) + EPS)


def _params(sem, vmem_mib):
    return pltpu.CompilerParams(dimension_semantics=sem, vmem_limit_bytes=vmem_mib * MIB)


def _const_spec(shape):
    nd = len(shape)
    return pl.BlockSpec(shape, lambda *_: (0,) * nd)


def _mod_kernel(c_ref, w_ref, b_ref, o_ref):
    c = c_ref[...]
    ca = _silu(c)
    ca_hi = ca.astype(BF16)
    ca_lo = (ca - ca_hi.astype(F32)).astype(BF16)
    w = w_ref[...]
    w_hi = w.astype(BF16)
    w_lo = (w - w_hi.astype(F32)).astype(BF16)
    acc = _dot(ca_hi, w_hi) + _dot(ca_hi, w_lo) + _dot(ca_lo, w_hi)
    o_ref[...] = acc + b_ref[...]


def _modulation(c, ada_w, ada_b):
    L, D, W = ada_w.shape
    B = c.shape[0]
    rows = 16
    c_pad = jnp.zeros((rows, D), F32).at[:B].set(c)
    tn = 1536
    out = pl.pallas_call(
        _mod_kernel,
        grid=(L, W // tn),
        in_specs=[
            pl.BlockSpec((rows, D), lambda l, j: (0, 0)),
            pl.BlockSpec((None, D, tn), lambda l, j: (l, 0, j)),
            pl.BlockSpec((None, 1, tn), lambda l, j: (l, 0, j)),
        ],
        out_specs=pl.BlockSpec((None, rows, tn), lambda l, j: (l, 0, j)),
        out_shape=jax.ShapeDtypeStruct((L, rows, W), F32),
        compiler_params=_params(("parallel", "parallel"), 40),
        name="adaln_mod",
    )(c_pad, ada_w, ada_b.reshape(L, 1, W))
    return out[:, :B].reshape(L, B, N_MOD, D)


def _rope_kernel(pos_ref, inv_ref, cos_ref, sin_ref):
    pos = pos_ref[0].astype(F32)
    ang = inv_ref[...] * pos
    co = jnp.cos(ang)
    si = jnp.sin(ang)
    z = jnp.zeros((2 * co.shape[0], co.shape[1]), F32)
    cos_ref[...] = jnp.concatenate([co, co, z], axis=0).T
    sin_ref[...] = jnp.concatenate([-si, si, z], axis=0).T


def _rope_tables(positions):
    n = positions.size
    tn = 512
    half = QK_ROPE // 2
    inv_freq = 1.0 / (ROPE_BASE ** (jnp.arange(0, QK_ROPE, 2, dtype=F32) / QK_ROPE))
    pos3 = positions.reshape(n // tn, 1, tn)
    return pl.pallas_call(
        _rope_kernel,
        grid=(n // tn,),
        in_specs=[
            pl.BlockSpec((1, 1, tn), lambda i: (i, 0, 0)),
            pl.BlockSpec((half, 1), lambda i: (0, 0)),
        ],
        out_specs=[pl.BlockSpec((tn, LANES), lambda i: (i, 0))] * 2,
        out_shape=[jax.ShapeDtypeStruct((n, LANES), F32)] * 2,
        compiler_params=_params(("parallel",), 32),
        name="rope_tables",
    )(pos3, inv_freq.reshape(half, 1))


def _rope_rotate(r, cos_t, sin_t, lane):
    partner = jnp.where(lane < QK_ROPE // 2,
                        pltpu.roll(r, LANES - QK_ROPE // 2, 1),
                        pltpu.roll(r, QK_ROPE // 2, 1))
    return r * cos_t + partner * sin_t


def _mixer_pre_kernel(x_ref, mod_ref, nmix_ref, win_ref, sgn_ref, sgw_ref, sgb_ref,
                      qln_ref, kvln_ref, wuq_ref, wuk_ref, wuv_ref, qn_ref, kn_ref,
                      cos_ref, sin_ref, a_ref, q_ref, k_ref, v_ref):
    tm, d = x_ref.shape
    x = x_ref[...]
    shift = mod_ref[0:1, :]
    scale = mod_ref[1:2, :]
    h = _rms(x, d) * nmix_ref[...]
    h = h * (1.0 + scale) + shift
    proj = _dot(h.astype(BF16), win_ref[...])

    o_zv = SGU_WIDTH
    o_cq = 2 * SGU_WIDTH
    o_ckv = o_cq + Q_LORA
    o_kr = o_ckv + KV_LORA

    u = _gelu_tanh(proj[:, :SGU_WIDTH])
    gv = _gelu_tanh(proj[:, o_zv:o_cq])
    mu = jnp.mean(gv, axis=-1, keepdims=True)
    cen = gv - mu
    var = jnp.mean(cen * cen, axis=-1, keepdims=True)
    vn = cen * lax.rsqrt(var + EPS) * sgn_ref[...]

    row = lax.broadcasted_iota(jnp.int32, (CHUNK, CHUNK), 0)
    col = lax.broadcasted_iota(jnp.int32, (CHUNK, CHUNK), 1)
    causal = col <= row
    lane = lax.broadcasted_iota(jnp.int32, (CHUNK, LANES), 1)
    low_half = lane < SGU_GROUP_DIM
    n_pairs = SGU_GROUPS // 2
    wcat = []
    for j in range(n_pairs):
        wa = jnp.where(causal, sgw_ref[2 * j], 0.0)
        wb = jnp.where(causal, sgw_ref[2 * j + 1], 0.0)
        wcat.append(jnp.concatenate([wa, wb], axis=1).astype(BF16))
    for c in range(tm // CHUNK):
        r0 = c * CHUNK
        for j in range(n_pairs):
            l0 = j * LANES
            vb = vn[r0:r0 + CHUNK, l0:l0 + LANES]
            rhs = jnp.concatenate([jnp.where(low_half, vb, 0.0),
                                   jnp.where(low_half, 0.0, vb)], axis=0).astype(BF16)
            s = _dot(wcat[j], rhs) + sgb_ref[:, l0:l0 + LANES]
            a_ref[r0:r0 + CHUNK, l0:l0 + LANES] = (u[r0:r0 + CHUNK, l0:l0 + LANES] * s).astype(BF16)

    lane_t = lax.broadcasted_iota(jnp.int32, (tm, LANES), 1)
    cos_t = cos_ref[...]
    sin_t = sin_ref[...]
    q_scale = QK_HEAD ** -0.5 * math.log2(math.e)

    cqn = _rms(proj[:, o_cq:o_ckv], Q_LORA) * qln_ref[...]
    qf = _dot(cqn.astype(BF16), wuq_ref[...])
    ckvn = _rms(proj[:, o_ckv:o_kr], KV_LORA) * kvln_ref[...]
    ckvb = ckvn.astype(BF16)
    kf = _dot(ckvb, wuk_ref[...])
    v_ref[...] = _dot(ckvb, wuv_ref[...]).astype(BF16)
    kr = proj[:, o_kr:o_kr + LANES]
    kr_ss = jnp.sum(kr * kr, axis=-1, keepdims=True)

    qn_lo = qn_ref[:, :QK_NOPE]
    qn_hi = qn_ref[:, QK_NOPE:]
    kn_lo = kn_ref[:, :QK_NOPE]
    kn_hi = kn_ref[:, QK_NOPE:]
    for hd in range(MLA_HEADS):
        q_lo = qf[:, hd * QK_PAD:hd * QK_PAD + QK_NOPE]
        q_hi = qf[:, hd * QK_PAD + QK_NOPE:(hd + 1) * QK_PAD]
        ss = jnp.sum(q_lo * q_lo, axis=-1, keepdims=True) + jnp.sum(q_hi * q_hi, axis=-1, keepdims=True)
        rs = lax.rsqrt(ss * (1.0 / QK_HEAD) + EPS) * q_scale
        q_ref[:, hd * QK_PAD:hd * QK_PAD + QK_NOPE] = (q_lo * rs * qn_lo).astype(BF16)
        q_ref[:, hd * QK_PAD + QK_NOPE:(hd + 1) * QK_PAD] = _rope_rotate(
            q_hi * rs * qn_hi, cos_t, sin_t, lane_t).astype(BF16)

        k_lo = kf[:, hd * QK_NOPE:(hd + 1) * QK_NOPE]
        ss = jnp.sum(k_lo * k_lo, axis=-1, keepdims=True) + kr_ss
        rs = lax.rsqrt(ss * (1.0 / QK_HEAD) + EPS)
        k_ref[:, hd * QK_PAD:hd * QK_PAD + QK_NOPE] = (k_lo * rs * kn_lo).astype(BF16)
        k_ref[:, hd * QK_PAD + QK_NOPE:(hd + 1) * QK_PAD] = _rope_rotate(
            kr * rs * kn_hi, cos_t, sin_t, lane_t).astype(BF16)


def _mixer_pre(x2, mod_l, norm_mix, w_in, sgu_norm, sgu_w, sgu_b, q_lat_norm, kv_lat_norm,
               w_uq, w_ukv, q_norm, k_norm, cos_t, sin_t, seq):
    n, d = x2.shape
    tm = TM_MIX
    per_b = seq // tm
    win_p = jnp.pad(w_in, ((0, 0), (0, D_IN_PAD - w_in.shape[1]))).astype(BF16)
    wuq_p = jnp.pad(w_uq.reshape(Q_LORA, MLA_HEADS, QK_HEAD),
                    ((0, 0), (0, 0), (0, QK_PAD - QK_HEAD))).reshape(Q_LORA, MLA_HEADS * QK_PAD).astype(BF16)
    wukv = w_ukv.reshape(KV_LORA, MLA_HEADS, QK_NOPE + V_HEAD)
    wuk = wukv[:, :, :QK_NOPE].reshape(KV_LORA, MLA_HEADS * QK_NOPE).astype(BF16)
    wuv = wukv[:, :, QK_NOPE:].reshape(KV_LORA, MLA_WIDTH).astype(BF16)
    qn_p = jnp.pad(q_norm, (0, QK_PAD - QK_HEAD)).reshape(1, QK_PAD)
    kn_p = jnp.pad(k_norm, (0, QK_PAD - QK_HEAD)).reshape(1, QK_PAD)
    sgb_full = jnp.repeat(sgu_b.T, SGU_GROUP_DIM, axis=1)

    tok = lambda i: (i, 0)
    in_specs = [
        pl.BlockSpec((tm, d), tok),
        pl.BlockSpec((None, N_MOD, d), lambda i: (i // per_b, 0, 0)),
        _const_spec((1, d)),
        _const_spec(win_p.shape),
        _const_spec((1, SGU_WIDTH)),
        _const_spec(sgu_w.shape),
        _const_spec(sgb_full.shape),
        _const_spec((1, Q_LORA)),
        _const_spec((1, KV_LORA)),
        _const_spec(wuq_p.shape),
        _const_spec(wuk.shape),
        _const_spec(wuv.shape),
        _const_spec((1, QK_PAD)),
        _const_spec((1, QK_PAD)),
        pl.BlockSpec((tm, LANES), tok),
        pl.BlockSpec((tm, LANES), tok),
    ]
    out_shape = [
        jax.ShapeDtypeStruct((n, SGU_WIDTH), BF16),
        jax.ShapeDtypeStruct((n, MLA_HEADS * QK_PAD), BF16),
        jax.ShapeDtypeStruct((n, MLA_HEADS * QK_PAD), BF16),
        jax.ShapeDtypeStruct((n, MLA_WIDTH), BF16),
    ]
    out_specs = [pl.BlockSpec((tm, s.shape[1]), tok) for s in out_shape]
    return pl.pallas_call(
        _mixer_pre_kernel,
        grid=(n // tm,),
        in_specs=in_specs,
        out_specs=out_specs,
        out_shape=out_shape,
        compiler_params=_params(("parallel",), 48),
        name="mixer_pre",
    )(x2, mod_l, norm_mix.reshape(1, d), win_p, sgu_norm.reshape(1, SGU_WIDTH), sgu_w, sgb_full,
      q_lat_norm.reshape(1, Q_LORA), kv_lat_norm.reshape(1, KV_LORA), wuq_p, wuk, wuv, qn_p, kn_p,
      cos_t, sin_t)


def _attn_kernel(q_ref, k_ref, v_ref, o_ref):
    tq = q_ref.shape[0]
    qi = pl.program_id(2)
    neg = jnp.finfo(F32).min

    def update(q, kb, vb, m, l, acc, mask_from):
        s = lax.dot_general(q, kb, (((1,), (1,)), ((), ())), preferred_element_type=F32)
        if mask_from is not None:
            row = lax.broadcasted_iota(jnp.int32, s.shape, 0)
            col = lax.broadcasted_iota(jnp.int32, s.shape, 1)
            s = jnp.where(col <= row + mask_from, s, neg)
        m_new = jnp.maximum(m, jnp.max(s, axis=-1, keepdims=True))
        alpha = jnp.exp2(m - m_new)
        p = jnp.exp2(s - m_new)
        l_new = alpha * l + jnp.sum(p, axis=-1, keepdims=True)
        acc_new = alpha * acc + _dot(p.astype(BF16), vb)
        return m_new, l_new, acc_new

    def full_step(ki, carry):
        k0 = pl.multiple_of(ki * tq, tq)
        return update(q_ref[...], k_ref[pl.ds(k0, tq), :], v_ref[pl.ds(k0, tq), :], *carry, None)

    init = (jnp.full((tq, 1), neg, F32), jnp.zeros((tq, 1), F32), jnp.zeros((tq, V_HEAD), F32))
    m, l, acc = lax.fori_loop(0, qi, full_step, init)
    k0 = pl.multiple_of(qi * tq, tq)
    for r in range(tq // ATTN_BAND):
        rows = slice(r * ATTN_BAND, (r + 1) * ATTN_BAND)
        nk = (r + 1) * ATTN_BAND
        _, lr, ar = update(q_ref[rows, :], k_ref[pl.ds(k0, nk), :], v_ref[pl.ds(k0, nk), :],
                           m[rows], l[rows], acc[rows], r * ATTN_BAND)
        o_ref[rows, :] = (ar / lr).astype(BF16)


def _attention(q, k, v, batch, seq):
    tq = min(TQ, seq)
    q3 = q.reshape(batch, seq, MLA_HEADS * QK_PAD)
    k3 = k.reshape(batch, seq, MLA_HEADS * QK_PAD)
    v3 = v.reshape(batch, seq, MLA_WIDTH)
    out = pl.pallas_call(
        _attn_kernel,
        grid=(batch, MLA_HEADS, seq // tq),
        in_specs=[
            pl.BlockSpec((None, tq, QK_PAD), lambda b, h, i: (b, i, h)),
            pl.BlockSpec((None, seq, QK_PAD), lambda b, h, i: (b, 0, h)),
            pl.BlockSpec((None, seq, V_HEAD), lambda b, h, i: (b, 0, h)),
        ],
        out_specs=pl.BlockSpec((None, tq, V_HEAD), lambda b, h, i: (b, i, h)),
        out_shape=jax.ShapeDtypeStruct((batch, seq, MLA_WIDTH), BF16),
        compiler_params=_params(("parallel", "parallel", "arbitrary"), 56),
        name="causal_attention",
    )(q3, k3, v3)
    return out.reshape(batch * seq, MLA_WIDTH)


def _mixer_post(x, a, o, mod_ref, wout_ref, nffn_ref):
    half = a.shape[1]
    y = _dot(a, wout_ref[:half, :]) + _dot(o, wout_ref[half:, :])
    x1 = x + mod_ref[2:3, :] * y
    h = _rms(x1, x1.shape[1]) * nffn_ref[...]
    h = h * (1.0 + mod_ref[4:5, :]) + mod_ref[3:4, :]
    return x1, h


def _dense_ffn_kernel(x_ref, a_ref, o_ref, mod_ref, wout_ref, nffn_ref, w1_ref, w3_ref, w2_ref,
                      out_ref):
    x1, h = _mixer_post(x_ref[...], a_ref[...], o_ref[...], mod_ref, wout_ref, nffn_ref)
    hb = h.astype(BF16)
    dff = w1_ref.shape[1]
    acc = jnp.zeros(x1.shape, F32)
    for c0 in range(0, dff, FF_CHUNK):
        h1 = _dot(hb, w1_ref[:, c0:c0 + FF_CHUNK])
        h3 = _dot(hb, w3_ref[:, c0:c0 + FF_CHUNK])
        act = (_silu(h1) * h3).astype(BF16)
        acc = acc + _dot(act, w2_ref[c0:c0 + FF_CHUNK, :])
    out_ref[...] = x1 + mod_ref[5:6, :] * acc


def _dense_layer_tail(x2, a, o, mod_l, w_out, norm_ffn, w1, w3, w2, seq):
    n, d = x2.shape
    tm = TM_FFN
    per_b = seq // tm
    tok = lambda i: (i, 0)
    single = pl.Buffered(1)
    wspec = lambda shape: pl.BlockSpec(shape, lambda i: (0, 0), pipeline_mode=single)
    dff = w1.shape[1]
    return pl.pallas_call(
        _dense_ffn_kernel,
        grid=(n // tm,),
        in_specs=[
            pl.BlockSpec((tm, d), tok),
            pl.BlockSpec((tm, a.shape[1]), tok),
            pl.BlockSpec((tm, o.shape[1]), tok),
            pl.BlockSpec((None, N_MOD, d), lambda i: (i // per_b, 0, 0)),
            wspec((d, d)),
            _const_spec((1, d)),
            wspec((d, dff)),
            wspec((d, dff)),
            wspec((dff, d)),
        ],
        out_specs=pl.BlockSpec((tm, d), tok),
        out_shape=jax.ShapeDtypeStruct((n, d), F32),
        compiler_params=_params(("parallel",), 56),
        name="dense_ffn",
    )(x2, a, o, mod_l, w_out.astype(BF16), norm_ffn.reshape(1, d),
      w1.astype(BF16), w3.astype(BF16), w2.astype(BF16))


def _router_kernel(x_ref, a_ref, o_ref, mod_ref, wout_ref, nffn_ref, rw_ref,
                   x1_ref, h_ref, info_ref):
    x1, h = _mixer_post(x_ref[...], a_ref[...], o_ref[...], mod_ref, wout_ref, nffn_ref)
    x1_ref[...] = x1
    h_ref[...] = h

    h_hi = h.astype(BF16)
    h_lo = (h - h_hi.astype(F32)).astype(BF16)
    rw = rw_ref[...]
    rw_hi = rw.astype(BF16)
    rw_lo = (rw - rw_hi.astype(F32)).astype(BF16)
    logits = _dot(h_hi, rw_hi) + _dot(h_hi, rw_lo) + _dot(h_lo, rw_hi)
    lt = logits.T[:N_EXPERTS, :]

    eid = lax.broadcasted_iota(jnp.int32, lt.shape, 0)
    m1 = jnp.max(lt, axis=0, keepdims=True)
    i1 = jnp.min(jnp.where(lt == m1, eid, N_EXPERTS), axis=0, keepdims=True)
    rest = jnp.where(eid == i1, -jnp.inf, lt)
    m2 = jnp.max(rest, axis=0, keepdims=True)
    i2 = jnp.min(jnp.where(rest == m2, eid, N_EXPERTS), axis=0, keepdims=True)
    e2 = jnp.exp(m2 - m1)
    g1 = 1.0 / (1.0 + e2)
    g2 = e2 / (1.0 + e2)

    zero = jnp.zeros_like(g1)
    info_ref[0] = jnp.concatenate(
        [i1.astype(F32), i2.astype(F32), g1, g2, zero, zero, zero, zero], axis=0)


def _router(x2, a, o, mod_l, w_out, norm_ffn, router_w, seq):
    n, d = x2.shape
    tm = TM_ROUTE
    per_b = seq // tm
    tok = lambda i: (i, 0)
    rw_pad = jnp.pad(router_w, ((0, 0), (0, LANES - N_EXPERTS)))
    return pl.pallas_call(
        _router_kernel,
        grid=(n // tm,),
        in_specs=[
            pl.BlockSpec((tm, d), tok),
            pl.BlockSpec((tm, a.shape[1]), tok),
            pl.BlockSpec((tm, o.shape[1]), tok),
            pl.BlockSpec((None, N_MOD, d), lambda i: (i // per_b, 0, 0)),
            _const_spec((d, d)),
            _const_spec((1, d)),
            _const_spec((d, LANES)),
        ],
        out_specs=[
            pl.BlockSpec((tm, d), tok),
            pl.BlockSpec((tm, d), tok),
            pl.BlockSpec((1, 8, tm), lambda i: (i, 0, 0)),
        ],
        out_shape=[
            jax.ShapeDtypeStruct((n, d), F32),
            jax.ShapeDtypeStruct((n, d), F32),
            jax.ShapeDtypeStruct((n // tm, 8, tm), F32),
        ],
        compiler_params=_params(("parallel",), 48),
        name="mixer_post_router",
    )(x2, a, o, mod_l, w_out.astype(BF16), norm_ffn.reshape(1, d), rw_pad)


def _expert_kernel(te_ref, src_cur, src_nxt, dst_prv, dst_cur, h_hbm, w1_ref, w3_ref, w2_ref,
                   y_hbm, xbuf, ybuf, xb_ref, acc_ref, gsem, ssem):
    i = pl.program_id(0)
    last = pl.num_programs(0) - 1
    tg = xbuf.shape[1]
    slot = lax.rem(i, 2)
    other = 1 - slot

    def gather(idx_ref, buf_slot, r):
        return pltpu.make_async_copy(h_hbm.at[pl.ds(idx_ref[0, 0, r], 1), :],
                                     xbuf.at[buf_slot, pl.ds(r, 1), :], gsem)

    def scatter(idx_ref, buf_slot, r):
        return pltpu.make_async_copy(ybuf.at[buf_slot, pl.ds(r, 1), :],
                                     y_hbm.at[pl.ds(idx_ref[0, 0, r], 1), :], ssem)

    def wait_rows(sem):
        pltpu.make_async_copy(h_hbm.at[pl.ds(0, tg), :], xbuf.at[0], sem).wait()

    @pl.when(i == 0)
    def _():
        ybuf[...] = jnp.zeros(ybuf.shape, F32)

        def issue(r, _):
            gather(src_cur, 0, r).start()
            return 0

        lax.fori_loop(0, tg, issue, 0, unroll=8)

    wait_rows(gsem)
    xb_ref[...] = xbuf[slot].astype(BF16)
    acc_ref[...] = jnp.zeros(acc_ref.shape, F32)
    n_chunks = w1_ref.shape[0]
    rows_per_chunk = tg // n_chunks

    def chunk(c, _):
        r0 = c * rows_per_chunk
        for j in range(rows_per_chunk):
            gather(src_nxt, other, r0 + j).start()
            scatter(dst_prv, other, r0 + j).start()
        xb = xb_ref[...]
        act = (_silu(_dot(xb, w1_ref[c])) * _dot(xb, w3_ref[c])).astype(BF16)
        acc_ref[...] += _dot(act, w2_ref[c])
        return 0

    lax.fori_loop(0, n_chunks, chunk, 0)
    ybuf[slot] = acc_ref[...]
    wait_rows(ssem)

    @pl.when(i == last)
    def _():
        wait_rows(gsem)

        def issue(r, _):
            scatter(dst_cur, slot, r).start()
            return 0

        lax.fori_loop(0, tg, issue, 0, unroll=8)
        wait_rows(ssem)


def _experts(h, tile_expert, src, dst, w1, w3, w2):
    n, d = h.shape
    rows = src.shape[0]
    tg = TG
    nt = rows // tg
    n_exp, _, dff = w1.shape
    nch = dff // FF_CHUNK
    w1c = w1.astype(BF16).reshape(n_exp, d, nch, FF_CHUNK).transpose(0, 2, 1, 3)
    w3c = w3.astype(BF16).reshape(n_exp, d, nch, FF_CHUNK).transpose(0, 2, 1, 3)
    w2c = w2.astype(BF16).reshape(n_exp, nch, FF_CHUNK, d)
    src3 = src.reshape(nt, 1, tg)
    dst3 = dst.reshape(nt, 1, tg)
    idx = lambda f: pl.BlockSpec((1, 1, tg), lambda i, te: (f(i), 0, 0), memory_space=pltpu.SMEM)
    grid_spec = pltpu.PrefetchScalarGridSpec(
        num_scalar_prefetch=1,
        grid=(nt,),
        in_specs=[
            idx(lambda i: i),
            idx(lambda i: jnp.minimum(i + 1, nt - 1)),
            idx(lambda i: jnp.maximum(i - 1, 0)),
            idx(lambda i: i),
            pl.BlockSpec(memory_space=pl.ANY),
            pl.BlockSpec((None, nch, d, FF_CHUNK), lambda i, te: (te[i], 0, 0, 0)),
            pl.BlockSpec((None, nch, d, FF_CHUNK), lambda i, te: (te[i], 0, 0, 0)),
            pl.BlockSpec((None, nch, FF_CHUNK, d), lambda i, te: (te[i], 0, 0, 0)),
        ],
        out_specs=pl.BlockSpec(memory_space=pl.ANY),
        scratch_shapes=[pltpu.VMEM((2, tg, d), F32), pltpu.VMEM((2, tg, d), F32),
                        pltpu.VMEM((tg, d), BF16), pltpu.VMEM((tg, d), F32),
                        pltpu.SemaphoreType.DMA(()), pltpu.SemaphoreType.DMA(())],
    )
    return pl.pallas_call(
        _expert_kernel,
        grid_spec=grid_spec,
        out_shape=jax.ShapeDtypeStruct((rows, d), F32),
        compiler_params=pltpu.CompilerParams(dimension_semantics=("arbitrary",),
                                             vmem_limit_bytes=56 * MIB, has_side_effects=True),
        name="moe_experts",
    )(tile_expert, src3, src3, dst3, dst3, h, w1c, w3c, w2c)


def _combine_kernel(x1_ref, gate_ref, mod_ref, y1_ref, y2_ref, out_ref):
    y = gate_ref[:, 2:3] * y1_ref[...] + gate_ref[:, 3:4] * y2_ref[...]
    out_ref[...] = x1_ref[...] + mod_ref[5:6, :] * y


def _combine(x1, y, gates, mod_l, seq):
    n, d = x1.shape
    tm = TM_COMB
    per_b = seq // tm
    nb = n // tm
    tok = lambda i: (i, 0)
    return pl.pallas_call(
        _combine_kernel,
        grid=(nb,),
        in_specs=[
            pl.BlockSpec((tm, d), tok),
            pl.BlockSpec((tm, gates.shape[1]), tok),
            pl.BlockSpec((None, N_MOD, d), lambda i: (i // per_b, 0, 0)),
            pl.BlockSpec((tm, d), tok),
            pl.BlockSpec((tm, d), lambda i: (i + nb, 0)),
        ],
        out_specs=pl.BlockSpec((tm, d), tok),
        out_shape=jax.ShapeDtypeStruct((n, d), F32),
        compiler_params=_params(("parallel",), 40),
        name="moe_combine",
    )(x1, gates, mod_l, y, y)


def _moe_layer_tail(x2, a, o, mod_l, w_out, norm_ffn, router_w, w1, w3, w2, seq):
    n, d = x2.shape
    x1, h, info = _router(x2, a, o, mod_l, w_out, norm_ffn, router_w, seq)

    info_t = jnp.transpose(info, (0, 2, 1)).reshape(n, 8)
    e_all = jnp.concatenate([info_t[:, 0], info_t[:, 1]]).astype(jnp.int32)
    n_assign = TOP_K * n
    rows = n_assign + N_EXPERTS * TG
    stride = 1 << (rows - 1).bit_length()
    experts = jnp.arange(N_EXPERTS, dtype=jnp.int32)
    cnt = jnp.sum((e_all[:, None] == experts[None, :]).astype(jnp.int32), axis=0)
    padded = ((cnt + TG - 1) // TG) * TG
    ends = jnp.cumsum(padded)
    fill_e = jnp.repeat(experts, TG)
    fill_i = jnp.tile(jnp.arange(TG, dtype=jnp.int32), N_EXPERTS)
    fill_used = fill_i < (padded - cnt)[fill_e]
    keys = jnp.concatenate([
        e_all * stride + jnp.arange(n_assign, dtype=jnp.int32),
        jnp.where(fill_used, fill_e, N_EXPERTS) * stride + n_assign + fill_e * TG + fill_i])
    dst = jnp.sort(keys) & (stride - 1)
    src = dst % n
    tile_start = jnp.arange(rows // TG, dtype=jnp.int32) * TG
    tile_expert = jnp.minimum(jnp.sum((ends[None, :] <= tile_start[:, None]).astype(jnp.int32), axis=1),
                              N_EXPERTS - 1)

    y = _experts(h, tile_expert, src, dst, w1, w3, w2)
    return _combine(x1, y, info_t, mod_l, seq)


def kernel(x, c, positions, ada_w, ada_b, norm_mix, norm_ffn, w_in, sgu_norm, sgu_w, sgu_b,
           q_lat_norm, kv_lat_norm, w_uq, w_ukv, q_norm, k_norm, w_out,
           ffn_w1, ffn_w3, ffn_w2, router_w, moe_w1, moe_w3, moe_w2):
    batch, seq, d = x.shape
    depth = ada_w.shape[0]
    mod = _modulation(c, ada_w, ada_b)
    cos_t, sin_t = _rope_tables(positions)
    x2 = x.reshape(batch * seq, d)
    for layer in range(depth):
        a, q, k, v = _mixer_pre(x2, mod[layer], norm_mix[layer], w_in[layer], sgu_norm[layer],
                                sgu_w[layer], sgu_b[layer], q_lat_norm[layer], kv_lat_norm[layer],
                                w_uq[layer], w_ukv[layer], q_norm[layer], k_norm[layer],
                                cos_t, sin_t, seq)
        o = _attention(q, k, v, batch, seq)
        i = layer // 2
        if layer % 2 == 0:
            x2 = _dense_layer_tail(x2, a, o, mod[layer], w_out[layer], norm_ffn[layer],
                                   ffn_w1[i], ffn_w3[i], ffn_w2[i], seq)
        else:
            x2 = _moe_layer_tail(x2, a, o, mod[layer], w_out[layer], norm_ffn[layer], router_w[i],
                                 moe_w1[i], moe_w3[i], moe_w2[i], seq)
    return x2.reshape(batch, seq, d)
```

```python
import functools
import math

import jax
import jax.numpy as jnp
from jax import lax
from jax.experimental import pallas as pl
from jax.experimental.pallas import tpu as pltpu

F32 = jnp.float32
BF16 = jnp.bfloat16

EPS = 1e-6
ROPE_BASE = 10000.0
SGU_GROUPS = 8
SGU_GROUP_DIM = 64
SGU_WIDTH = SGU_GROUPS * SGU_GROUP_DIM
CHUNK = 128
MLA_HEADS = 4
QK_NOPE = 128
QK_ROPE = 64
QK_HEAD = QK_NOPE + QK_ROPE
QK_PAD = 256
V_HEAD = 128
MLA_WIDTH = MLA_HEADS * V_HEAD
Q_LORA = 256
KV_LORA = 128
N_MOD = 6
N_EXPERTS = 8
TOP_K = 2
LANES = 128
D_IN_PAD = 2 * SGU_WIDTH + Q_LORA + KV_LORA + LANES

MIB = 1024 * 1024

TM_MIX = 512
TQ = 4096
ATTN_BAND = 512
TM_FFN = 512
TM_ROUTE = 512
TG = 256
TM_COMB = 256
FF_CHUNK = 1408


def _dot(a, b):
    return jnp.dot(a, b, preferred_element_type=F32)


def _gelu_tanh(x):
    k = -2.0 * math.sqrt(2.0 / math.pi) * math.log2(math.e)
    return x * (1.0 / (1.0 + jnp.exp2(x * (k + (k * 0.044715) * (x * x)))))


def _silu(x):
    return x * (1.0 / (1.0 + jnp.exp(-x)))


def _row_sum(v):
    part = v[:, :LANES]
    for j in range(LANES, v.shape[1], LANES):
        part = part + v[:, j:j + LANES]
    return _dot(part.astype(BF16), jnp.ones((LANES, LANES), BF16))


def _lanes(v, width):
    return jnp.concatenate([v] * (width // LANES), axis=1)


def _scale_rows(x, s):
    return jnp.concatenate([x[:, j:j + LANES] * s for j in range(0, x.shape[1], LANES)], axis=1)


def _rms(x, width):
    rs = lax.rsqrt(_row_sum(x * x) * (1.0 / width) + EPS)
    return _scale_rows(x, rs)


def _params(sem, vmem_mib):
    return pltpu.CompilerParams(dimension_semantics=sem, vmem_limit_bytes=vmem_mib * MIB)


def _const_spec(shape):
    nd = len(shape)
    return pl.BlockSpec(shape, lambda *_: (0,) * nd)


def _mod_kernel(c_ref, w_ref, b_ref, o_ref):
    c = c_ref[...]
    ca = _silu(c)
    ca_hi = ca.astype(BF16)
    ca_lo = (ca - ca_hi.astype(F32)).astype(BF16)
    w = w_ref[...]
    w_hi = w.astype(BF16)
    w_lo = (w - w_hi.astype(F32)).astype(BF16)
    acc = _dot(ca_hi, w_hi) + _dot(ca_hi, w_lo) + _dot(ca_lo, w_hi)
    o_ref[...] = acc + b_ref[...]


def _modulation(c, ada_w, ada_b):
    L, D, W = ada_w.shape
    B = c.shape[0]
    rows = 16
    c_pad = jnp.zeros((rows, D), F32).at[:B].set(c)
    tn = 1536
    out = pl.pallas_call(
        _mod_kernel,
        grid=(L, W // tn),
        in_specs=[
            pl.BlockSpec((rows, D), lambda l, j: (0, 0)),
            pl.BlockSpec((None, D, tn), lambda l, j: (l, 0, j)),
            pl.BlockSpec((None, 1, tn), lambda l, j: (l, 0, j)),
        ],
        out_specs=pl.BlockSpec((None, rows, tn), lambda l, j: (l, 0, j)),
        out_shape=jax.ShapeDtypeStruct((L, rows, W), F32),
        compiler_params=_params(("parallel", "parallel"), 40),
        name="adaln_mod",
    )(c_pad, ada_w, ada_b.reshape(L, 1, W))
    return out[:, :B].reshape(L, B, N_MOD, D)


def _rope_kernel(pos_ref, inv_ref, cos_ref, sin_ref):
    pos = pos_ref[0].astype(F32)
    ang = inv_ref[...] * pos
    co = jnp.cos(ang)
    si = jnp.sin(ang)
    z = jnp.zeros((2 * co.shape[0], co.shape[1]), F32)
    cos_ref[...] = jnp.concatenate([co, co, z], axis=0).T
    sin_ref[...] = jnp.concatenate([-si, si, z], axis=0).T


def _rope_tables(positions):
    n = positions.size
    tn = 512
    half = QK_ROPE // 2
    inv_freq = 1.0 / (ROPE_BASE ** (jnp.arange(0, QK_ROPE, 2, dtype=F32) / QK_ROPE))
    pos3 = positions.reshape(n // tn, 1, tn)
    return pl.pallas_call(
        _rope_kernel,
        grid=(n // tn,),
        in_specs=[
            pl.BlockSpec((1, 1, tn), lambda i: (i, 0, 0)),
            pl.BlockSpec((half, 1), lambda i: (0, 0)),
        ],
        out_specs=[pl.BlockSpec((tn, LANES), lambda i: (i, 0))] * 2,
        out_shape=[jax.ShapeDtypeStruct((n, LANES), F32)] * 2,
        compiler_params=_params(("parallel",), 32),
        name="rope_tables",
    )(pos3, inv_freq.reshape(half, 1))


def _rope_rotate(r, cos_t, sin_t, lane):
    partner = jnp.where(lane < QK_ROPE // 2,
                        pltpu.roll(r, LANES - QK_ROPE // 2, 1),
                        pltpu.roll(r, QK_ROPE // 2, 1))
    return r * cos_t + partner * sin_t


def _mixer_pre_kernel(x_ref, mod_ref, nmix_ref, win_ref, sgn_ref, sgw_ref, sgb_ref,
                      qln_ref, kvln_ref, wuq_ref, wuk_ref, wuv_ref, qn_ref, kn_ref,
                      cos_ref, sin_ref, a_ref, q_ref, k_ref, v_ref):
    tm, d = x_ref.shape
    gain = nmix_ref[...] * (1.0 + mod_ref[1:2, :])
    h = _rms(x_ref[...], d) * gain + mod_ref[0:1, :]
    proj = _dot(h.astype(BF16), win_ref[...])

    o_zv = SGU_WIDTH
    o_cq = 2 * SGU_WIDTH
    o_ckv = o_cq + Q_LORA
    o_kr = o_ckv + KV_LORA

    u = _gelu_tanh(proj[:, :SGU_WIDTH])
    gv = _gelu_tanh(proj[:, o_zv:o_cq])
    cen = gv - _lanes(_row_sum(gv) * (1.0 / SGU_WIDTH), SGU_WIDTH)
    var = _row_sum(cen * cen) * (1.0 / SGU_WIDTH)
    vn = cen * _lanes(lax.rsqrt(var + EPS), SGU_WIDTH) * sgn_ref[...]

    row = lax.broadcasted_iota(jnp.int32, (CHUNK, CHUNK), 0)
    col = lax.broadcasted_iota(jnp.int32, (CHUNK, CHUNK), 1)
    causal = col <= row
    lane = lax.broadcasted_iota(jnp.int32, (CHUNK, LANES), 1)
    low_half = lane < SGU_GROUP_DIM
    n_pairs = SGU_GROUPS // 2
    wcat = []
    for j in range(n_pairs):
        wa = jnp.where(causal, sgw_ref[2 * j], 0.0)
        wb = jnp.where(causal, sgw_ref[2 * j + 1], 0.0)
        wcat.append(jnp.concatenate([wa, wb], axis=1).astype(BF16))
    for c in range(tm // CHUNK):
        r0 = c * CHUNK
        for j in range(n_pairs):
            l0 = j * LANES
            vb = vn[r0:r0 + CHUNK, l0:l0 + LANES]
            rhs = jnp.concatenate([jnp.where(low_half, vb, 0.0),
                                   jnp.where(low_half, 0.0, vb)], axis=0).astype(BF16)
            s = _dot(wcat[j], rhs) + sgb_ref[:, l0:l0 + LANES]
            a_ref[r0:r0 + CHUNK, l0:l0 + LANES] = (u[r0:r0 + CHUNK, l0:l0 + LANES] * s).astype(BF16)

    lane_t = lax.broadcasted_iota(jnp.int32, (tm, LANES), 1)
    cos_t = cos_ref[...]
    sin_t = sin_ref[...]
    q_scale = QK_HEAD ** -0.5 * math.log2(math.e)

    cqn = _rms(proj[:, o_cq:o_ckv], Q_LORA) * qln_ref[...]
    qf = _dot(cqn.astype(BF16), wuq_ref[...])
    ckvn = _rms(proj[:, o_ckv:o_kr], KV_LORA) * kvln_ref[...]
    ckvb = ckvn.astype(BF16)
    kf = _dot(ckvb, wuk_ref[...])
    v_ref[...] = _dot(ckvb, wuv_ref[...]).astype(BF16)
    kr = proj[:, o_kr:o_kr + LANES]

    qn_lo = qn_ref[:, :QK_NOPE]
    qn_hi = qn_ref[:, QK_NOPE:]
    kn_lo = kn_ref[:, :QK_NOPE]
    kr_sq = kr * kr
    kr_rot = _rope_rotate(kr * kn_ref[:, QK_NOPE:], cos_t, sin_t, lane_t)
    for hd in range(MLA_HEADS):
        q_lo = qf[:, hd * QK_PAD:hd * QK_PAD + QK_NOPE]
        q_hi = qf[:, hd * QK_PAD + QK_NOPE:(hd + 1) * QK_PAD]
        rq = lax.rsqrt(_row_sum(q_lo * q_lo + q_hi * q_hi) * (1.0 / QK_HEAD) + EPS) * q_scale
        q_ref[:, hd * QK_PAD:hd * QK_PAD + QK_NOPE] = (q_lo * rq * qn_lo).astype(BF16)
        q_ref[:, hd * QK_PAD + QK_NOPE:(hd + 1) * QK_PAD] = _rope_rotate(
            q_hi * rq * qn_hi, cos_t, sin_t, lane_t).astype(BF16)
        k_lo = kf[:, hd * QK_NOPE:(hd + 1) * QK_NOPE]
        rk = lax.rsqrt(_row_sum(k_lo * k_lo + kr_sq) * (1.0 / QK_HEAD) + EPS)
        k_ref[:, hd * QK_PAD:hd * QK_PAD + QK_NOPE] = (k_lo * rk * kn_lo).astype(BF16)
        k_ref[:, hd * QK_PAD + QK_NOPE:(hd + 1) * QK_PAD] = (kr_rot * rk).astype(BF16)


def _mixer_pre(x2, mod_l, norm_mix, w_in, sgu_norm, sgu_w, sgu_b, q_lat_norm, kv_lat_norm,
               w_uq, w_ukv, q_norm, k_norm, cos_t, sin_t, seq):
    n, d = x2.shape
    tm = TM_MIX
    per_b = seq // tm
    win_p = jnp.pad(w_in, ((0, 0), (0, D_IN_PAD - w_in.shape[1]))).astype(BF16)
    wuq_p = jnp.pad(w_uq.reshape(Q_LORA, MLA_HEADS, QK_HEAD),
                    ((0, 0), (0, 0), (0, QK_PAD - QK_HEAD))).reshape(Q_LORA, MLA_HEADS * QK_PAD).astype(BF16)
    wukv = w_ukv.reshape(KV_LORA, MLA_HEADS, QK_NOPE + V_HEAD)
    wuk = wukv[:, :, :QK_NOPE].reshape(KV_LORA, MLA_HEADS * QK_NOPE).astype(BF16)
    wuv = wukv[:, :, QK_NOPE:].reshape(KV_LORA, MLA_WIDTH).astype(BF16)
    qn_p = jnp.pad(q_norm, (0, QK_PAD - QK_HEAD)).reshape(1, QK_PAD)
    kn_p = jnp.pad(k_norm, (0, QK_PAD - QK_HEAD)).reshape(1, QK_PAD)
    sgb_full = jnp.repeat(sgu_b.T, SGU_GROUP_DIM, axis=1)

    tok = lambda i: (i, 0)
    in_specs = [
        pl.BlockSpec((tm, d), tok),
        pl.BlockSpec((None, N_MOD, d), lambda i: (i // per_b, 0, 0)),
        _const_spec((1, d)),
        _const_spec(win_p.shape),
        _const_spec((1, SGU_WIDTH)),
        _const_spec(sgu_w.shape),
        _const_spec(sgb_full.shape),
        _const_spec((1, Q_LORA)),
        _const_spec((1, KV_LORA)),
        _const_spec(wuq_p.shape),
        _const_spec(wuk.shape),
        _const_spec(wuv.shape),
        _const_spec((1, QK_PAD)),
        _const_spec((1, QK_PAD)),
        pl.BlockSpec((tm, LANES), tok),
        pl.BlockSpec((tm, LANES), tok),
    ]
    out_shape = [
        jax.ShapeDtypeStruct((n, SGU_WIDTH), BF16),
        jax.ShapeDtypeStruct((n, MLA_HEADS * QK_PAD), BF16),
        jax.ShapeDtypeStruct((n, MLA_HEADS * QK_PAD), BF16),
        jax.ShapeDtypeStruct((n, MLA_WIDTH), BF16),
    ]
    out_specs = [pl.BlockSpec((tm, s.shape[1]), tok) for s in out_shape]
    return pl.pallas_call(
        _mixer_pre_kernel,
        grid=(n // tm,),
        in_specs=in_specs,
        out_specs=out_specs,
        out_shape=out_shape,
        compiler_params=_params(("parallel",), 48),
        name="mixer_pre",
    )(x2, mod_l, norm_mix.reshape(1, d), win_p, sgu_norm.reshape(1, SGU_WIDTH), sgu_w, sgb_full,
      q_lat_norm.reshape(1, Q_LORA), kv_lat_norm.reshape(1, KV_LORA), wuq_p, wuk, wuv, qn_p, kn_p,
      cos_t, sin_t)


def _attn_kernel(q_ref, k_ref, v_ref, o_ref):
    tq = q_ref.shape[0]
    qi = pl.program_id(2)
    neg = jnp.finfo(F32).min

    def update(q, kb, vb, m, l, acc, mask_from):
        s = lax.dot_general(q, kb, (((1,), (1,)), ((), ())), preferred_element_type=F32)
        if mask_from is not None:
            row = lax.broadcasted_iota(jnp.int32, s.shape, 0)
            col = lax.broadcasted_iota(jnp.int32, s.shape, 1)
            s = jnp.where(col <= row + mask_from, s, neg)
        m_new = jnp.maximum(m, jnp.max(s, axis=-1, keepdims=True))
        alpha = jnp.exp2(m - m_new)
        p = jnp.exp2(s - m_new)
        l_new = alpha * l + jnp.sum(p, axis=-1, keepdims=True)
        acc_new = alpha * acc + _dot(p.astype(BF16), vb)
        return m_new, l_new, acc_new

    def full_step(ki, carry):
        k0 = pl.multiple_of(ki * tq, tq)
        return update(q_ref[...], k_ref[pl.ds(k0, tq), :], v_ref[pl.ds(k0, tq), :], *carry, None)

    init = (jnp.full((tq, 1), neg, F32), jnp.zeros((tq, 1), F32), jnp.zeros((tq, V_HEAD), F32))
    m, l, acc = lax.fori_loop(0, qi, full_step, init)
    k0 = pl.multiple_of(qi * tq, tq)
    for r in range(tq // ATTN_BAND):
        rows = slice(r * ATTN_BAND, (r + 1) * ATTN_BAND)
        nk = (r + 1) * ATTN_BAND
        _, lr, ar = update(q_ref[rows, :], k_ref[pl.ds(k0, nk), :], v_ref[pl.ds(k0, nk), :],
                           m[rows], l[rows], acc[rows], r * ATTN_BAND)
        o_ref[rows, :] = (ar / lr).astype(BF16)


def _attention(q, k, v, batch, seq):
    tq = min(TQ, seq)
    q3 = q.reshape(batch, seq, MLA_HEADS * QK_PAD)
    k3 = k.reshape(batch, seq, MLA_HEADS * QK_PAD)
    v3 = v.reshape(batch, seq, MLA_WIDTH)
    out = pl.pallas_call(
        _attn_kernel,
        grid=(batch, MLA_HEADS, seq // tq),
        in_specs=[
            pl.BlockSpec((None, tq, QK_PAD), lambda b, h, i: (b, i, h)),
            pl.BlockSpec((None, seq, QK_PAD), lambda b, h, i: (b, 0, h)),
            pl.BlockSpec((None, seq, V_HEAD), lambda b, h, i: (b, 0, h)),
        ],
        out_specs=pl.BlockSpec((None, tq, V_HEAD), lambda b, h, i: (b, i, h)),
        out_shape=jax.ShapeDtypeStruct((batch, seq, MLA_WIDTH), BF16),
        compiler_params=_params(("parallel", "parallel", "arbitrary"), 56),
        name="causal_attention",
    )(q3, k3, v3)
    return out.reshape(batch * seq, MLA_WIDTH)


def _mixer_post(x, a, o, mod_ref, wout_ref, nffn_ref):
    half = a.shape[1]
    y = _dot(a, wout_ref[:half, :]) + _dot(o, wout_ref[half:, :])
    x1 = x + mod_ref[2:3, :] * y
    gain = nffn_ref[...] * (1.0 + mod_ref[4:5, :])
    h = _rms(x1, x1.shape[1]) * gain + mod_ref[3:4, :]
    return x1, h


def _dense_ffn_kernel(x_ref, a_ref, o_ref, mod_ref, wout_ref, nffn_ref, w1_ref, w3_ref, w2_ref,
                      out_ref):
    x1, h = _mixer_post(x_ref[...], a_ref[...], o_ref[...], mod_ref, wout_ref, nffn_ref)
    hb = h.astype(BF16)
    dff = w1_ref.shape[1]
    acc = jnp.zeros(x1.shape, F32)
    for c0 in range(0, dff, FF_CHUNK):
        h1 = _dot(hb, w1_ref[:, c0:c0 + FF_CHUNK])
        h3 = _dot(hb, w3_ref[:, c0:c0 + FF_CHUNK])
        act = (_silu(h1) * h3).astype(BF16)
        acc = acc + _dot(act, w2_ref[c0:c0 + FF_CHUNK, :])
    out_ref[...] = x1 + mod_ref[5:6, :] * acc


def _dense_layer_tail(x2, a, o, mod_l, w_out, norm_ffn, w1, w3, w2, seq):
    n, d = x2.shape
    tm = TM_FFN
    per_b = seq // tm
    tok = lambda i: (i, 0)
    single = pl.Buffered(1)
    wspec = lambda shape: pl.BlockSpec(shape, lambda i: (0, 0), pipeline_mode=single)
    dff = w1.shape[1]
    return pl.pallas_call(
        _dense_ffn_kernel,
        grid=(n // tm,),
        in_specs=[
            pl.BlockSpec((tm, d), tok),
            pl.BlockSpec((tm, a.shape[1]), tok),
            pl.BlockSpec((tm, o.shape[1]), tok),
            pl.BlockSpec((None, N_MOD, d), lambda i: (i // per_b, 0, 0)),
            wspec((d, d)),
            _const_spec((1, d)),
            wspec((d, dff)),
            wspec((d, dff)),
            wspec((dff, d)),
        ],
        out_specs=pl.BlockSpec((tm, d), tok),
        out_shape=jax.ShapeDtypeStruct((n, d), F32),
        compiler_params=_params(("parallel",), 56),
        name="dense_ffn",
    )(x2, a, o, mod_l, w_out.astype(BF16), norm_ffn.reshape(1, d),
      w1.astype(BF16), w3.astype(BF16), w2.astype(BF16))


def _router_kernel(x_ref, a_ref, o_ref, mod_ref, wout_ref, nffn_ref, rw_ref,
                   x1_ref, h_ref, info_ref, cnt_ref, carry_ref):
    @pl.when(pl.program_id(0) == 0)
    def _():
        carry_ref[...] = jnp.zeros(carry_ref.shape, F32)

    x1, h = _mixer_post(x_ref[...], a_ref[...], o_ref[...], mod_ref, wout_ref, nffn_ref)
    x1_ref[...] = x1
    h_ref[...] = h
    tm = h.shape[0]

    h_hi = h.astype(BF16)
    h_lo = (h - h_hi.astype(F32)).astype(BF16)
    rw = rw_ref[...]
    rw_hi = rw.astype(BF16)
    rw_lo = (rw - rw_hi.astype(F32)).astype(BF16)
    logits = _dot(h_hi, rw_hi) + _dot(h_hi, rw_lo) + _dot(h_lo, rw_hi)
    lt = logits.T[:N_EXPERTS, :]

    eid = lax.broadcasted_iota(jnp.int32, lt.shape, 0)
    m1 = jnp.max(lt, axis=0, keepdims=True)
    i1 = jnp.min(jnp.where(lt == m1, eid, N_EXPERTS), axis=0, keepdims=True)
    rest = jnp.where(eid == i1, -jnp.inf, lt)
    m2 = jnp.max(rest, axis=0, keepdims=True)
    i2 = jnp.min(jnp.where(rest == m2, eid, N_EXPERTS), axis=0, keepdims=True)
    e2 = jnp.exp(m2 - m1)
    g1 = 1.0 / (1.0 + e2)
    g2 = e2 / (1.0 + e2)

    oh1 = (eid == i1).astype(F32)
    oh2 = (eid == i2).astype(F32)
    r_io = lax.broadcasted_iota(jnp.int32, (tm, tm), 0)
    c_io = lax.broadcasted_iota(jnp.int32, (tm, tm), 1)
    before = jnp.where(r_io < c_io, 1.0, 0.0).astype(BF16)
    cnt1 = jnp.sum(oh1, axis=1, keepdims=True)
    cnt2 = jnp.sum(oh2, axis=1, keepdims=True)
    base = carry_ref[:, 0:1]
    rank1_e = _dot(oh1.astype(BF16), before) + base
    rank2_e = _dot(oh2.astype(BF16), before) + base + cnt1
    rank1 = jnp.sum(oh1 * rank1_e, axis=0, keepdims=True)
    rank2 = jnp.sum(oh2 * rank2_e, axis=0, keepdims=True)
    total = base + cnt1 + cnt2
    carry_ref[...] = jnp.broadcast_to(total, carry_ref.shape)
    cnt_ref[...] = jnp.broadcast_to(total, cnt_ref.shape)

    zero = jnp.zeros_like(g1)
    info_ref[0] = jnp.concatenate(
        [i1.astype(F32), i2.astype(F32), g1, g2, rank1, rank2, zero, zero], axis=0)


def _router(x2, a, o, mod_l, w_out, norm_ffn, router_w, seq):
    n, d = x2.shape
    tm = TM_ROUTE
    per_b = seq // tm
    tok = lambda i: (i, 0)
    rw_pad = jnp.pad(router_w, ((0, 0), (0, LANES - N_EXPERTS)))
    return pl.pallas_call(
        _router_kernel,
        grid=(n // tm,),
        in_specs=[
            pl.BlockSpec((tm, d), tok),
            pl.BlockSpec((tm, a.shape[1]), tok),
            pl.BlockSpec((tm, o.shape[1]), tok),
            pl.BlockSpec((None, N_MOD, d), lambda i: (i // per_b, 0, 0)),
            _const_spec((d, d)),
            _const_spec((1, d)),
            _const_spec((d, LANES)),
        ],
        out_specs=[
            pl.BlockSpec((tm, d), tok),
            pl.BlockSpec((tm, d), tok),
            pl.BlockSpec((1, 8, tm), lambda i: (i, 0, 0)),
            pl.BlockSpec((N_EXPERTS, LANES), lambda i: (0, 0)),
        ],
        out_shape=[
            jax.ShapeDtypeStruct((n, d), F32),
            jax.ShapeDtypeStruct((n, d), F32),
            jax.ShapeDtypeStruct((n // tm, 8, tm), F32),
            jax.ShapeDtypeStruct((N_EXPERTS, LANES), F32),
        ],
        scratch_shapes=[pltpu.VMEM((N_EXPERTS, LANES), F32)],
        compiler_params=_params(("arbitrary",), 48),
        name="mixer_post_router",
    )(x2, a, o, mod_l, w_out.astype(BF16), norm_ffn.reshape(1, d), rw_pad)


def _dispatch_kernel(lo_ref, hi_ref, p1_ref, p2_ref, h_ref, xs_hbm, zbuf, sem, zsem):
    tm = p1_ref.shape[2]

    @pl.when(pl.program_id(0) == 0)
    def _():
        zbuf[...] = jnp.zeros(zbuf.shape, F32)

        def fill(r, _):
            pltpu.make_async_copy(zbuf.at[pl.ds(0, 1), :], xs_hbm.at[pl.ds(r, 1), :], zsem).start()
            return 0

        for g in range(lo_ref.shape[0]):
            lax.fori_loop(lo_ref[g], hi_ref[g], fill, 0)
        slack = N_EXPERTS * TG
        pltpu.make_async_copy(xs_hbm.at[pl.ds(0, slack), :], xs_hbm.at[pl.ds(0, slack), :], zsem).wait()

    def issue(t, _):
        src = h_ref.at[pl.ds(t, 1), :]
        pltpu.make_async_copy(src, xs_hbm.at[pl.ds(p1_ref[0, 0, t], 1), :], sem).start()
        pltpu.make_async_copy(src, xs_hbm.at[pl.ds(p2_ref[0, 0, t], 1), :], sem).start()
        return 0

    lax.fori_loop(0, tm, issue, 0, unroll=8)
    for _ in range(TOP_K):
        pltpu.make_async_copy(h_ref, xs_hbm.at[pl.ds(0, tm), :], sem).wait()


def _dispatch(h, pos1, pos2, pad_lo, pad_hi, rows):
    n, d = h.shape
    tm = TM_ROUTE
    idx_spec = pl.BlockSpec((1, 1, tm), lambda i, lo, hi: (i, 0, 0), memory_space=pltpu.SMEM)
    grid_spec = pltpu.PrefetchScalarGridSpec(
        num_scalar_prefetch=2,
        grid=(n // tm,),
        in_specs=[idx_spec, idx_spec, pl.BlockSpec((tm, d), lambda i, lo, hi: (i, 0))],
        out_specs=pl.BlockSpec(memory_space=pl.ANY),
        scratch_shapes=[pltpu.VMEM((8, d), F32), pltpu.SemaphoreType.DMA(()),
                        pltpu.SemaphoreType.DMA(())],
    )
    return pl.pallas_call(
        _dispatch_kernel,
        grid_spec=grid_spec,
        out_shape=jax.ShapeDtypeStruct((rows, d), F32),
        compiler_params=pltpu.CompilerParams(dimension_semantics=("arbitrary",),
                                             has_side_effects=True),
        name="moe_dispatch",
    )(pad_lo, pad_hi, pos1.reshape(n // tm, 1, tm), pos2.reshape(n // tm, 1, tm), h)


def _expert_kernel(te_ref, tv_ref, xs_ref, w1_ref, w3_ref, w2_ref, y_ref):
    i = pl.program_id(0)
    valid = tv_ref[i]

    @pl.when(valid > 0)
    def _():
        xb = xs_ref[...].astype(BF16)
        dff = w1_ref.shape[1]
        acc = jnp.zeros(xs_ref.shape, F32)
        for c0 in range(0, dff, FF_CHUNK):
            h1 = _dot(xb, w1_ref[:, c0:c0 + FF_CHUNK])
            h3 = _dot(xb, w3_ref[:, c0:c0 + FF_CHUNK])
            act = (_silu(h1) * h3).astype(BF16)
            acc = acc + _dot(act, w2_ref[c0:c0 + FF_CHUNK, :])
        y_ref[...] = acc

    @pl.when(valid <= 0)
    def _():
        y_ref[...] = jnp.zeros(y_ref.shape, F32)


def _experts(xs, tile_expert, tile_valid, w1, w3, w2):
    rows, d = xs.shape
    tg = TG
    dff = w1.shape[2]
    grid_spec = pltpu.PrefetchScalarGridSpec(
        num_scalar_prefetch=2,
        grid=(rows // tg,),
        in_specs=[
            pl.BlockSpec((tg, d), lambda i, te, tv: (i, 0)),
            pl.BlockSpec((None, d, dff), lambda i, te, tv: (te[i], 0, 0)),
            pl.BlockSpec((None, d, dff), lambda i, te, tv: (te[i], 0, 0)),
            pl.BlockSpec((None, dff, d), lambda i, te, tv: (te[i], 0, 0)),
        ],
        out_specs=pl.BlockSpec((tg, d), lambda i, te, tv: (i, 0)),
    )
    return pl.pallas_call(
        _expert_kernel,
        grid_spec=grid_spec,
        out_shape=jax.ShapeDtypeStruct((rows, d), F32),
        compiler_params=_params(("arbitrary",), 56),
        name="moe_experts",
    )(tile_expert, tile_valid, xs, w1.astype(BF16), w3.astype(BF16), w2.astype(BF16))


def _combine_kernel(p1_ref, p2_ref, x1_ref, gate_ref, mod_ref, y_hbm, out_ref, buf1, buf2, sem):
    tm = x1_ref.shape[0]

    def issue(t, _):
        pltpu.make_async_copy(y_hbm.at[pl.ds(p1_ref[0, 0, t], 1), :], buf1.at[pl.ds(t, 1), :], sem).start()
        pltpu.make_async_copy(y_hbm.at[pl.ds(p2_ref[0, 0, t], 1), :], buf2.at[pl.ds(t, 1), :], sem).start()
        return 0

    lax.fori_loop(0, tm, issue, 0, unroll=8)
    pltpu.make_async_copy(y_hbm.at[pl.ds(0, tm), :], buf1, sem).wait()
    pltpu.make_async_copy(y_hbm.at[pl.ds(0, tm), :], buf2, sem).wait()
    y = gate_ref[:, 2:3] * buf1[...] + gate_ref[:, 3:4] * buf2[...]
    out_ref[...] = x1_ref[...] + mod_ref[5:6, :] * y


def _combine(x1, y, pos1, pos2, gates, mod_l, seq):
    n, d = x1.shape
    tm = TM_COMB
    per_b = seq // tm
    tok = lambda i: (i, 0)
    idx_spec = pl.BlockSpec((1, 1, tm), lambda i: (i, 0, 0), memory_space=pltpu.SMEM)
    return pl.pallas_call(
        _combine_kernel,
        grid=(n // tm,),
        in_specs=[
            idx_spec, idx_spec,
            pl.BlockSpec((tm, d), tok),
            pl.BlockSpec((tm, gates.shape[1]), tok),
            pl.BlockSpec((None, N_MOD, d), lambda i: (i // per_b, 0, 0)),
            pl.BlockSpec(memory_space=pl.ANY),
        ],
        out_specs=pl.BlockSpec((tm, d), tok),
        out_shape=jax.ShapeDtypeStruct((n, d), F32),
        scratch_shapes=[pltpu.VMEM((tm, d), F32), pltpu.VMEM((tm, d), F32),
                        pltpu.SemaphoreType.DMA(())],
        compiler_params=_params(("arbitrary",), 40),
        name="moe_combine",
    )(pos1.reshape(n // tm, 1, tm), pos2.reshape(n // tm, 1, tm), x1, gates, mod_l, y)


def _moe_layer_tail(x2, a, o, mod_l, w_out, norm_ffn, router_w, w1, w3, w2, seq):
    n, d = x2.shape
    x1, h, info, counts = _router(x2, a, o, mod_l, w_out, norm_ffn, router_w, seq)

    info_t = jnp.transpose(info, (0, 2, 1)).reshape(n, 8)
    cnt = counts[:, 0].astype(jnp.int32)
    padded = ((cnt + TG - 1) // TG) * TG
    ends = jnp.cumsum(padded)
    starts = ends - padded
    e1 = info_t[:, 0].astype(jnp.int32)
    e2 = info_t[:, 1].astype(jnp.int32)
    pos1 = starts[e1] + info_t[:, 4].astype(jnp.int32)
    pos2 = starts[e2] + info_t[:, 5].astype(jnp.int32)
    rows = TOP_K * n + N_EXPERTS * TG
    tile_start = jnp.arange(rows // TG, dtype=jnp.int32) * TG
    tile_expert = jnp.minimum(jnp.sum((ends[None, :] <= tile_start[:, None]).astype(jnp.int32), axis=1),
                              N_EXPERTS - 1)
    tile_valid = jnp.clip(starts[tile_expert] + cnt[tile_expert] - tile_start, 0, TG).astype(jnp.int32)

    pad_lo = jnp.concatenate([starts + cnt, ends[-1:]]).astype(jnp.int32)
    pad_hi = jnp.concatenate([ends, jnp.full((1,), rows, ends.dtype)]).astype(jnp.int32)

    xs = _dispatch(h, pos1, pos2, pad_lo, pad_hi, rows)
    y = _experts(xs, tile_expert, tile_valid, w1, w3, w2)
    return _combine(x1, y, pos1, pos2, info_t, mod_l, seq)


def kernel(x, c, positions, ada_w, ada_b, norm_mix, norm_ffn, w_in, sgu_norm, sgu_w, sgu_b,
           q_lat_norm, kv_lat_norm, w_uq, w_ukv, q_norm, k_norm, w_out,
           ffn_w1, ffn_w3, ffn_w2, router_w, moe_w1, moe_w3, moe_w2):
    batch, seq, d = x.shape
    depth = ada_w.shape[0]
    mod = _modulation(c, ada_w, ada_b)
    cos_t, sin_t = _rope_tables(positions)
    x2 = x.reshape(batch * seq, d)
    for layer in range(depth):
        a, q, k, v = _mixer_pre(x2, mod[layer], norm_mix[layer], w_in[layer], sgu_norm[layer],
                                sgu_w[layer], sgu_b[layer], q_lat_norm[layer], kv_lat_norm[layer],
                                w_uq[layer], w_ukv[layer], q_norm[layer], k_norm[layer],
                                cos_t, sin_t, seq)
        o = _attention(q, k, v, batch, seq)
        i = layer // 2
        if layer % 2 == 0:
            x2 = _dense_layer_tail(x2, a, o, mod[layer], w_out[layer], norm_ffn[layer],
                                   ffn_w1[i], ffn_w3[i], ffn_w2[i], seq)
        else:
            x2 = _moe_layer_tail(x2, a, o, mod[layer], w_out[layer], norm_ffn[layer], router_w[i],
                                 moe_w1[i], moe_w3[i], moe_w2[i], seq)
    return x2.reshape(batch, seq, d)
```

```python
import functools
import math

import jax
import jax.numpy as jnp
from jax import lax
from jax.experimental import pallas as pl
from jax.experimental.pallas import tpu as pltpu

F32 = jnp.float32
BF16 = jnp.bfloat16

EPS = 1e-6
ROPE_BASE = 10000.0
SGU_GROUPS = 8
SGU_GROUP_DIM = 64
SGU_WIDTH = SGU_GROUPS * SGU_GROUP_DIM
CHUNK = 128
MLA_HEADS = 4
QK_NOPE = 128
QK_ROPE = 64
QK_HEAD = QK_NOPE + QK_ROPE
QK_PAD = 256
V_HEAD = 128
MLA_WIDTH = MLA_HEADS * V_HEAD
Q_LORA = 256
KV_LORA = 128
N_MOD = 6
N_EXPERTS = 8
TOP_K = 2
LANES = 128
D_IN_PAD = 2 * SGU_WIDTH + Q_LORA + KV_LORA + LANES

MIB = 1024 * 1024

TM_MIX = 1024
MIX_SUB = 512
TQ = 4096
ATTN_BAND = 512
TM_FFN = 512
TM_ROUTE = 512
TG = 256
TM_COMB = 256
FF_CHUNK = 1408


def _dot(a, b):
    return jnp.dot(a, b, preferred_element_type=F32)


def _gelu_tanh(x):
    k = -2.0 * math.sqrt(2.0 / math.pi) * math.log2(math.e)
    return x * (1.0 / (1.0 + jnp.exp2(x * (k + (k * 0.044715) * (x * x)))))


def _silu(x):
    return x * (1.0 / (1.0 + jnp.exp(-x)))


def _row_sum(v):
    part = v[:, :LANES]
    for j in range(LANES, v.shape[1], LANES):
        part = part + v[:, j:j + LANES]
    return _dot(part.astype(BF16), jnp.ones((LANES, LANES), BF16))


def _lanes(v, width):
    return jnp.concatenate([v] * (width // LANES), axis=1)


def _scale_rows(x, s):
    return jnp.concatenate([x[:, j:j + LANES] * s for j in range(0, x.shape[1], LANES)], axis=1)


def _rms(x, width):
    rs = lax.rsqrt(_row_sum(x * x) * (1.0 / width) + EPS)
    return _scale_rows(x, rs)


def _params(sem, vmem_mib):
    return pltpu.CompilerParams(dimension_semantics=sem, vmem_limit_bytes=vmem_mib * MIB)


def _const_spec(shape):
    nd = len(shape)
    return pl.BlockSpec(shape, lambda *_: (0,) * nd)


def _mod_kernel(c_ref, w_ref, b_ref, o_ref):
    c = c_ref[...]
    ca = _silu(c)
    ca_hi = ca.astype(BF16)
    ca_lo = (ca - ca_hi.astype(F32)).astype(BF16)
    w = w_ref[...]
    w_hi = w.astype(BF16)
    w_lo = (w - w_hi.astype(F32)).astype(BF16)
    acc = _dot(ca_hi, w_hi) + _dot(ca_hi, w_lo) + _dot(ca_lo, w_hi)
    o_ref[...] = acc + b_ref[...]


def _modulation(c, ada_w, ada_b):
    L, D, W = ada_w.shape
    B = c.shape[0]
    rows = 16
    c_pad = jnp.zeros((rows, D), F32).at[:B].set(c)
    tn = 1536
    out = pl.pallas_call(
        _mod_kernel,
        grid=(L, W // tn),
        in_specs=[
            pl.BlockSpec((rows, D), lambda l, j: (0, 0)),
            pl.BlockSpec((None, D, tn), lambda l, j: (l, 0, j)),
            pl.BlockSpec((None, 1, tn), lambda l, j: (l, 0, j)),
        ],
        out_specs=pl.BlockSpec((None, rows, tn), lambda l, j: (l, 0, j)),
        out_shape=jax.ShapeDtypeStruct((L, rows, W), F32),
        compiler_params=_params(("parallel", "parallel"), 40),
        name="adaln_mod",
    )(c_pad, ada_w, ada_b.reshape(L, 1, W))
    return out[:, :B].reshape(L, B, N_MOD, D)


def _rope_kernel(pos_ref, inv_ref, cos_ref, sin_ref):
    pos = pos_ref[0].astype(F32)
    ang = inv_ref[...] * pos
    co = jnp.cos(ang)
    si = jnp.sin(ang)
    z = jnp.zeros((2 * co.shape[0], co.shape[1]), F32)
    cos_ref[...] = jnp.concatenate([co, co, z], axis=0).T
    sin_ref[...] = jnp.concatenate([-si, si, z], axis=0).T


def _rope_tables(positions):
    n = positions.size
    tn = 512
    half = QK_ROPE // 2
    inv_freq = 1.0 / (ROPE_BASE ** (jnp.arange(0, QK_ROPE, 2, dtype=F32) / QK_ROPE))
    pos3 = positions.reshape(n // tn, 1, tn)
    return pl.pallas_call(
        _rope_kernel,
        grid=(n // tn,),
        in_specs=[
            pl.BlockSpec((1, 1, tn), lambda i: (i, 0, 0)),
            pl.BlockSpec((half, 1), lambda i: (0, 0)),
        ],
        out_specs=[pl.BlockSpec((tn, LANES), lambda i: (i, 0))] * 2,
        out_shape=[jax.ShapeDtypeStruct((n, LANES), F32)] * 2,
        compiler_params=_params(("parallel",), 32),
        name="rope_tables",
    )(pos3, inv_freq.reshape(half, 1))


def _rope_rotate(r, cos_t, sin_t, lane):
    partner = jnp.where(lane < QK_ROPE // 2,
                        pltpu.roll(r, LANES - QK_ROPE // 2, 1),
                        pltpu.roll(r, QK_ROPE // 2, 1))
    return r * cos_t + partner * sin_t


def _mixer_pre_kernel(x_ref, mod_ref, nmix_ref, win_ref, sgn_ref, sgw_ref, sgb_ref,
                      qln_ref, kvln_ref, wuq_ref, wuk_ref, wuv_ref, qn_ref, kn_ref,
                      cos_ref, sin_ref, a_ref, q_ref, k_ref, v_ref):
    for r0 in range(0, x_ref.shape[0], MIX_SUB):
        rows = pl.ds(r0, MIX_SUB)
        _mixer_pre_rows(x_ref.at[rows], mod_ref, nmix_ref, win_ref, sgn_ref, sgw_ref, sgb_ref,
                        qln_ref, kvln_ref, wuq_ref, wuk_ref, wuv_ref, qn_ref, kn_ref,
                        cos_ref.at[rows], sin_ref.at[rows],
                        a_ref.at[rows], q_ref.at[rows], k_ref.at[rows], v_ref.at[rows])


def _mixer_pre_rows(x_ref, mod_ref, nmix_ref, win_ref, sgn_ref, sgw_ref, sgb_ref,
                    qln_ref, kvln_ref, wuq_ref, wuk_ref, wuv_ref, qn_ref, kn_ref,
                    cos_ref, sin_ref, a_ref, q_ref, k_ref, v_ref):
    tm, d = x_ref.shape
    gain = nmix_ref[...] * (1.0 + mod_ref[1:2, :])
    h = _rms(x_ref[...], d) * gain + mod_ref[0:1, :]
    proj = _dot(h.astype(BF16), win_ref[...])

    o_zv = SGU_WIDTH
    o_cq = 2 * SGU_WIDTH
    o_ckv = o_cq + Q_LORA
    o_kr = o_ckv + KV_LORA

    u = _gelu_tanh(proj[:, :SGU_WIDTH])
    gv = _gelu_tanh(proj[:, o_zv:o_cq])
    cen = gv - _lanes(_row_sum(gv) * (1.0 / SGU_WIDTH), SGU_WIDTH)
    var = _row_sum(cen * cen) * (1.0 / SGU_WIDTH)
    vn = cen * _lanes(lax.rsqrt(var + EPS), SGU_WIDTH) * sgn_ref[...]

    row = lax.broadcasted_iota(jnp.int32, (CHUNK, CHUNK), 0)
    col = lax.broadcasted_iota(jnp.int32, (CHUNK, CHUNK), 1)
    causal = col <= row
    lane = lax.broadcasted_iota(jnp.int32, (CHUNK, LANES), 1)
    low_half = lane < SGU_GROUP_DIM
    n_pairs = SGU_GROUPS // 2
    wcat = []
    for j in range(n_pairs):
        wa = jnp.where(causal, sgw_ref[2 * j], 0.0)
        wb = jnp.where(causal, sgw_ref[2 * j + 1], 0.0)
        wcat.append(jnp.concatenate([wa, wb], axis=1).astype(BF16))
    for c in range(tm // CHUNK):
        r0 = c * CHUNK
        for j in range(n_pairs):
            l0 = j * LANES
            vb = vn[r0:r0 + CHUNK, l0:l0 + LANES]
            rhs = jnp.concatenate([jnp.where(low_half, vb, 0.0),
                                   jnp.where(low_half, 0.0, vb)], axis=0).astype(BF16)
            s = _dot(wcat[j], rhs) + sgb_ref[:, l0:l0 + LANES]
            a_ref[r0:r0 + CHUNK, l0:l0 + LANES] = (u[r0:r0 + CHUNK, l0:l0 + LANES] * s).astype(BF16)

    lane_t = lax.broadcasted_iota(jnp.int32, (tm, LANES), 1)
    cos_t = cos_ref[...]
    sin_t = sin_ref[...]
    q_scale = QK_HEAD ** -0.5 * math.log2(math.e)

    cqn = _rms(proj[:, o_cq:o_ckv], Q_LORA) * qln_ref[...]
    qf = _dot(cqn.astype(BF16), wuq_ref[...])
    ckvn = _rms(proj[:, o_ckv:o_kr], KV_LORA) * kvln_ref[...]
    ckvb = ckvn.astype(BF16)
    kf = _dot(ckvb, wuk_ref[...])
    v_ref[...] = _dot(ckvb, wuv_ref[...]).astype(BF16)
    kr = proj[:, o_kr:o_kr + LANES]

    qn_lo = qn_ref[:, :QK_NOPE]
    qn_hi = qn_ref[:, QK_NOPE:]
    kn_lo = kn_ref[:, :QK_NOPE]
    kr_sq = kr * kr
    kr_rot = _rope_rotate(kr * kn_ref[:, QK_NOPE:], cos_t, sin_t, lane_t)
    for hd in range(MLA_HEADS):
        q_lo = qf[:, hd * QK_PAD:hd * QK_PAD + QK_NOPE]
        q_hi = qf[:, hd * QK_PAD + QK_NOPE:(hd + 1) * QK_PAD]
        rq = lax.rsqrt(_row_sum(q_lo * q_lo + q_hi * q_hi) * (1.0 / QK_HEAD) + EPS) * q_scale
        q_ref[:, hd * QK_PAD:hd * QK_PAD + QK_NOPE] = (q_lo * rq * qn_lo).astype(BF16)
        q_ref[:, hd * QK_PAD + QK_NOPE:(hd + 1) * QK_PAD] = _rope_rotate(
            q_hi * rq * qn_hi, cos_t, sin_t, lane_t).astype(BF16)
        k_lo = kf[:, hd * QK_NOPE:(hd + 1) * QK_NOPE]
        rk = lax.rsqrt(_row_sum(k_lo * k_lo + kr_sq) * (1.0 / QK_HEAD) + EPS)
        k_ref[:, hd * QK_PAD:hd * QK_PAD + QK_NOPE] = (k_lo * rk * kn_lo).astype(BF16)
        k_ref[:, hd * QK_PAD + QK_NOPE:(hd + 1) * QK_PAD] = (kr_rot * rk).astype(BF16)


def _mixer_pre(x2, mod_l, norm_mix, w_in, sgu_norm, sgu_w, sgu_b, q_lat_norm, kv_lat_norm,
               w_uq, w_ukv, q_norm, k_norm, cos_t, sin_t, seq):
    n, d = x2.shape
    tm = TM_MIX
    per_b = seq // tm
    win_p = jnp.pad(w_in, ((0, 0), (0, D_IN_PAD - w_in.shape[1]))).astype(BF16)
    wuq_p = jnp.pad(w_uq.reshape(Q_LORA, MLA_HEADS, QK_HEAD),
                    ((0, 0), (0, 0), (0, QK_PAD - QK_HEAD))).reshape(Q_LORA, MLA_HEADS * QK_PAD).astype(BF16)
    wukv = w_ukv.reshape(KV_LORA, MLA_HEADS, QK_NOPE + V_HEAD)
    wuk = wukv[:, :, :QK_NOPE].reshape(KV_LORA, MLA_HEADS * QK_NOPE).astype(BF16)
    wuv = wukv[:, :, QK_NOPE:].reshape(KV_LORA, MLA_WIDTH).astype(BF16)
    qn_p = jnp.pad(q_norm, (0, QK_PAD - QK_HEAD)).reshape(1, QK_PAD)
    kn_p = jnp.pad(k_norm, (0, QK_PAD - QK_HEAD)).reshape(1, QK_PAD)
    sgb_full = jnp.repeat(sgu_b.T, SGU_GROUP_DIM, axis=1)

    tok = lambda i: (i, 0)
    in_specs = [
        pl.BlockSpec((tm, d), tok),
        pl.BlockSpec((None, N_MOD, d), lambda i: (i // per_b, 0, 0)),
        _const_spec((1, d)),
        _const_spec(win_p.shape),
        _const_spec((1, SGU_WIDTH)),
        _const_spec(sgu_w.shape),
        _const_spec(sgb_full.shape),
        _const_spec((1, Q_LORA)),
        _const_spec((1, KV_LORA)),
        _const_spec(wuq_p.shape),
        _const_spec(wuk.shape),
        _const_spec(wuv.shape),
        _const_spec((1, QK_PAD)),
        _const_spec((1, QK_PAD)),
        pl.BlockSpec((tm, LANES), tok),
        pl.BlockSpec((tm, LANES), tok),
    ]
    out_shape = [
        jax.ShapeDtypeStruct((n, SGU_WIDTH), BF16),
        jax.ShapeDtypeStruct((n, MLA_HEADS * QK_PAD), BF16),
        jax.ShapeDtypeStruct((n, MLA_HEADS * QK_PAD), BF16),
        jax.ShapeDtypeStruct((n, MLA_WIDTH), BF16),
    ]
    out_specs = [pl.BlockSpec((tm, s.shape[1]), tok) for s in out_shape]
    return pl.pallas_call(
        _mixer_pre_kernel,
        grid=(n // tm,),
        in_specs=in_specs,
        out_specs=out_specs,
        out_shape=out_shape,
        compiler_params=_params(("parallel",), 48),
        name="mixer_pre",
    )(x2, mod_l, norm_mix.reshape(1, d), win_p, sgu_norm.reshape(1, SGU_WIDTH), sgu_w, sgb_full,
      q_lat_norm.reshape(1, Q_LORA), kv_lat_norm.reshape(1, KV_LORA), wuq_p, wuk, wuv, qn_p, kn_p,
      cos_t, sin_t)


def _attn_kernel(q_ref, k_ref, v_ref, o_ref):
    tq = q_ref.shape[0]
    qi = pl.program_id(2)
    neg = jnp.finfo(F32).min

    def update(q, kb, vb, m, l, acc, mask_from):
        s = lax.dot_general(q, kb, (((1,), (1,)), ((), ())), preferred_element_type=F32)
        if mask_from is not None:
            row = lax.broadcasted_iota(jnp.int32, s.shape, 0)
            col = lax.broadcasted_iota(jnp.int32, s.shape, 1)
            s = jnp.where(col <= row + mask_from, s, neg)
        m_new = jnp.maximum(m, jnp.max(s, axis=-1, keepdims=True))
        alpha = jnp.exp2(m - m_new)
        p = jnp.exp2(s - m_new)
        l_new = alpha * l + jnp.sum(p, axis=-1, keepdims=True)
        acc_new = alpha * acc + _dot(p.astype(BF16), vb)
        return m_new, l_new, acc_new

    def full_step(ki, carry):
        k0 = pl.multiple_of(ki * tq, tq)
        return update(q_ref[...], k_ref[pl.ds(k0, tq), :], v_ref[pl.ds(k0, tq), :], *carry, None)

    init = (jnp.full((tq, 1), neg, F32), jnp.zeros((tq, 1), F32), jnp.zeros((tq, V_HEAD), F32))
    m, l, acc = lax.fori_loop(0, qi, full_step, init)
    k0 = pl.multiple_of(qi * tq, tq)
    for r in range(tq // ATTN_BAND):
        rows = slice(r * ATTN_BAND, (r + 1) * ATTN_BAND)
        nk = (r + 1) * ATTN_BAND
        _, lr, ar = update(q_ref[rows, :], k_ref[pl.ds(k0, nk), :], v_ref[pl.ds(k0, nk), :],
                           m[rows], l[rows], acc[rows], r * ATTN_BAND)
        o_ref[rows, :] = (ar / lr).astype(BF16)


def _attention(q, k, v, batch, seq):
    tq = min(TQ, seq)
    q3 = q.reshape(batch, seq, MLA_HEADS * QK_PAD)
    k3 = k.reshape(batch, seq, MLA_HEADS * QK_PAD)
    v3 = v.reshape(batch, seq, MLA_WIDTH)
    out = pl.pallas_call(
        _attn_kernel,
        grid=(batch, MLA_HEADS, seq // tq),
        in_specs=[
            pl.BlockSpec((None, tq, QK_PAD), lambda b, h, i: (b, i, h)),
            pl.BlockSpec((None, seq, QK_PAD), lambda b, h, i: (b, 0, h)),
            pl.BlockSpec((None, seq, V_HEAD), lambda b, h, i: (b, 0, h)),
        ],
        out_specs=pl.BlockSpec((None, tq, V_HEAD), lambda b, h, i: (b, i, h)),
        out_shape=jax.ShapeDtypeStruct((batch, seq, MLA_WIDTH), BF16),
        compiler_params=_params(("parallel", "parallel", "arbitrary"), 56),
        name="causal_attention",
    )(q3, k3, v3)
    return out.reshape(batch * seq, MLA_WIDTH)


def _mixer_post(x, a, o, mod_ref, wout_ref, nffn_ref):
    half = a.shape[1]
    y = _dot(a, wout_ref[:half, :]) + _dot(o, wout_ref[half:, :])
    x1 = x + mod_ref[2:3, :] * y
    gain = nffn_ref[...] * (1.0 + mod_ref[4:5, :])
    h = _rms(x1, x1.shape[1]) * gain + mod_ref[3:4, :]
    return x1, h


def _dense_ffn_kernel(x_ref, a_ref, o_ref, mod_ref, wout_ref, nffn_ref, w1_ref, w3_ref, w2_ref,
                      out_ref):
    x1, h = _mixer_post(x_ref[...], a_ref[...], o_ref[...], mod_ref, wout_ref, nffn_ref)
    hb = h.astype(BF16)
    dff = w1_ref.shape[1]
    acc = jnp.zeros(x1.shape, F32)
    for c0 in range(0, dff, FF_CHUNK):
        h1 = _dot(hb, w1_ref[:, c0:c0 + FF_CHUNK])
        h3 = _dot(hb, w3_ref[:, c0:c0 + FF_CHUNK])
        act = (_silu(h1) * h3).astype(BF16)
        acc = acc + _dot(act, w2_ref[c0:c0 + FF_CHUNK, :])
    out_ref[...] = x1 + mod_ref[5:6, :] * acc


def _dense_layer_tail(x2, a, o, mod_l, w_out, norm_ffn, w1, w3, w2, seq):
    n, d = x2.shape
    tm = TM_FFN
    per_b = seq // tm
    tok = lambda i: (i, 0)
    single = pl.Buffered(1)
    wspec = lambda shape: pl.BlockSpec(shape, lambda i: (0, 0), pipeline_mode=single)
    dff = w1.shape[1]
    return pl.pallas_call(
        _dense_ffn_kernel,
        grid=(n // tm,),
        in_specs=[
            pl.BlockSpec((tm, d), tok),
            pl.BlockSpec((tm, a.shape[1]), tok),
            pl.BlockSpec((tm, o.shape[1]), tok),
            pl.BlockSpec((None, N_MOD, d), lambda i: (i // per_b, 0, 0)),
            wspec((d, d)),
            _const_spec((1, d)),
            wspec((d, dff)),
            wspec((d, dff)),
            wspec((dff, d)),
        ],
        out_specs=pl.BlockSpec((tm, d), tok),
        out_shape=jax.ShapeDtypeStruct((n, d), F32),
        compiler_params=_params(("parallel",), 56),
        name="dense_ffn",
    )(x2, a, o, mod_l, w_out.astype(BF16), norm_ffn.reshape(1, d),
      w1.astype(BF16), w3.astype(BF16), w2.astype(BF16))


def _router_kernel(x_ref, a_ref, o_ref, mod_ref, wout_ref, nffn_ref, rw_ref,
                   x1_ref, h_ref, info_ref, cnt_ref, carry_ref):
    @pl.when(pl.program_id(0) == 0)
    def _():
        carry_ref[...] = jnp.zeros(carry_ref.shape, F32)

    x1, h = _mixer_post(x_ref[...], a_ref[...], o_ref[...], mod_ref, wout_ref, nffn_ref)
    x1_ref[...] = x1
    h_ref[...] = h
    tm = h.shape[0]

    h_hi = h.astype(BF16)
    h_lo = (h - h_hi.astype(F32)).astype(BF16)
    rw = rw_ref[...]
    rw_hi = rw.astype(BF16)
    rw_lo = (rw - rw_hi.astype(F32)).astype(BF16)
    logits = _dot(h_hi, rw_hi) + _dot(h_hi, rw_lo) + _dot(h_lo, rw_hi)
    lt = logits.T[:N_EXPERTS, :]

    eid = lax.broadcasted_iota(jnp.int32, lt.shape, 0)
    m1 = jnp.max(lt, axis=0, keepdims=True)
    i1 = jnp.min(jnp.where(lt == m1, eid, N_EXPERTS), axis=0, keepdims=True)
    rest = jnp.where(eid == i1, -jnp.inf, lt)
    m2 = jnp.max(rest, axis=0, keepdims=True)
    i2 = jnp.min(jnp.where(rest == m2, eid, N_EXPERTS), axis=0, keepdims=True)
    e2 = jnp.exp(m2 - m1)
    g1 = 1.0 / (1.0 + e2)
    g2 = e2 / (1.0 + e2)

    oh1 = (eid == i1).astype(F32)
    oh2 = (eid == i2).astype(F32)
    r_io = lax.broadcasted_iota(jnp.int32, (tm, tm), 0)
    c_io = lax.broadcasted_iota(jnp.int32, (tm, tm), 1)
    before = jnp.where(r_io < c_io, 1.0, 0.0).astype(BF16)
    cnt1 = jnp.sum(oh1, axis=1, keepdims=True)
    cnt2 = jnp.sum(oh2, axis=1, keepdims=True)
    base = carry_ref[:, 0:1]
    rank1_e = _dot(oh1.astype(BF16), before) + base
    rank2_e = _dot(oh2.astype(BF16), before) + base + cnt1
    rank1 = jnp.sum(oh1 * rank1_e, axis=0, keepdims=True)
    rank2 = jnp.sum(oh2 * rank2_e, axis=0, keepdims=True)
    total = base + cnt1 + cnt2
    carry_ref[...] = jnp.broadcast_to(total, carry_ref.shape)
    cnt_ref[...] = jnp.broadcast_to(total, cnt_ref.shape)

    zero = jnp.zeros_like(g1)
    info_ref[0] = jnp.concatenate(
        [i1.astype(F32), i2.astype(F32), g1, g2, rank1, rank2, zero, zero], axis=0)


def _router(x2, a, o, mod_l, w_out, norm_ffn, router_w, seq):
    n, d = x2.shape
    tm = TM_ROUTE
    per_b = seq // tm
    tok = lambda i: (i, 0)
    rw_pad = jnp.pad(router_w, ((0, 0), (0, LANES - N_EXPERTS)))
    return pl.pallas_call(
        _router_kernel,
        grid=(n // tm,),
        in_specs=[
            pl.BlockSpec((tm, d), tok),
            pl.BlockSpec((tm, a.shape[1]), tok),
            pl.BlockSpec((tm, o.shape[1]), tok),
            pl.BlockSpec((None, N_MOD, d), lambda i: (i // per_b, 0, 0)),
            _const_spec((d, d)),
            _const_spec((1, d)),
            _const_spec((d, LANES)),
        ],
        out_specs=[
            pl.BlockSpec((tm, d), tok),
            pl.BlockSpec((tm, d), tok),
            pl.BlockSpec((1, 8, tm), lambda i: (i, 0, 0)),
            pl.BlockSpec((N_EXPERTS, LANES), lambda i: (0, 0)),
        ],
        out_shape=[
            jax.ShapeDtypeStruct((n, d), F32),
            jax.ShapeDtypeStruct((n, d), F32),
            jax.ShapeDtypeStruct((n // tm, 8, tm), F32),
            jax.ShapeDtypeStruct((N_EXPERTS, LANES), F32),
        ],
        scratch_shapes=[pltpu.VMEM((N_EXPERTS, LANES), F32)],
        compiler_params=_params(("arbitrary",), 48),
        name="mixer_post_router",
    )(x2, a, o, mod_l, w_out.astype(BF16), norm_ffn.reshape(1, d), rw_pad)


def _dispatch_kernel(lo_ref, hi_ref, p1_ref, p2_ref, h_ref, xs_hbm, zbuf, sem, zsem):
    tm = p1_ref.shape[2]

    @pl.when(pl.program_id(0) == 0)
    def _():
        zbuf[...] = jnp.zeros(zbuf.shape, F32)

        def fill(r, _):
            pltpu.make_async_copy(zbuf.at[pl.ds(0, 1), :], xs_hbm.at[pl.ds(r, 1), :], zsem).start()
            return 0

        for g in range(lo_ref.shape[0]):
            lax.fori_loop(lo_ref[g], hi_ref[g], fill, 0)
        slack = N_EXPERTS * TG
        pltpu.make_async_copy(xs_hbm.at[pl.ds(0, slack), :], xs_hbm.at[pl.ds(0, slack), :], zsem).wait()

    def issue(t, _):
        src = h_ref.at[pl.ds(t, 1), :]
        pltpu.make_async_copy(src, xs_hbm.at[pl.ds(p1_ref[0, 0, t], 1), :], sem).start()
        pltpu.make_async_copy(src, xs_hbm.at[pl.ds(p2_ref[0, 0, t], 1), :], sem).start()
        return 0

    lax.fori_loop(0, tm, issue, 0, unroll=8)
    for _ in range(TOP_K):
        pltpu.make_async_copy(h_ref, xs_hbm.at[pl.ds(0, tm), :], sem).wait()


def _dispatch(h, pos1, pos2, pad_lo, pad_hi, rows):
    n, d = h.shape
    tm = TM_ROUTE
    idx_spec = pl.BlockSpec((1, 1, tm), lambda i, lo, hi: (i, 0, 0), memory_space=pltpu.SMEM)
    grid_spec = pltpu.PrefetchScalarGridSpec(
        num_scalar_prefetch=2,
        grid=(n // tm,),
        in_specs=[idx_spec, idx_spec, pl.BlockSpec((tm, d), lambda i, lo, hi: (i, 0))],
        out_specs=pl.BlockSpec(memory_space=pl.ANY),
        scratch_shapes=[pltpu.VMEM((8, d), F32), pltpu.SemaphoreType.DMA(()),
                        pltpu.SemaphoreType.DMA(())],
    )
    return pl.pallas_call(
        _dispatch_kernel,
        grid_spec=grid_spec,
        out_shape=jax.ShapeDtypeStruct((rows, d), F32),
        compiler_params=pltpu.CompilerParams(dimension_semantics=("arbitrary",),
                                             has_side_effects=True),
        name="moe_dispatch",
    )(pad_lo, pad_hi, pos1.reshape(n // tm, 1, tm), pos2.reshape(n // tm, 1, tm), h)


def _expert_kernel(te_ref, tv_ref, xs_ref, w1_ref, w3_ref, w2_ref, y_ref):
    i = pl.program_id(0)
    valid = tv_ref[i]

    @pl.when(valid > 0)
    def _():
        xb = xs_ref[...].astype(BF16)
        dff = w1_ref.shape[1]
        acc = jnp.zeros(xs_ref.shape, F32)
        for c0 in range(0, dff, FF_CHUNK):
            h1 = _dot(xb, w1_ref[:, c0:c0 + FF_CHUNK])
            h3 = _dot(xb, w3_ref[:, c0:c0 + FF_CHUNK])
            act = (_silu(h1) * h3).astype(BF16)
            acc = acc + _dot(act, w2_ref[c0:c0 + FF_CHUNK, :])
        y_ref[...] = acc

    @pl.when(valid <= 0)
    def _():
        y_ref[...] = jnp.zeros(y_ref.shape, F32)


def _experts(xs, tile_expert, tile_valid, w1, w3, w2):
    rows, d = xs.shape
    tg = TG
    dff = w1.shape[2]
    grid_spec = pltpu.PrefetchScalarGridSpec(
        num_scalar_prefetch=2,
        grid=(rows // tg,),
        in_specs=[
            pl.BlockSpec((tg, d), lambda i, te, tv: (i, 0)),
            pl.BlockSpec((None, d, dff), lambda i, te, tv: (te[i], 0, 0)),
            pl.BlockSpec((None, d, dff), lambda i, te, tv: (te[i], 0, 0)),
            pl.BlockSpec((None, dff, d), lambda i, te, tv: (te[i], 0, 0)),
        ],
        out_specs=pl.BlockSpec((tg, d), lambda i, te, tv: (i, 0)),
    )
    return pl.pallas_call(
        _expert_kernel,
        grid_spec=grid_spec,
        out_shape=jax.ShapeDtypeStruct((rows, d), F32),
        compiler_params=_params(("arbitrary",), 56),
        name="moe_experts",
    )(tile_expert, tile_valid, xs, w1.astype(BF16), w3.astype(BF16), w2.astype(BF16))


def _combine_kernel(p1_cur, p2_cur, p1_nxt, p2_nxt, x1_ref, info_ref, mod_ref, y_hbm, out_ref,
                    buf1, buf2, sems):
    i = pl.program_id(0)
    tm = x1_ref.shape[0]
    slot = lax.rem(i, 2)

    def issue(p1_ref, p2_ref, s):
        def body(t, _):
            pltpu.make_async_copy(y_hbm.at[pl.ds(p1_ref[0, 0, t], 1), :],
                                  buf1.at[s, pl.ds(t, 1), :], sems.at[s]).start()
            pltpu.make_async_copy(y_hbm.at[pl.ds(p2_ref[0, 0, t], 1), :],
                                  buf2.at[s, pl.ds(t, 1), :], sems.at[s]).start()
            return 0

        lax.fori_loop(0, tm, body, 0, unroll=8)

    @pl.when(i == 0)
    def _():
        issue(p1_cur, p2_cur, 0)

    @pl.when(i + 1 < pl.num_programs(0))
    def _():
        issue(p1_nxt, p2_nxt, 1 - slot)

    pltpu.make_async_copy(y_hbm.at[pl.ds(0, tm), :], buf1.at[slot], sems.at[slot]).wait()
    pltpu.make_async_copy(y_hbm.at[pl.ds(0, tm), :], buf2.at[slot], sems.at[slot]).wait()
    info = info_ref[0]
    pad = jnp.zeros((LANES - info.shape[0], tm), F32)
    cols = jnp.concatenate([info, pad], axis=0).T
    y = cols[:, 2:3] * buf1[slot] + cols[:, 3:4] * buf2[slot]
    out_ref[...] = x1_ref[...] + mod_ref[5:6, :] * y


def _combine(x1, y, pos1, pos2, info, mod_l, seq):
    n, d = x1.shape
    tm = TM_COMB
    per_b = seq // tm
    nb = n // tm
    tok = lambda i: (i, 0)
    cur = pl.BlockSpec((1, 1, tm), lambda i: (i, 0, 0), memory_space=pltpu.SMEM)
    nxt = pl.BlockSpec((1, 1, tm), lambda i: (jnp.minimum(i + 1, nb - 1), 0, 0), memory_space=pltpu.SMEM)
    per_info = info.shape[2] // tm
    p1 = pos1.reshape(nb, 1, tm)
    p2 = pos2.reshape(nb, 1, tm)
    return pl.pallas_call(
        _combine_kernel,
        grid=(nb,),
        in_specs=[
            cur, cur, nxt, nxt,
            pl.BlockSpec((tm, d), tok),
            pl.BlockSpec((1, info.shape[1], tm), lambda i: (i // per_info, 0, i % per_info)),
            pl.BlockSpec((None, N_MOD, d), lambda i: (i // per_b, 0, 0)),
            pl.BlockSpec(memory_space=pl.ANY),
        ],
        out_specs=pl.BlockSpec((tm, d), tok),
        out_shape=jax.ShapeDtypeStruct((n, d), F32),
        scratch_shapes=[pltpu.VMEM((2, tm, d), F32), pltpu.VMEM((2, tm, d), F32),
                        pltpu.SemaphoreType.DMA((2,))],
        compiler_params=_params(("arbitrary",), 40),
        name="moe_combine",
    )(p1, p2, p1, p2, x1, info, mod_l, y)


def _moe_layer_tail(x2, a, o, mod_l, w_out, norm_ffn, router_w, w1, w3, w2, seq):
    n, d = x2.shape
    x1, h, info, counts = _router(x2, a, o, mod_l, w_out, norm_ffn, router_w, seq)

    field = lambda r: info[:, r, :].reshape(n).astype(jnp.int32)
    cnt = counts[:, 0].astype(jnp.int32)
    padded = ((cnt + TG - 1) // TG) * TG
    ends = jnp.cumsum(padded)
    starts = ends - padded
    pos1 = starts[field(0)] + field(4)
    pos2 = starts[field(1)] + field(5)
    rows = TOP_K * n + N_EXPERTS * TG
    tile_start = jnp.arange(rows // TG, dtype=jnp.int32) * TG
    tile_expert = jnp.minimum(jnp.sum((ends[None, :] <= tile_start[:, None]).astype(jnp.int32), axis=1),
                              N_EXPERTS - 1)
    tile_valid = jnp.clip(starts[tile_expert] + cnt[tile_expert] - tile_start, 0, TG).astype(jnp.int32)

    pad_lo = jnp.concatenate([starts + cnt, ends[-1:]]).astype(jnp.int32)
    pad_hi = jnp.concatenate([ends, jnp.full((1,), rows, ends.dtype)]).astype(jnp.int32)

    xs = _dispatch(h, pos1, pos2, pad_lo, pad_hi, rows)
    y = _experts(xs, tile_expert, tile_valid, w1, w3, w2)
    return _combine(x1, y, pos1, pos2, info, mod_l, seq)


def kernel(x, c, positions, ada_w, ada_b, norm_mix, norm_ffn, w_in, sgu_norm, sgu_w, sgu_b,
           q_lat_norm, kv_lat_norm, w_uq, w_ukv, q_norm, k_norm, w_out,
           ffn_w1, ffn_w3, ffn_w2, router_w, moe_w1, moe_w3, moe_w2):
    batch, seq, d = x.shape
    depth = ada_w.shape[0]
    mod = _modulation(c, ada_w, ada_b)
    cos_t, sin_t = _rope_tables(positions)
    x2 = x.reshape(batch * seq, d)
    for layer in range(depth):
        a, q, k, v = _mixer_pre(x2, mod[layer], norm_mix[layer], w_in[layer], sgu_norm[layer],
                                sgu_w[layer], sgu_b[layer], q_lat_norm[layer], kv_lat_norm[layer],
                                w_uq[layer], w_ukv[layer], q_norm[layer], k_norm[layer],
                                cos_t, sin_t, seq)
        o = _attention(q, k, v, batch, seq)
        i = layer // 2
        if layer % 2 == 0:
            x2 = _dense_layer_tail(x2, a, o, mod[layer], w_out[layer], norm_ffn[layer],
                                   ffn_w1[i], ffn_w3[i], ffn_w2[i], seq)
        else:
            x2 = _moe_layer_tail(x2, a, o, mod[layer], w_out[layer], norm_ffn[layer], router_w[i],
                                 moe_w1[i], moe_w3[i], moe_w2[i], seq)
    return x2.reshape(batch, seq, d)
```

```python
import functools
import math

import jax
import jax.numpy as jnp
from jax import lax
from jax.experimental import pallas as pl
from jax.experimental.pallas import tpu as pltpu

F32 = jnp.float32
BF16 = jnp.bfloat16

EPS = 1e-6
ROPE_BASE = 10000.0
SGU_GROUPS = 8
SGU_GROUP_DIM = 64
SGU_WIDTH = SGU_GROUPS * SGU_GROUP_DIM
CHUNK = 128
MLA_HEADS = 4
QK_NOPE = 128
QK_ROPE = 64
QK_HEAD = QK_NOPE + QK_ROPE
QK_PAD = 256
V_HEAD = 128
MLA_WIDTH = MLA_HEADS * V_HEAD
Q_LORA = 256
KV_LORA = 128
N_MOD = 6
N_EXPERTS = 8
TOP_K = 2
LANES = 128
D_IN_PAD = 2 * SGU_WIDTH + Q_LORA + KV_LORA + LANES

MIB = 1024 * 1024

TM_MIX = 1024
MIX_SUB = 512
TQ = 4096
ATTN_BAND = 512
TM_FFN = 512
TM_ROUTE = 512
TG = 256
TM_COMB = 256
FF_CHUNK = 2816


def _dot(a, b):
    return jnp.dot(a, b, preferred_element_type=F32)


def _gelu_tanh(x):
    k = -2.0 * math.sqrt(2.0 / math.pi) * math.log2(math.e)
    return x * (1.0 / (1.0 + jnp.exp2(x * (k + (k * 0.044715) * (x * x)))))


def _silu(x):
    return x * (1.0 / (1.0 + jnp.exp(-x)))


def _row_sum(v):
    part = v[:, :LANES]
    for j in range(LANES, v.shape[1], LANES):
        part = part + v[:, j:j + LANES]
    return _dot(part.astype(BF16), jnp.ones((LANES, LANES), BF16))


def _lanes(v, width):
    return jnp.concatenate([v] * (width // LANES), axis=1)


def _scale_rows(x, s):
    return jnp.concatenate([x[:, j:j + LANES] * s for j in range(0, x.shape[1], LANES)], axis=1)


def _rms(x, width):
    rs = lax.rsqrt(_row_sum(x * x) * (1.0 / width) + EPS)
    return _scale_rows(x, rs)


def _params(sem, vmem_mib):
    return pltpu.CompilerParams(dimension_semantics=sem, vmem_limit_bytes=vmem_mib * MIB)


def _const_spec(shape):
    nd = len(shape)
    return pl.BlockSpec(shape, lambda *_: (0,) * nd)


def _mod_kernel(c_ref, w_ref, b_ref, o_ref):
    c = c_ref[...]
    ca = _silu(c)
    ca_hi = ca.astype(BF16)
    ca_lo = (ca - ca_hi.astype(F32)).astype(BF16)
    w = w_ref[...]
    w_hi = w.astype(BF16)
    w_lo = (w - w_hi.astype(F32)).astype(BF16)
    acc = _dot(ca_hi, w_hi) + _dot(ca_hi, w_lo) + _dot(ca_lo, w_hi)
    o_ref[...] = acc + b_ref[...]


def _modulation(c, ada_w, ada_b):
    L, D, W = ada_w.shape
    B = c.shape[0]
    rows = 16
    c_pad = jnp.zeros((rows, D), F32).at[:B].set(c)
    tn = 1536
    out = pl.pallas_call(
        _mod_kernel,
        grid=(L, W // tn),
        in_specs=[
            pl.BlockSpec((rows, D), lambda l, j: (0, 0)),
            pl.BlockSpec((None, D, tn), lambda l, j: (l, 0, j)),
            pl.BlockSpec((None, 1, tn), lambda l, j: (l, 0, j)),
        ],
        out_specs=pl.BlockSpec((None, rows, tn), lambda l, j: (l, 0, j)),
        out_shape=jax.ShapeDtypeStruct((L, rows, W), F32),
        compiler_params=_params(("parallel", "parallel"), 40),
        name="adaln_mod",
    )(c_pad, ada_w, ada_b.reshape(L, 1, W))
    return out[:, :B].reshape(L, B, N_MOD, D)


def _rope_kernel(pos_ref, inv_ref, cos_ref, sin_ref):
    pos = pos_ref[0].astype(F32)
    ang = inv_ref[...] * pos
    co = jnp.cos(ang)
    si = jnp.sin(ang)
    z = jnp.zeros((2 * co.shape[0], co.shape[1]), F32)
    cos_ref[...] = jnp.concatenate([co, co, z], axis=0).T
    sin_ref[...] = jnp.concatenate([-si, si, z], axis=0).T


def _rope_tables(positions):
    n = positions.size
    tn = 512
    half = QK_ROPE // 2
    inv_freq = 1.0 / (ROPE_BASE ** (jnp.arange(0, QK_ROPE, 2, dtype=F32) / QK_ROPE))
    pos3 = positions.reshape(n // tn, 1, tn)
    return pl.pallas_call(
        _rope_kernel,
        grid=(n // tn,),
        in_specs=[
            pl.BlockSpec((1, 1, tn), lambda i: (i, 0, 0)),
            pl.BlockSpec((half, 1), lambda i: (0, 0)),
        ],
        out_specs=[pl.BlockSpec((tn, LANES), lambda i: (i, 0))] * 2,
        out_shape=[jax.ShapeDtypeStruct((n, LANES), F32)] * 2,
        compiler_params=_params(("parallel",), 32),
        name="rope_tables",
    )(pos3, inv_freq.reshape(half, 1))


def _rope_rotate(r, cos_t, sin_t, lane):
    partner = jnp.where(lane < QK_ROPE // 2,
                        pltpu.roll(r, LANES - QK_ROPE // 2, 1),
                        pltpu.roll(r, QK_ROPE // 2, 1))
    return r * cos_t + partner * sin_t


def _mixer_pre_kernel(x_ref, mod_ref, nmix_ref, win_ref, sgn_ref, sgw_ref, sgb_ref,
                      qln_ref, kvln_ref, wuq_ref, wuk_ref, wuv_ref, qn_ref, kn_ref,
                      cos_ref, sin_ref, a_ref, q_ref, k_ref, v_ref):
    for r0 in range(0, x_ref.shape[0], MIX_SUB):
        rows = pl.ds(r0, MIX_SUB)
        _mixer_pre_rows(x_ref.at[rows], mod_ref, nmix_ref, win_ref, sgn_ref, sgw_ref, sgb_ref,
                        qln_ref, kvln_ref, wuq_ref, wuk_ref, wuv_ref, qn_ref, kn_ref,
                        cos_ref.at[rows], sin_ref.at[rows],
                        a_ref.at[rows], q_ref.at[rows], k_ref.at[rows], v_ref.at[rows])


def _mixer_pre_rows(x_ref, mod_ref, nmix_ref, win_ref, sgn_ref, sgw_ref, sgb_ref,
                    qln_ref, kvln_ref, wuq_ref, wuk_ref, wuv_ref, qn_ref, kn_ref,
                    cos_ref, sin_ref, a_ref, q_ref, k_ref, v_ref):
    tm, d = x_ref.shape
    gain = nmix_ref[...] * (1.0 + mod_ref[1:2, :])
    h = _rms(x_ref[...], d) * gain + mod_ref[0:1, :]
    proj = _dot(h.astype(BF16), win_ref[...])

    o_zv = SGU_WIDTH
    o_cq = 2 * SGU_WIDTH
    o_ckv = o_cq + Q_LORA
    o_kr = o_ckv + KV_LORA

    u = _gelu_tanh(proj[:, :SGU_WIDTH])
    gv = _gelu_tanh(proj[:, o_zv:o_cq])
    cen = gv - _lanes(_row_sum(gv) * (1.0 / SGU_WIDTH), SGU_WIDTH)
    var = _row_sum(cen * cen) * (1.0 / SGU_WIDTH)
    vn = cen * _lanes(lax.rsqrt(var + EPS), SGU_WIDTH) * sgn_ref[...]

    row = lax.broadcasted_iota(jnp.int32, (CHUNK, CHUNK), 0)
    col = lax.broadcasted_iota(jnp.int32, (CHUNK, CHUNK), 1)
    causal = col <= row
    lane = lax.broadcasted_iota(jnp.int32, (CHUNK, LANES), 1)
    low_half = lane < SGU_GROUP_DIM
    n_pairs = SGU_GROUPS // 2
    wcat = []
    for j in range(n_pairs):
        wa = jnp.where(causal, sgw_ref[2 * j], 0.0)
        wb = jnp.where(causal, sgw_ref[2 * j + 1], 0.0)
        wcat.append(jnp.concatenate([wa, wb], axis=1).astype(BF16))
    for c in range(tm // CHUNK):
        r0 = c * CHUNK
        for j in range(n_pairs):
            l0 = j * LANES
            vb = vn[r0:r0 + CHUNK, l0:l0 + LANES]
            rhs = jnp.concatenate([jnp.where(low_half, vb, 0.0),
                                   jnp.where(low_half, 0.0, vb)], axis=0).astype(BF16)
            s = _dot(wcat[j], rhs) + sgb_ref[:, l0:l0 + LANES]
            a_ref[r0:r0 + CHUNK, l0:l0 + LANES] = (u[r0:r0 + CHUNK, l0:l0 + LANES] * s).astype(BF16)

    lane_t = lax.broadcasted_iota(jnp.int32, (tm, LANES), 1)
    cos_t = cos_ref[...]
    sin_t = sin_ref[...]
    q_scale = QK_HEAD ** -0.5 * math.log2(math.e)

    cqn = _rms(proj[:, o_cq:o_ckv], Q_LORA) * qln_ref[...]
    qf = _dot(cqn.astype(BF16), wuq_ref[...])
    ckvn = _rms(proj[:, o_ckv:o_kr], KV_LORA) * kvln_ref[...]
    ckvb = ckvn.astype(BF16)
    kf = _dot(ckvb, wuk_ref[...])
    v_ref[...] = _dot(ckvb, wuv_ref[...]).astype(BF16)
    kr = proj[:, o_kr:o_kr + LANES]

    qn_lo = qn_ref[:, :QK_NOPE]
    qn_hi = qn_ref[:, QK_NOPE:]
    kn_lo = kn_ref[:, :QK_NOPE]
    kr_sq = kr * kr
    kr_rot = _rope_rotate(kr * kn_ref[:, QK_NOPE:], cos_t, sin_t, lane_t)
    for hd in range(MLA_HEADS):
        q_lo = qf[:, hd * QK_PAD:hd * QK_PAD + QK_NOPE]
        q_hi = qf[:, hd * QK_PAD + QK_NOPE:(hd + 1) * QK_PAD]
        rq = lax.rsqrt(_row_sum(q_lo * q_lo + q_hi * q_hi) * (1.0 / QK_HEAD) + EPS) * q_scale
        q_ref[:, hd * QK_PAD:hd * QK_PAD + QK_NOPE] = (q_lo * rq * qn_lo).astype(BF16)
        q_ref[:, hd * QK_PAD + QK_NOPE:(hd + 1) * QK_PAD] = _rope_rotate(
            q_hi * rq * qn_hi, cos_t, sin_t, lane_t).astype(BF16)
        k_lo = kf[:, hd * QK_NOPE:(hd + 1) * QK_NOPE]
        rk = lax.rsqrt(_row_sum(k_lo * k_lo + kr_sq) * (1.0 / QK_HEAD) + EPS)
        k_ref[:, hd * QK_PAD:hd * QK_PAD + QK_NOPE] = (k_lo * rk * kn_lo).astype(BF16)
        k_ref[:, hd * QK_PAD + QK_NOPE:(hd + 1) * QK_PAD] = (kr_rot * rk).astype(BF16)


def _mixer_pre(x2, mod_l, norm_mix, w_in, sgu_norm, sgu_w, sgu_b, q_lat_norm, kv_lat_norm,
               w_uq, w_ukv, q_norm, k_norm, cos_t, sin_t, seq):
    n, d = x2.shape
    tm = TM_MIX
    per_b = seq // tm
    win_p = jnp.pad(w_in, ((0, 0), (0, D_IN_PAD - w_in.shape[1]))).astype(BF16)
    wuq_p = jnp.pad(w_uq.reshape(Q_LORA, MLA_HEADS, QK_HEAD),
                    ((0, 0), (0, 0), (0, QK_PAD - QK_HEAD))).reshape(Q_LORA, MLA_HEADS * QK_PAD).astype(BF16)
    wukv = w_ukv.reshape(KV_LORA, MLA_HEADS, QK_NOPE + V_HEAD)
    wuk = wukv[:, :, :QK_NOPE].reshape(KV_LORA, MLA_HEADS * QK_NOPE).astype(BF16)
    wuv = wukv[:, :, QK_NOPE:].reshape(KV_LORA, MLA_WIDTH).astype(BF16)
    qn_p = jnp.pad(q_norm, (0, QK_PAD - QK_HEAD)).reshape(1, QK_PAD)
    kn_p = jnp.pad(k_norm, (0, QK_PAD - QK_HEAD)).reshape(1, QK_PAD)
    sgb_full = jnp.repeat(sgu_b.T, SGU_GROUP_DIM, axis=1)

    tok = lambda i: (i, 0)
    in_specs = [
        pl.BlockSpec((tm, d), tok),
        pl.BlockSpec((None, N_MOD, d), lambda i: (i // per_b, 0, 0)),
        _const_spec((1, d)),
        _const_spec(win_p.shape),
        _const_spec((1, SGU_WIDTH)),
        _const_spec(sgu_w.shape),
        _const_spec(sgb_full.shape),
        _const_spec((1, Q_LORA)),
        _const_spec((1, KV_LORA)),
        _const_spec(wuq_p.shape),
        _const_spec(wuk.shape),
        _const_spec(wuv.shape),
        _const_spec((1, QK_PAD)),
        _const_spec((1, QK_PAD)),
        pl.BlockSpec((tm, LANES), tok),
        pl.BlockSpec((tm, LANES), tok),
    ]
    out_shape = [
        jax.ShapeDtypeStruct((n, SGU_WIDTH), BF16),
        jax.ShapeDtypeStruct((n, MLA_HEADS * QK_PAD), BF16),
        jax.ShapeDtypeStruct((n, MLA_HEADS * QK_PAD), BF16),
        jax.ShapeDtypeStruct((n, MLA_WIDTH), BF16),
    ]
    out_specs = [pl.BlockSpec((tm, s.shape[1]), tok) for s in out_shape]
    return pl.pallas_call(
        _mixer_pre_kernel,
        grid=(n // tm,),
        in_specs=in_specs,
        out_specs=out_specs,
        out_shape=out_shape,
        compiler_params=_params(("parallel",), 48),
        name="mixer_pre",
    )(x2, mod_l, norm_mix.reshape(1, d), win_p, sgu_norm.reshape(1, SGU_WIDTH), sgu_w, sgb_full,
      q_lat_norm.reshape(1, Q_LORA), kv_lat_norm.reshape(1, KV_LORA), wuq_p, wuk, wuv, qn_p, kn_p,
      cos_t, sin_t)


def _attn_kernel(q_ref, k_ref, v_ref, o_ref):
    tq = q_ref.shape[0]
    qi = pl.program_id(2)
    neg = jnp.finfo(F32).min

    def update(q, kb, vb, m, l, acc, mask_from):
        s = lax.dot_general(q, kb, (((1,), (1,)), ((), ())), preferred_element_type=F32)
        if mask_from is not None:
            row = lax.broadcasted_iota(jnp.int32, s.shape, 0)
            col = lax.broadcasted_iota(jnp.int32, s.shape, 1)
            s = jnp.where(col <= row + mask_from, s, neg)
        m_new = jnp.maximum(m, jnp.max(s, axis=-1, keepdims=True))
        alpha = jnp.exp2(m - m_new)
        p = jnp.exp2(s - m_new)
        l_new = alpha * l + jnp.sum(p, axis=-1, keepdims=True)
        acc_new = alpha * acc + _dot(p.astype(BF16), vb)
        return m_new, l_new, acc_new

    def full_step(ki, carry):
        k0 = pl.multiple_of(ki * tq, tq)
        return update(q_ref[...], k_ref[pl.ds(k0, tq), :], v_ref[pl.ds(k0, tq), :], *carry, None)

    init = (jnp.full((tq, 1), neg, F32), jnp.zeros((tq, 1), F32), jnp.zeros((tq, V_HEAD), F32))
    m, l, acc = lax.fori_loop(0, qi, full_step, init)
    k0 = pl.multiple_of(qi * tq, tq)
    for r in range(tq // ATTN_BAND):
        rows = slice(r * ATTN_BAND, (r + 1) * ATTN_BAND)
        nk = (r + 1) * ATTN_BAND
        _, lr, ar = update(q_ref[rows, :], k_ref[pl.ds(k0, nk), :], v_ref[pl.ds(k0, nk), :],
                           m[rows], l[rows], acc[rows], r * ATTN_BAND)
        o_ref[rows, :] = (ar / lr).astype(BF16)


def _attention(q, k, v, batch, seq):
    tq = min(TQ, seq)
    q3 = q.reshape(batch, seq, MLA_HEADS * QK_PAD)
    k3 = k.reshape(batch, seq, MLA_HEADS * QK_PAD)
    v3 = v.reshape(batch, seq, MLA_WIDTH)
    out = pl.pallas_call(
        _attn_kernel,
        grid=(batch, MLA_HEADS, seq // tq),
        in_specs=[
            pl.BlockSpec((None, tq, QK_PAD), lambda b, h, i: (b, i, h)),
            pl.BlockSpec((None, seq, QK_PAD), lambda b, h, i: (b, 0, h)),
            pl.BlockSpec((None, seq, V_HEAD), lambda b, h, i: (b, 0, h)),
        ],
        out_specs=pl.BlockSpec((None, tq, V_HEAD), lambda b, h, i: (b, i, h)),
        out_shape=jax.ShapeDtypeStruct((batch, seq, MLA_WIDTH), BF16),
        compiler_params=_params(("parallel", "parallel", "arbitrary"), 56),
        name="causal_attention",
    )(q3, k3, v3)
    return out.reshape(batch * seq, MLA_WIDTH)


def _mixer_post(x, a, o, mod_ref, wout_ref, nffn_ref):
    half = a.shape[1]
    y = _dot(a, wout_ref[:half, :]) + _dot(o, wout_ref[half:, :])
    x1 = x + mod_ref[2:3, :] * y
    gain = nffn_ref[...] * (1.0 + mod_ref[4:5, :])
    h = _rms(x1, x1.shape[1]) * gain + mod_ref[3:4, :]
    return x1, h


def _dense_ffn_kernel(x_ref, a_ref, o_ref, mod_ref, wout_ref, nffn_ref, w1_ref, w3_ref, w2_ref,
                      out_ref):
    x1, h = _mixer_post(x_ref[...], a_ref[...], o_ref[...], mod_ref, wout_ref, nffn_ref)
    hb = h.astype(BF16)
    dff = w1_ref.shape[1]
    acc = jnp.zeros(x1.shape, F32)
    for c0 in range(0, dff, FF_CHUNK):
        h1 = _dot(hb, w1_ref[:, c0:c0 + FF_CHUNK])
        h3 = _dot(hb, w3_ref[:, c0:c0 + FF_CHUNK])
        act = (_silu(h1) * h3).astype(BF16)
        acc = acc + _dot(act, w2_ref[c0:c0 + FF_CHUNK, :])
    out_ref[...] = x1 + mod_ref[5:6, :] * acc


def _dense_layer_tail(x2, a, o, mod_l, w_out, norm_ffn, w1, w3, w2, seq):
    n, d = x2.shape
    tm = TM_FFN
    per_b = seq // tm
    tok = lambda i: (i, 0)
    single = pl.Buffered(1)
    wspec = lambda shape: pl.BlockSpec(shape, lambda i: (0, 0), pipeline_mode=single)
    dff = w1.shape[1]
    return pl.pallas_call(
        _dense_ffn_kernel,
        grid=(n // tm,),
        in_specs=[
            pl.BlockSpec((tm, d), tok),
            pl.BlockSpec((tm, a.shape[1]), tok),
            pl.BlockSpec((tm, o.shape[1]), tok),
            pl.BlockSpec((None, N_MOD, d), lambda i: (i // per_b, 0, 0)),
            wspec((d, d)),
            _const_spec((1, d)),
            wspec((d, dff)),
            wspec((d, dff)),
            wspec((dff, d)),
        ],
        out_specs=pl.BlockSpec((tm, d), tok),
        out_shape=jax.ShapeDtypeStruct((n, d), F32),
        compiler_params=_params(("parallel",), 56),
        name="dense_ffn",
    )(x2, a, o, mod_l, w_out.astype(BF16), norm_ffn.reshape(1, d),
      w1.astype(BF16), w3.astype(BF16), w2.astype(BF16))


def _router_kernel(x_ref, a_ref, o_ref, mod_ref, wout_ref, nffn_ref, rw_ref,
                   x1_ref, h_ref, info_ref, cnt_ref, carry_ref):
    @pl.when(pl.program_id(0) == 0)
    def _():
        carry_ref[...] = jnp.zeros(carry_ref.shape, F32)

    x1, h = _mixer_post(x_ref[...], a_ref[...], o_ref[...], mod_ref, wout_ref, nffn_ref)
    x1_ref[...] = x1
    h_ref[...] = h
    tm = h.shape[0]

    h_hi = h.astype(BF16)
    h_lo = (h - h_hi.astype(F32)).astype(BF16)
    rw = rw_ref[...]
    rw_hi = rw.astype(BF16)
    rw_lo = (rw - rw_hi.astype(F32)).astype(BF16)
    logits = _dot(h_hi, rw_hi) + _dot(h_hi, rw_lo) + _dot(h_lo, rw_hi)
    lt = logits.T[:N_EXPERTS, :]

    eid = lax.broadcasted_iota(jnp.int32, lt.shape, 0)
    m1 = jnp.max(lt, axis=0, keepdims=True)
    i1 = jnp.min(jnp.where(lt == m1, eid, N_EXPERTS), axis=0, keepdims=True)
    rest = jnp.where(eid == i1, -jnp.inf, lt)
    m2 = jnp.max(rest, axis=0, keepdims=True)
    i2 = jnp.min(jnp.where(rest == m2, eid, N_EXPERTS), axis=0, keepdims=True)
    e2 = jnp.exp(m2 - m1)
    g1 = 1.0 / (1.0 + e2)
    g2 = e2 / (1.0 + e2)

    oh1 = (eid == i1).astype(F32)
    oh2 = (eid == i2).astype(F32)
    r_io = lax.broadcasted_iota(jnp.int32, (tm, tm), 0)
    c_io = lax.broadcasted_iota(jnp.int32, (tm, tm), 1)
    before = jnp.where(r_io < c_io, 1.0, 0.0).astype(BF16)
    cnt1 = jnp.sum(oh1, axis=1, keepdims=True)
    cnt2 = jnp.sum(oh2, axis=1, keepdims=True)
    base = carry_ref[:, 0:1]
    rank1_e = _dot(oh1.astype(BF16), before) + base
    rank2_e = _dot(oh2.astype(BF16), before) + base + cnt1
    rank1 = jnp.sum(oh1 * rank1_e, axis=0, keepdims=True)
    rank2 = jnp.sum(oh2 * rank2_e, axis=0, keepdims=True)
    total = base + cnt1 + cnt2
    carry_ref[...] = jnp.broadcast_to(total, carry_ref.shape)
    cnt_ref[...] = jnp.broadcast_to(total, cnt_ref.shape)

    zero = jnp.zeros_like(g1)
    info_ref[0] = jnp.concatenate(
        [i1.astype(F32), i2.astype(F32), g1, g2, rank1, rank2, zero, zero], axis=0)


def _router(x2, a, o, mod_l, w_out, norm_ffn, router_w, seq):
    n, d = x2.shape
    tm = TM_ROUTE
    per_b = seq // tm
    tok = lambda i: (i, 0)
    rw_pad = jnp.pad(router_w, ((0, 0), (0, LANES - N_EXPERTS)))
    return pl.pallas_call(
        _router_kernel,
        grid=(n // tm,),
        in_specs=[
            pl.BlockSpec((tm, d), tok),
            pl.BlockSpec((tm, a.shape[1]), tok),
            pl.BlockSpec((tm, o.shape[1]), tok),
            pl.BlockSpec((None, N_MOD, d), lambda i: (i // per_b, 0, 0)),
            _const_spec((d, d)),
            _const_spec((1, d)),
            _const_spec((d, LANES)),
        ],
        out_specs=[
            pl.BlockSpec((tm, d), tok),
            pl.BlockSpec((tm, d), tok),
            pl.BlockSpec((1, 8, tm), lambda i: (i, 0, 0)),
            pl.BlockSpec((N_EXPERTS, LANES), lambda i: (0, 0)),
        ],
        out_shape=[
            jax.ShapeDtypeStruct((n, d), F32),
            jax.ShapeDtypeStruct((n, d), F32),
            jax.ShapeDtypeStruct((n // tm, 8, tm), F32),
            jax.ShapeDtypeStruct((N_EXPERTS, LANES), F32),
        ],
        scratch_shapes=[pltpu.VMEM((N_EXPERTS, LANES), F32)],
        compiler_params=_params(("arbitrary",), 48),
        name="mixer_post_router",
    )(x2, a, o, mod_l, w_out.astype(BF16), norm_ffn.reshape(1, d), rw_pad)


def _dispatch_kernel(lo_ref, hi_ref, p1_ref, p2_ref, h_ref, w1_ref, w3_ref, w2_ref,
                     xs_hbm, w1b_ref, w3b_ref, w2b_ref, zbuf, sem, zsem):
    tm = p1_ref.shape[2]
    w1b_ref[...] = w1_ref[...].astype(BF16)
    w3b_ref[...] = w3_ref[...].astype(BF16)
    w2b_ref[...] = w2_ref[...].astype(BF16)

    @pl.when(pl.program_id(0) == 0)
    def _():
        zbuf[...] = jnp.zeros(zbuf.shape, F32)

        def fill(r, _):
            pltpu.make_async_copy(zbuf.at[pl.ds(0, 1), :], xs_hbm.at[pl.ds(r, 1), :], zsem).start()
            return 0

        for g in range(lo_ref.shape[0]):
            lax.fori_loop(lo_ref[g], hi_ref[g], fill, 0)
        slack = N_EXPERTS * TG
        pltpu.make_async_copy(xs_hbm.at[pl.ds(0, slack), :], xs_hbm.at[pl.ds(0, slack), :], zsem).wait()

    def issue(t, _):
        src = h_ref.at[pl.ds(t, 1), :]
        pltpu.make_async_copy(src, xs_hbm.at[pl.ds(p1_ref[0, 0, t], 1), :], sem).start()
        pltpu.make_async_copy(src, xs_hbm.at[pl.ds(p2_ref[0, 0, t], 1), :], sem).start()
        return 0

    lax.fori_loop(0, tm, issue, 0, unroll=8)
    for _ in range(TOP_K):
        pltpu.make_async_copy(h_ref, xs_hbm.at[pl.ds(0, tm), :], sem).wait()


def _dispatch(h, pos1, pos2, pad_lo, pad_hi, rows, w1, w3, w2):
    n, d = h.shape
    tm = TM_ROUTE
    steps = n // tm
    idx_spec = pl.BlockSpec((1, 1, tm), lambda i, lo, hi: (i, 0, 0), memory_space=pltpu.SMEM)
    flat = [w.reshape(-1, w.shape[-1]) for w in (w1, w3, w2)]
    slab = lambda w: pl.BlockSpec((w.shape[0] // steps, w.shape[1]), lambda i, lo, hi: (i, 0))
    grid_spec = pltpu.PrefetchScalarGridSpec(
        num_scalar_prefetch=2,
        grid=(steps,),
        in_specs=[idx_spec, idx_spec, pl.BlockSpec((tm, d), lambda i, lo, hi: (i, 0))]
        + [slab(w) for w in flat],
        out_specs=[pl.BlockSpec(memory_space=pl.ANY)] + [slab(w) for w in flat],
        scratch_shapes=[pltpu.VMEM((8, d), F32), pltpu.SemaphoreType.DMA(()),
                        pltpu.SemaphoreType.DMA(())],
    )
    xs, w1b, w3b, w2b = pl.pallas_call(
        _dispatch_kernel,
        grid_spec=grid_spec,
        out_shape=[jax.ShapeDtypeStruct((rows, d), F32)]
        + [jax.ShapeDtypeStruct(w.shape, BF16) for w in flat],
        compiler_params=pltpu.CompilerParams(dimension_semantics=("arbitrary",),
                                             vmem_limit_bytes=48 * MIB, has_side_effects=True),
        name="moe_dispatch",
    )(pad_lo, pad_hi, pos1.reshape(steps, 1, tm), pos2.reshape(steps, 1, tm), h, *flat)
    return xs, w1b.reshape(w1.shape), w3b.reshape(w3.shape), w2b.reshape(w2.shape)


def _expert_kernel(te_ref, tv_ref, xs_ref, w1_ref, w3_ref, w2_ref, y_ref):
    i = pl.program_id(0)
    valid = tv_ref[i]

    @pl.when(valid > 0)
    def _():
        xb = xs_ref[...].astype(BF16)
        dff = w1_ref.shape[1]
        acc = jnp.zeros(xs_ref.shape, F32)
        for c0 in range(0, dff, FF_CHUNK):
            h1 = _dot(xb, w1_ref[:, c0:c0 + FF_CHUNK])
            h3 = _dot(xb, w3_ref[:, c0:c0 + FF_CHUNK])
            act = (_silu(h1) * h3).astype(BF16)
            acc = acc + _dot(act, w2_ref[c0:c0 + FF_CHUNK, :])
        y_ref[...] = acc

    @pl.when(valid <= 0)
    def _():
        y_ref[...] = jnp.zeros(y_ref.shape, F32)


def _experts(xs, tile_expert, tile_valid, w1, w3, w2):
    rows, d = xs.shape
    tg = TG
    dff = w1.shape[2]
    grid_spec = pltpu.PrefetchScalarGridSpec(
        num_scalar_prefetch=2,
        grid=(rows // tg,),
        in_specs=[
            pl.BlockSpec((tg, d), lambda i, te, tv: (i, 0)),
            pl.BlockSpec((None, d, dff), lambda i, te, tv: (te[i], 0, 0)),
            pl.BlockSpec((None, d, dff), lambda i, te, tv: (te[i], 0, 0)),
            pl.BlockSpec((None, dff, d), lambda i, te, tv: (te[i], 0, 0)),
        ],
        out_specs=pl.BlockSpec((tg, d), lambda i, te, tv: (i, 0)),
    )
    return pl.pallas_call(
        _expert_kernel,
        grid_spec=grid_spec,
        out_shape=jax.ShapeDtypeStruct((rows, d), F32),
        compiler_params=_params(("arbitrary",), 56),
        name="moe_experts",
    )(tile_expert, tile_valid, xs, w1, w3, w2)


def _combine_kernel(p1_cur, p2_cur, p1_nxt, p2_nxt, x1_ref, info_ref, mod_ref, y_hbm, out_ref,
                    buf1, buf2, sems):
    i = pl.program_id(0)
    tm = x1_ref.shape[0]
    slot = lax.rem(i, 2)

    def issue(p1_ref, p2_ref, s):
        def body(t, _):
            pltpu.make_async_copy(y_hbm.at[pl.ds(p1_ref[0, 0, t], 1), :],
                                  buf1.at[s, pl.ds(t, 1), :], sems.at[s]).start()
            pltpu.make_async_copy(y_hbm.at[pl.ds(p2_ref[0, 0, t], 1), :],
                                  buf2.at[s, pl.ds(t, 1), :], sems.at[s]).start()
            return 0

        lax.fori_loop(0, tm, body, 0, unroll=8)

    @pl.when(i == 0)
    def _():
        issue(p1_cur, p2_cur, 0)

    @pl.when(i + 1 < pl.num_programs(0))
    def _():
        issue(p1_nxt, p2_nxt, 1 - slot)

    pltpu.make_async_copy(y_hbm.at[pl.ds(0, tm), :], buf1.at[slot], sems.at[slot]).wait()
    pltpu.make_async_copy(y_hbm.at[pl.ds(0, tm), :], buf2.at[slot], sems.at[slot]).wait()
    info = info_ref[0]
    pad = jnp.zeros((LANES - info.shape[0], tm), F32)
    cols = jnp.concatenate([info, pad], axis=0).T
    y = cols[:, 2:3] * buf1[slot] + cols[:, 3:4] * buf2[slot]
    out_ref[...] = x1_ref[...] + mod_ref[5:6, :] * y


def _combine(x1, y, pos1, pos2, info, mod_l, seq):
    n, d = x1.shape
    tm = TM_COMB
    per_b = seq // tm
    nb = n // tm
    tok = lambda i: (i, 0)
    cur = pl.BlockSpec((1, 1, tm), lambda i: (i, 0, 0), memory_space=pltpu.SMEM)
    nxt = pl.BlockSpec((1, 1, tm), lambda i: (jnp.minimum(i + 1, nb - 1), 0, 0), memory_space=pltpu.SMEM)
    per_info = info.shape[2] // tm
    p1 = pos1.reshape(nb, 1, tm)
    p2 = pos2.reshape(nb, 1, tm)
    return pl.pallas_call(
        _combine_kernel,
        grid=(nb,),
        in_specs=[
            cur, cur, nxt, nxt,
            pl.BlockSpec((tm, d), tok),
            pl.BlockSpec((1, info.shape[1], tm), lambda i: (i // per_info, 0, i % per_info)),
            pl.BlockSpec((None, N_MOD, d), lambda i: (i // per_b, 0, 0)),
            pl.BlockSpec(memory_space=pl.ANY),
        ],
        out_specs=pl.BlockSpec((tm, d), tok),
        out_shape=jax.ShapeDtypeStruct((n, d), F32),
        scratch_shapes=[pltpu.VMEM((2, tm, d), F32), pltpu.VMEM((2, tm, d), F32),
                        pltpu.SemaphoreType.DMA((2,))],
        compiler_params=_params(("arbitrary",), 40),
        name="moe_combine",
    )(p1, p2, p1, p2, x1, info, mod_l, y)


def _moe_layer_tail(x2, a, o, mod_l, w_out, norm_ffn, router_w, w1, w3, w2, seq):
    n, d = x2.shape
    x1, h, info, counts = _router(x2, a, o, mod_l, w_out, norm_ffn, router_w, seq)

    field = lambda r: info[:, r, :].reshape(n).astype(jnp.int32)
    cnt = counts[:, 0].astype(jnp.int32)
    padded = ((cnt + TG - 1) // TG) * TG
    ends = jnp.cumsum(padded)
    starts = ends - padded
    pos1 = starts[field(0)] + field(4)
    pos2 = starts[field(1)] + field(5)
    rows = TOP_K * n + N_EXPERTS * TG
    tile_start = jnp.arange(rows // TG, dtype=jnp.int32) * TG
    tile_expert = jnp.minimum(jnp.sum((ends[None, :] <= tile_start[:, None]).astype(jnp.int32), axis=1),
                              N_EXPERTS - 1)
    tile_valid = jnp.clip(starts[tile_expert] + cnt[tile_expert] - tile_start, 0, TG).astype(jnp.int32)

    pad_lo = jnp.concatenate([starts + cnt, ends[-1:]]).astype(jnp.int32)
    pad_hi = jnp.concatenate([ends, jnp.full((1,), rows, ends.dtype)]).astype(jnp.int32)

    xs, w1b, w3b, w2b = _dispatch(h, pos1, pos2, pad_lo, pad_hi, rows, w1, w3, w2)
    y = _experts(xs, tile_expert, tile_valid, w1b, w3b, w2b)
    return _combine(x1, y, pos1, pos2, info, mod_l, seq)


def kernel(x, c, positions, ada_w, ada_b, norm_mix, norm_ffn, w_in, sgu_norm, sgu_w, sgu_b,
           q_lat_norm, kv_lat_norm, w_uq, w_ukv, q_norm, k_norm, w_out,
           ffn_w1, ffn_w3, ffn_w2, router_w, moe_w1, moe_w3, moe_w2):
    batch, seq, d = x.shape
    depth = ada_w.shape[0]
    mod = _modulation(c, ada_w, ada_b)
    cos_t, sin_t = _rope_tables(positions)
    x2 = x.reshape(batch * seq, d)
    for layer in range(depth):
        a, q, k, v = _mixer_pre(x2, mod[layer], norm_mix[layer], w_in[layer], sgu_norm[layer],
                                sgu_w[layer], sgu_b[layer], q_lat_norm[layer], kv_lat_norm[layer],
                                w_uq[layer], w_ukv[layer], q_norm[layer], k_norm[layer],
                                cos_t, sin_t, seq)
        o = _attention(q, k, v, batch, seq)
        i = layer // 2
        if layer % 2 == 0:
            x2 = _dense_layer_tail(x2, a, o, mod[layer], w_out[layer], norm_ffn[layer],
                                   ffn_w1[i], ffn_w3[i], ffn_w2[i], seq)
        else:
            x2 = _moe_layer_tail(x2, a, o, mod[layer], w_out[layer], norm_ffn[layer], router_w[i],
                                 moe_w1[i], moe_w3[i], moe_w2[i], seq)
    return x2.reshape(batch, seq, d)
```

```python
import functools
import math

import jax
import jax.numpy as jnp
from jax import lax
from jax.experimental import pallas as pl
from jax.experimental.pallas import tpu as pltpu

F32 = jnp.float32
BF16 = jnp.bfloat16

EPS = 1e-6
ROPE_BASE = 10000.0
SGU_GROUPS = 8
SGU_GROUP_DIM = 64
SGU_WIDTH = SGU_GROUPS * SGU_GROUP_DIM
CHUNK = 128
MLA_HEADS = 4
QK_NOPE = 128
QK_ROPE = 64
QK_HEAD = QK_NOPE + QK_ROPE
QK_PAD = 256
V_HEAD = 128
MLA_WIDTH = MLA_HEADS * V_HEAD
Q_LORA = 256
KV_LORA = 128
N_MOD = 6
N_EXPERTS = 8
TOP_K = 2
LANES = 128
D_IN_PAD = 2 * SGU_WIDTH + Q_LORA + KV_LORA + LANES

MIB = 1024 * 1024

TM_MIX = 1024
MIX_SUB = 512
TQ = 4096
ATTN_BAND = 512
TM_FFN = 512
TM_ROUTE = 512
TM_DISPATCH = 1024
TG = 256
TM_COMB = 512
FF_CHUNK = 2816


def _dot(a, b):
    return jnp.dot(a, b, preferred_element_type=F32)


def _gelu_tanh(x):
    k = -2.0 * math.sqrt(2.0 / math.pi) * math.log2(math.e)
    return x * (1.0 / (1.0 + jnp.exp2(x * (k + (k * 0.044715) * (x * x)))))


def _silu(x):
    return x * (1.0 / (1.0 + jnp.exp(-x)))


def _row_sum(v):
    return _dot(_fold_lane_tiles(v).astype(BF16), jnp.ones((LANES, LANES), BF16))


def _fold_lane_tiles(v):
    part = v[:, :LANES]
    for j in range(LANES, v.shape[1], LANES):
        part = part + v[:, j:j + LANES]
    return part


def _row_sum_pair(va, vb):
    parts = jnp.concatenate([_fold_lane_tiles(va), _fold_lane_tiles(vb)], axis=1).astype(BF16)
    r = lax.broadcasted_iota(jnp.int32, (2 * LANES, 2 * LANES), 0) // LANES
    c = lax.broadcasted_iota(jnp.int32, (2 * LANES, 2 * LANES), 1) // LANES
    both = _dot(parts, jnp.where(r == c, 1.0, 0.0).astype(BF16))
    return both[:, :LANES], both[:, LANES:]


def _lanes(v, width):
    return jnp.concatenate([v] * (width // LANES), axis=1)


def _scale_rows(x, s):
    return jnp.concatenate([x[:, j:j + LANES] * s for j in range(0, x.shape[1], LANES)], axis=1)


def _rms(x, width):
    rs = lax.rsqrt(_row_sum(x * x) * (1.0 / width) + EPS)
    return _scale_rows(x, rs)


def _params(sem, vmem_mib):
    return pltpu.CompilerParams(dimension_semantics=sem, vmem_limit_bytes=vmem_mib * MIB)


def _const_spec(shape):
    nd = len(shape)
    return pl.BlockSpec(shape, lambda *_: (0,) * nd)


def _mod_kernel(c_ref, w_ref, b_ref, o_ref):
    c = c_ref[...]
    ca = _silu(c)
    ca_hi = ca.astype(BF16)
    ca_lo = (ca - ca_hi.astype(F32)).astype(BF16)
    w = w_ref[...]
    w_hi = w.astype(BF16)
    w_lo = (w - w_hi.astype(F32)).astype(BF16)
    acc = _dot(ca_hi, w_hi) + _dot(ca_hi, w_lo) + _dot(ca_lo, w_hi)
    o_ref[...] = acc + b_ref[...]


def _modulation(c, ada_w, ada_b):
    L, D, W = ada_w.shape
    B = c.shape[0]
    rows = 16
    c_pad = jnp.zeros((rows, D), F32).at[:B].set(c)
    tn = 1536
    out = pl.pallas_call(
        _mod_kernel,
        grid=(L, W // tn),
        in_specs=[
            pl.BlockSpec((rows, D), lambda l, j: (0, 0)),
            pl.BlockSpec((None, D, tn), lambda l, j: (l, 0, j)),
            pl.BlockSpec((None, 1, tn), lambda l, j: (l, 0, j)),
        ],
        out_specs=pl.BlockSpec((None, rows, tn), lambda l, j: (l, 0, j)),
        out_shape=jax.ShapeDtypeStruct((L, rows, W), F32),
        compiler_params=_params(("parallel", "parallel"), 40),
        name="adaln_mod",
    )(c_pad, ada_w, ada_b.reshape(L, 1, W))
    return out[:, :B].reshape(L, B, N_MOD, D)


def _rope_kernel(pos_ref, inv_ref, cos_ref, sin_ref):
    pos = pos_ref[0].astype(F32)
    ang = inv_ref[...] * pos
    co = jnp.cos(ang)
    si = jnp.sin(ang)
    z = jnp.zeros((2 * co.shape[0], co.shape[1]), F32)
    cos_ref[...] = jnp.concatenate([co, co, z], axis=0).T
    sin_ref[...] = jnp.concatenate([-si, si, z], axis=0).T


def _rope_tables(positions):
    n = positions.size
    tn = 512
    half = QK_ROPE // 2
    inv_freq = 1.0 / (ROPE_BASE ** (jnp.arange(0, QK_ROPE, 2, dtype=F32) / QK_ROPE))
    pos3 = positions.reshape(n // tn, 1, tn)
    return pl.pallas_call(
        _rope_kernel,
        grid=(n // tn,),
        in_specs=[
            pl.BlockSpec((1, 1, tn), lambda i: (i, 0, 0)),
            pl.BlockSpec((half, 1), lambda i: (0, 0)),
        ],
        out_specs=[pl.BlockSpec((tn, LANES), lambda i: (i, 0))] * 2,
        out_shape=[jax.ShapeDtypeStruct((n, LANES), F32)] * 2,
        compiler_params=_params(("parallel",), 32),
        name="rope_tables",
    )(pos3, inv_freq.reshape(half, 1))


def _rope_rotate(r, cos_t, sin_t, lane):
    partner = jnp.where(lane < QK_ROPE // 2,
                        pltpu.roll(r, LANES - QK_ROPE // 2, 1),
                        pltpu.roll(r, QK_ROPE // 2, 1))
    return r * cos_t + partner * sin_t


def _mixer_pre_kernel(x_ref, mod_ref, nmix_ref, win_ref, sgn_ref, sgw_ref, sgb_ref,
                      qln_ref, kvln_ref, wuq_ref, wuk_ref, wuv_ref, qn_ref, kn_ref,
                      cos_ref, sin_ref, a_ref, q_ref, k_ref, v_ref):
    for r0 in range(0, x_ref.shape[0], MIX_SUB):
        rows = pl.ds(r0, MIX_SUB)
        _mixer_pre_rows(x_ref.at[rows], mod_ref, nmix_ref, win_ref, sgn_ref, sgw_ref, sgb_ref,
                        qln_ref, kvln_ref, wuq_ref, wuk_ref, wuv_ref, qn_ref, kn_ref,
                        cos_ref.at[rows], sin_ref.at[rows],
                        a_ref.at[rows], q_ref.at[rows], k_ref.at[rows], v_ref.at[rows])


def _mixer_pre_rows(x_ref, mod_ref, nmix_ref, win_ref, sgn_ref, sgw_ref, sgb_ref,
                    qln_ref, kvln_ref, wuq_ref, wuk_ref, wuv_ref, qn_ref, kn_ref,
                    cos_ref, sin_ref, a_ref, q_ref, k_ref, v_ref):
    tm, d = x_ref.shape
    gain = nmix_ref[...] * (1.0 + mod_ref[1:2, :])
    h = _rms(x_ref[...], d) * gain + mod_ref[0:1, :]
    proj = _dot(h.astype(BF16), win_ref[...])

    o_zv = SGU_WIDTH
    o_cq = 2 * SGU_WIDTH
    o_ckv = o_cq + Q_LORA
    o_kr = o_ckv + KV_LORA

    u = _gelu_tanh(proj[:, :SGU_WIDTH])
    gv = _gelu_tanh(proj[:, o_zv:o_cq])
    cq = proj[:, o_cq:o_ckv]
    ckv = proj[:, o_ckv:o_kr]
    gv_sum, cq_ss = _row_sum_pair(gv, cq * cq)
    cen = gv - _lanes(gv_sum * (1.0 / SGU_WIDTH), SGU_WIDTH)
    cen_ss, ckv_ss = _row_sum_pair(cen * cen, ckv * ckv)
    vn = cen * _lanes(lax.rsqrt(cen_ss * (1.0 / SGU_WIDTH) + EPS), SGU_WIDTH) * sgn_ref[...]

    row = lax.broadcasted_iota(jnp.int32, (CHUNK, CHUNK), 0)
    col = lax.broadcasted_iota(jnp.int32, (CHUNK, CHUNK), 1)
    causal = col <= row
    lane = lax.broadcasted_iota(jnp.int32, (CHUNK, LANES), 1)
    low_half = lane < SGU_GROUP_DIM
    n_pairs = SGU_GROUPS // 2
    wcat = []
    for j in range(n_pairs):
        wa = jnp.where(causal, sgw_ref[2 * j], 0.0)
        wb = jnp.where(causal, sgw_ref[2 * j + 1], 0.0)
        wcat.append(jnp.concatenate([wa, wb], axis=1).astype(BF16))
    for c in range(tm // CHUNK):
        r0 = c * CHUNK
        for j in range(n_pairs):
            l0 = j * LANES
            vb = vn[r0:r0 + CHUNK, l0:l0 + LANES]
            rhs = jnp.concatenate([jnp.where(low_half, vb, 0.0),
                                   jnp.where(low_half, 0.0, vb)], axis=0).astype(BF16)
            s = _dot(wcat[j], rhs) + sgb_ref[:, l0:l0 + LANES]
            a_ref[r0:r0 + CHUNK, l0:l0 + LANES] = (u[r0:r0 + CHUNK, l0:l0 + LANES] * s).astype(BF16)

    lane_t = lax.broadcasted_iota(jnp.int32, (tm, LANES), 1)
    cos_t = cos_ref[...]
    sin_t = sin_ref[...]
    q_scale = QK_HEAD ** -0.5 * math.log2(math.e)

    cqn = _scale_rows(cq, lax.rsqrt(cq_ss * (1.0 / Q_LORA) + EPS)) * qln_ref[...]
    qf = _dot(cqn.astype(BF16), wuq_ref[...])
    ckvn = _scale_rows(ckv, lax.rsqrt(ckv_ss * (1.0 / KV_LORA) + EPS)) * kvln_ref[...]
    ckvb = ckvn.astype(BF16)
    kf = _dot(ckvb, wuk_ref[...])
    v_ref[...] = _dot(ckvb, wuv_ref[...]).astype(BF16)
    kr = proj[:, o_kr:o_kr + LANES]

    qn_lo = qn_ref[:, :QK_NOPE]
    qn_hi = qn_ref[:, QK_NOPE:]
    kn_lo = kn_ref[:, :QK_NOPE]
    kr_sq = kr * kr
    kr_rot = _rope_rotate(kr * kn_ref[:, QK_NOPE:], cos_t, sin_t, lane_t)
    for hd in range(MLA_HEADS):
        q_lo = qf[:, hd * QK_PAD:hd * QK_PAD + QK_NOPE]
        q_hi = qf[:, hd * QK_PAD + QK_NOPE:(hd + 1) * QK_PAD]
        k_lo = kf[:, hd * QK_NOPE:(hd + 1) * QK_NOPE]
        q_ss, k_ss = _row_sum_pair(q_lo * q_lo + q_hi * q_hi, k_lo * k_lo + kr_sq)
        rq = lax.rsqrt(q_ss * (1.0 / QK_HEAD) + EPS) * q_scale
        rk = lax.rsqrt(k_ss * (1.0 / QK_HEAD) + EPS)
        q_ref[:, hd * QK_PAD:hd * QK_PAD + QK_NOPE] = (q_lo * rq * qn_lo).astype(BF16)
        q_ref[:, hd * QK_PAD + QK_NOPE:(hd + 1) * QK_PAD] = _rope_rotate(
            q_hi * rq * qn_hi, cos_t, sin_t, lane_t).astype(BF16)
        k_ref[:, hd * QK_PAD:hd * QK_PAD + QK_NOPE] = (k_lo * rk * kn_lo).astype(BF16)
        k_ref[:, hd * QK_PAD + QK_NOPE:(hd + 1) * QK_PAD] = (kr_rot * rk).astype(BF16)


def _mixer_pre(x2, mod_l, norm_mix, w_in, sgu_norm, sgu_w, sgu_b, q_lat_norm, kv_lat_norm,
               w_uq, w_ukv, q_norm, k_norm, cos_t, sin_t, seq):
    n, d = x2.shape
    tm = TM_MIX
    per_b = seq // tm
    win_p = jnp.pad(w_in, ((0, 0), (0, D_IN_PAD - w_in.shape[1]))).astype(BF16)
    wuq_p = jnp.pad(w_uq.reshape(Q_LORA, MLA_HEADS, QK_HEAD),
                    ((0, 0), (0, 0), (0, QK_PAD - QK_HEAD))).reshape(Q_LORA, MLA_HEADS * QK_PAD).astype(BF16)
    wukv = w_ukv.reshape(KV_LORA, MLA_HEADS, QK_NOPE + V_HEAD)
    wuk = wukv[:, :, :QK_NOPE].reshape(KV_LORA, MLA_HEADS * QK_NOPE).astype(BF16)
    wuv = wukv[:, :, QK_NOPE:].reshape(KV_LORA, MLA_WIDTH).astype(BF16)
    qn_p = jnp.pad(q_norm, (0, QK_PAD - QK_HEAD)).reshape(1, QK_PAD)
    kn_p = jnp.pad(k_norm, (0, QK_PAD - QK_HEAD)).reshape(1, QK_PAD)
    sgb_full = jnp.repeat(sgu_b.T, SGU_GROUP_DIM, axis=1)

    tok = lambda i: (i, 0)
    in_specs = [
        pl.BlockSpec((tm, d), tok),
        pl.BlockSpec((None, N_MOD, d), lambda i: (i // per_b, 0, 0)),
        _const_spec((1, d)),
        _const_spec(win_p.shape),
        _const_spec((1, SGU_WIDTH)),
        _const_spec(sgu_w.shape),
        _const_spec(sgb_full.shape),
        _const_spec((1, Q_LORA)),
        _const_spec((1, KV_LORA)),
        _const_spec(wuq_p.shape),
        _const_spec(wuk.shape),
        _const_spec(wuv.shape),
        _const_spec((1, QK_PAD)),
        _const_spec((1, QK_PAD)),
        pl.BlockSpec((tm, LANES), tok),
        pl.BlockSpec((tm, LANES), tok),
    ]
    out_shape = [
        jax.ShapeDtypeStruct((n, SGU_WIDTH), BF16),
        jax.ShapeDtypeStruct((n, MLA_HEADS * QK_PAD), BF16),
        jax.ShapeDtypeStruct((n, MLA_HEADS * QK_PAD), BF16),
        jax.ShapeDtypeStruct((n, MLA_WIDTH), BF16),
    ]
    out_specs = [pl.BlockSpec((tm, s.shape[1]), tok) for s in out_shape]
    return pl.pallas_call(
        _mixer_pre_kernel,
        grid=(n // tm,),
        in_specs=in_specs,
        out_specs=out_specs,
        out_shape=out_shape,
        compiler_params=_params(("parallel",), 48),
        name="mixer_pre",
    )(x2, mod_l, norm_mix.reshape(1, d), win_p, sgu_norm.reshape(1, SGU_WIDTH), sgu_w, sgb_full,
      q_lat_norm.reshape(1, Q_LORA), kv_lat_norm.reshape(1, KV_LORA), wuq_p, wuk, wuv, qn_p, kn_p,
      cos_t, sin_t)


def _attn_kernel(q_ref, k_ref, v_ref, o_ref):
    tq = q_ref.shape[0]
    qi = pl.program_id(2)
    neg = jnp.finfo(F32).min

    def update(q, kb, vb, m, l, acc, mask_from):
        s = lax.dot_general(q, kb, (((1,), (1,)), ((), ())), preferred_element_type=F32)
        if mask_from is not None:
            row = lax.broadcasted_iota(jnp.int32, s.shape, 0)
            col = lax.broadcasted_iota(jnp.int32, s.shape, 1)
            s = jnp.where(col <= row + mask_from, s, neg)
        m_new = jnp.maximum(m, jnp.max(s, axis=-1, keepdims=True))
        alpha = jnp.exp2(m - m_new)
        p = jnp.exp2(s - m_new)
        l_new = alpha * l + jnp.sum(p, axis=-1, keepdims=True)
        acc_new = alpha * acc + _dot(p.astype(BF16), vb)
        return m_new, l_new, acc_new

    def full_step(ki, carry):
        k0 = pl.multiple_of(ki * tq, tq)
        return update(q_ref[...], k_ref[pl.ds(k0, tq), :], v_ref[pl.ds(k0, tq), :], *carry, None)

    init = (jnp.full((tq, 1), neg, F32), jnp.zeros((tq, 1), F32), jnp.zeros((tq, V_HEAD), F32))
    m, l, acc = lax.fori_loop(0, qi, full_step, init)
    k0 = pl.multiple_of(qi * tq, tq)
    for r in range(tq // ATTN_BAND):
        rows = slice(r * ATTN_BAND, (r + 1) * ATTN_BAND)
        nk = (r + 1) * ATTN_BAND
        _, lr, ar = update(q_ref[rows, :], k_ref[pl.ds(k0, nk), :], v_ref[pl.ds(k0, nk), :],
                           m[rows], l[rows], acc[rows], r * ATTN_BAND)
        o_ref[rows, :] = (ar / lr).astype(BF16)


def _attention(q, k, v, batch, seq):
    tq = min(TQ, seq)
    q3 = q.reshape(batch, seq, MLA_HEADS * QK_PAD)
    k3 = k.reshape(batch, seq, MLA_HEADS * QK_PAD)
    v3 = v.reshape(batch, seq, MLA_WIDTH)
    out = pl.pallas_call(
        _attn_kernel,
        grid=(batch, MLA_HEADS, seq // tq),
        in_specs=[
            pl.BlockSpec((None, tq, QK_PAD), lambda b, h, i: (b, i, h)),
            pl.BlockSpec((None, seq, QK_PAD), lambda b, h, i: (b, 0, h)),
            pl.BlockSpec((None, seq, V_HEAD), lambda b, h, i: (b, 0, h)),
        ],
        out_specs=pl.BlockSpec((None, tq, V_HEAD), lambda b, h, i: (b, i, h)),
        out_shape=jax.ShapeDtypeStruct((batch, seq, MLA_WIDTH), BF16),
        compiler_params=_params(("parallel", "parallel", "arbitrary"), 56),
        name="causal_attention",
    )(q3, k3, v3)
    return out.reshape(batch * seq, MLA_WIDTH)


def _mixer_post(x, a, o, mod_ref, wout_ref, nffn_ref):
    half = a.shape[1]
    y = _dot(a, wout_ref[:half, :]) + _dot(o, wout_ref[half:, :])
    x1 = x + mod_ref[2:3, :] * y
    gain = nffn_ref[...] * (1.0 + mod_ref[4:5, :])
    h = _rms(x1, x1.shape[1]) * gain + mod_ref[3:4, :]
    return x1, h


def _dense_ffn_kernel(x_ref, a_ref, o_ref, mod_ref, wout_ref, nffn_ref, w1_ref, w3_ref, w2_ref,
                      out_ref):
    x1, h = _mixer_post(x_ref[...], a_ref[...], o_ref[...], mod_ref, wout_ref, nffn_ref)
    hb = h.astype(BF16)
    dff = w1_ref.shape[1]
    acc = jnp.zeros(x1.shape, F32)
    for c0 in range(0, dff, FF_CHUNK):
        h1 = _dot(hb, w1_ref[:, c0:c0 + FF_CHUNK])
        h3 = _dot(hb, w3_ref[:, c0:c0 + FF_CHUNK])
        act = (_silu(h1) * h3).astype(BF16)
        acc = acc + _dot(act, w2_ref[c0:c0 + FF_CHUNK, :])
    out_ref[...] = x1 + mod_ref[5:6, :] * acc


def _dense_layer_tail(x2, a, o, mod_l, w_out, norm_ffn, w1, w3, w2, seq):
    n, d = x2.shape
    tm = TM_FFN
    per_b = seq // tm
    tok = lambda i: (i, 0)
    single = pl.Buffered(1)
    wspec = lambda shape: pl.BlockSpec(shape, lambda i: (0, 0), pipeline_mode=single)
    dff = w1.shape[1]
    return pl.pallas_call(
        _dense_ffn_kernel,
        grid=(n // tm,),
        in_specs=[
            pl.BlockSpec((tm, d), tok),
            pl.BlockSpec((tm, a.shape[1]), tok),
            pl.BlockSpec((tm, o.shape[1]), tok),
            pl.BlockSpec((None, N_MOD, d), lambda i: (i // per_b, 0, 0)),
            wspec((d, d)),
            _const_spec((1, d)),
            wspec((d, dff)),
            wspec((d, dff)),
            wspec((dff, d)),
        ],
        out_specs=pl.BlockSpec((tm, d), tok),
        out_shape=jax.ShapeDtypeStruct((n, d), F32),
        compiler_params=_params(("parallel",), 56),
        name="dense_ffn",
    )(x2, a, o, mod_l, w_out.astype(BF16), norm_ffn.reshape(1, d),
      w1.astype(BF16), w3.astype(BF16), w2.astype(BF16))


def _router_kernel(x_ref, a_ref, o_ref, mod_ref, wout_ref, nffn_ref, rw_ref,
                   x1_ref, h_ref, info_ref, cnt_ref, carry_ref):
    @pl.when(pl.program_id(0) == 0)
    def _():
        carry_ref[...] = jnp.zeros(carry_ref.shape, F32)

    x1, h = _mixer_post(x_ref[...], a_ref[...], o_ref[...], mod_ref, wout_ref, nffn_ref)
    x1_ref[...] = x1
    h_ref[...] = h
    tm = h.shape[0]

    h_hi = h.astype(BF16)
    h_lo = (h - h_hi.astype(F32)).astype(BF16)
    rw = rw_ref[...]
    rw_hi = rw.astype(BF16)
    rw_lo = (rw - rw_hi.astype(F32)).astype(BF16)
    logits = _dot(h_hi, rw_hi) + _dot(h_hi, rw_lo) + _dot(h_lo, rw_hi)
    lt = logits.T[:N_EXPERTS, :]

    eid = lax.broadcasted_iota(jnp.int32, lt.shape, 0)
    m1 = jnp.max(lt, axis=0, keepdims=True)
    i1 = jnp.min(jnp.where(lt == m1, eid, N_EXPERTS), axis=0, keepdims=True)
    rest = jnp.where(eid == i1, -jnp.inf, lt)
    m2 = jnp.max(rest, axis=0, keepdims=True)
    i2 = jnp.min(jnp.where(rest == m2, eid, N_EXPERTS), axis=0, keepdims=True)
    e2 = jnp.exp(m2 - m1)
    g1 = 1.0 / (1.0 + e2)
    g2 = e2 / (1.0 + e2)

    oh1 = (eid == i1).astype(F32)
    oh2 = (eid == i2).astype(F32)
    r_io = lax.broadcasted_iota(jnp.int32, (tm, tm), 0)
    c_io = lax.broadcasted_iota(jnp.int32, (tm, tm), 1)
    before = jnp.where(r_io < c_io, 1.0, 0.0).astype(BF16)
    cnt1 = jnp.sum(oh1, axis=1, keepdims=True)
    cnt2 = jnp.sum(oh2, axis=1, keepdims=True)
    base = carry_ref[:, 0:1]
    rank1_e = _dot(oh1.astype(BF16), before) + base
    rank2_e = _dot(oh2.astype(BF16), before) + base + cnt1
    rank1 = jnp.sum(oh1 * rank1_e, axis=0, keepdims=True)
    rank2 = jnp.sum(oh2 * rank2_e, axis=0, keepdims=True)
    total = base + cnt1 + cnt2
    carry_ref[...] = jnp.broadcast_to(total, carry_ref.shape)
    cnt_ref[...] = jnp.broadcast_to(total, cnt_ref.shape)

    zero = jnp.zeros_like(g1)
    info_ref[0] = jnp.concatenate(
        [i1.astype(F32), i2.astype(F32), g1, g2, rank1, rank2, zero, zero], axis=0)


def _router(x2, a, o, mod_l, w_out, norm_ffn, router_w, seq):
    n, d = x2.shape
    tm = TM_ROUTE
    per_b = seq // tm
    tok = lambda i: (i, 0)
    rw_pad = jnp.pad(router_w, ((0, 0), (0, LANES - N_EXPERTS)))
    return pl.pallas_call(
        _router_kernel,
        grid=(n // tm,),
        in_specs=[
            pl.BlockSpec((tm, d), tok),
            pl.BlockSpec((tm, a.shape[1]), tok),
            pl.BlockSpec((tm, o.shape[1]), tok),
            pl.BlockSpec((None, N_MOD, d), lambda i: (i // per_b, 0, 0)),
            _const_spec((d, d)),
            _const_spec((1, d)),
            _const_spec((d, LANES)),
        ],
        out_specs=[
            pl.BlockSpec((tm, d), tok),
            pl.BlockSpec((tm, d), tok),
            pl.BlockSpec((1, 8, tm), lambda i: (i, 0, 0)),
            pl.BlockSpec((N_EXPERTS, LANES), lambda i: (0, 0)),
        ],
        out_shape=[
            jax.ShapeDtypeStruct((n, d), F32),
            jax.ShapeDtypeStruct((n, d), F32),
            jax.ShapeDtypeStruct((n // tm, 8, tm), F32),
            jax.ShapeDtypeStruct((N_EXPERTS, LANES), F32),
        ],
        scratch_shapes=[pltpu.VMEM((N_EXPERTS, LANES), F32)],
        compiler_params=_params(("arbitrary",), 48),
        name="mixer_post_router",
    )(x2, a, o, mod_l, w_out.astype(BF16), norm_ffn.reshape(1, d), rw_pad)


def _dispatch_kernel(lo_ref, hi_ref, p1_ref, p2_ref, h_ref, w1_ref, w3_ref, w2_ref,
                     xs_hbm, w1b_ref, w3b_ref, w2b_ref, zbuf, sem, zsem):
    tm = p1_ref.shape[2]
    w1b_ref[...] = w1_ref[...].astype(BF16)
    w3b_ref[...] = w3_ref[...].astype(BF16)
    w2b_ref[...] = w2_ref[...].astype(BF16)

    @pl.when(pl.program_id(0) == 0)
    def _():
        zbuf[...] = jnp.zeros(zbuf.shape, F32)

        def fill(r, _):
            pltpu.make_async_copy(zbuf.at[pl.ds(0, 1), :], xs_hbm.at[pl.ds(r, 1), :], zsem).start()
            return 0

        for g in range(lo_ref.shape[0]):
            lax.fori_loop(lo_ref[g], hi_ref[g], fill, 0)
        slack = N_EXPERTS * TG
        pltpu.make_async_copy(xs_hbm.at[pl.ds(0, slack), :], xs_hbm.at[pl.ds(0, slack), :], zsem).wait()

    def issue(t, _):
        src = h_ref.at[pl.ds(t, 1), :]
        pltpu.make_async_copy(src, xs_hbm.at[pl.ds(p1_ref[0, 0, t], 1), :], sem).start()
        pltpu.make_async_copy(src, xs_hbm.at[pl.ds(p2_ref[0, 0, t], 1), :], sem).start()
        return 0

    lax.fori_loop(0, tm, issue, 0, unroll=8)
    for _ in range(TOP_K):
        pltpu.make_async_copy(h_ref, xs_hbm.at[pl.ds(0, tm), :], sem).wait()


def _dispatch(h, pos1, pos2, pad_lo, pad_hi, rows, w1, w3, w2):
    n, d = h.shape
    tm = TM_DISPATCH
    steps = n // tm
    idx_spec = pl.BlockSpec((1, 1, tm), lambda i, lo, hi: (i, 0, 0), memory_space=pltpu.SMEM)
    flat = [w.reshape(-1, w.shape[-1]) for w in (w1, w3, w2)]
    slab = lambda w: pl.BlockSpec((w.shape[0] // steps, w.shape[1]), lambda i, lo, hi: (i, 0))
    grid_spec = pltpu.PrefetchScalarGridSpec(
        num_scalar_prefetch=2,
        grid=(steps,),
        in_specs=[idx_spec, idx_spec, pl.BlockSpec((tm, d), lambda i, lo, hi: (i, 0))]
        + [slab(w) for w in flat],
        out_specs=[pl.BlockSpec(memory_space=pl.ANY)] + [slab(w) for w in flat],
        scratch_shapes=[pltpu.VMEM((8, d), F32), pltpu.SemaphoreType.DMA(()),
                        pltpu.SemaphoreType.DMA(())],
    )
    xs, w1b, w3b, w2b = pl.pallas_call(
        _dispatch_kernel,
        grid_spec=grid_spec,
        out_shape=[jax.ShapeDtypeStruct((rows, d), F32)]
        + [jax.ShapeDtypeStruct(w.shape, BF16) for w in flat],
        compiler_params=pltpu.CompilerParams(dimension_semantics=("arbitrary",),
                                             vmem_limit_bytes=48 * MIB, has_side_effects=True),
        name="moe_dispatch",
    )(pad_lo, pad_hi, pos1.reshape(steps, 1, tm), pos2.reshape(steps, 1, tm), h, *flat)
    return xs, w1b.reshape(w1.shape), w3b.reshape(w3.shape), w2b.reshape(w2.shape)


def _expert_kernel(te_ref, tv_ref, xs_ref, w1_ref, w3_ref, w2_ref, y_ref):
    i = pl.program_id(0)
    valid = tv_ref[i]

    @pl.when(valid > 0)
    def _():
        xb = xs_ref[...].astype(BF16)
        dff = w1_ref.shape[1]
        acc = jnp.zeros(xs_ref.shape, F32)
        for c0 in range(0, dff, FF_CHUNK):
            h1 = _dot(xb, w1_ref[:, c0:c0 + FF_CHUNK])
            h3 = _dot(xb, w3_ref[:, c0:c0 + FF_CHUNK])
            act = (_silu(h1) * h3).astype(BF16)
            acc = acc + _dot(act, w2_ref[c0:c0 + FF_CHUNK, :])
        y_ref[...] = acc

    @pl.when(valid <= 0)
    def _():
        y_ref[...] = jnp.zeros(y_ref.shape, F32)


def _experts(xs, tile_expert, tile_valid, w1, w3, w2):
    rows, d = xs.shape
    tg = TG
    dff = w1.shape[2]
    grid_spec = pltpu.PrefetchScalarGridSpec(
        num_scalar_prefetch=2,
        grid=(rows // tg,),
        in_specs=[
            pl.BlockSpec((tg, d), lambda i, te, tv: (i, 0)),
            pl.BlockSpec((None, d, dff), lambda i, te, tv: (te[i], 0, 0)),
            pl.BlockSpec((None, d, dff), lambda i, te, tv: (te[i], 0, 0)),
            pl.BlockSpec((None, dff, d), lambda i, te, tv: (te[i], 0, 0)),
        ],
        out_specs=pl.BlockSpec((tg, d), lambda i, te, tv: (i, 0)),
    )
    return pl.pallas_call(
        _expert_kernel,
        grid_spec=grid_spec,
        out_shape=jax.ShapeDtypeStruct((rows, d), F32),
        compiler_params=_params(("arbitrary",), 56),
        name="moe_experts",
    )(tile_expert, tile_valid, xs, w1, w3, w2)


def _combine_kernel(p1_cur, p2_cur, p1_nxt, p2_nxt, x1_ref, info_ref, mod_ref, y_hbm, out_ref,
                    buf1, buf2, sems):
    i = pl.program_id(0)
    tm = x1_ref.shape[0]
    slot = lax.rem(i, 2)

    def issue(p1_ref, p2_ref, s):
        def body(t, _):
            pltpu.make_async_copy(y_hbm.at[pl.ds(p1_ref[0, 0, t], 1), :],
                                  buf1.at[s, pl.ds(t, 1), :], sems.at[s]).start()
            pltpu.make_async_copy(y_hbm.at[pl.ds(p2_ref[0, 0, t], 1), :],
                                  buf2.at[s, pl.ds(t, 1), :], sems.at[s]).start()
            return 0

        lax.fori_loop(0, tm, body, 0, unroll=8)

    @pl.when(i == 0)
    def _():
        issue(p1_cur, p2_cur, 0)

    @pl.when(i + 1 < pl.num_programs(0))
    def _():
        issue(p1_nxt, p2_nxt, 1 - slot)

    pltpu.make_async_copy(y_hbm.at[pl.ds(0, tm), :], buf1.at[slot], sems.at[slot]).wait()
    pltpu.make_async_copy(y_hbm.at[pl.ds(0, tm), :], buf2.at[slot], sems.at[slot]).wait()
    info = info_ref[0]
    pad = jnp.zeros((LANES - info.shape[0], tm), F32)
    cols = jnp.concatenate([info, pad], axis=0).T
    y = cols[:, 2:3] * buf1[slot] + cols[:, 3:4] * buf2[slot]
    out_ref[...] = x1_ref[...] + mod_ref[5:6, :] * y


def _combine(x1, y, pos1, pos2, info, mod_l, seq):
    n, d = x1.shape
    tm = TM_COMB
    per_b = seq // tm
    nb = n // tm
    tok = lambda i: (i, 0)
    cur = pl.BlockSpec((1, 1, tm), lambda i: (i, 0, 0), memory_space=pltpu.SMEM)
    nxt = pl.BlockSpec((1, 1, tm), lambda i: (jnp.minimum(i + 1, nb - 1), 0, 0), memory_space=pltpu.SMEM)
    per_info = info.shape[2] // tm
    p1 = pos1.reshape(nb, 1, tm)
    p2 = pos2.reshape(nb, 1, tm)
    return pl.pallas_call(
        _combine_kernel,
        grid=(nb,),
        in_specs=[
            cur, cur, nxt, nxt,
            pl.BlockSpec((tm, d), tok),
            pl.BlockSpec((1, info.shape[1], tm), lambda i: (i // per_info, 0, i % per_info)),
            pl.BlockSpec((None, N_MOD, d), lambda i: (i // per_b, 0, 0)),
            pl.BlockSpec(memory_space=pl.ANY),
        ],
        out_specs=pl.BlockSpec((tm, d), tok),
        out_shape=jax.ShapeDtypeStruct((n, d), F32),
        scratch_shapes=[pltpu.VMEM((2, tm, d), F32), pltpu.VMEM((2, tm, d), F32),
                        pltpu.SemaphoreType.DMA((2,))],
        compiler_params=_params(("arbitrary",), 40),
        name="moe_combine",
    )(p1, p2, p1, p2, x1, info, mod_l, y)


def _moe_layer_tail(x2, a, o, mod_l, w_out, norm_ffn, router_w, w1, w3, w2, seq):
    n, d = x2.shape
    x1, h, info, counts = _router(x2, a, o, mod_l, w_out, norm_ffn, router_w, seq)

    field = lambda r: info[:, r, :].reshape(n).astype(jnp.int32)
    cnt = counts[:, 0].astype(jnp.int32)
    padded = ((cnt + TG - 1) // TG) * TG
    ends = jnp.cumsum(padded)
    starts = ends - padded
    pos1 = starts[field(0)] + field(4)
    pos2 = starts[field(1)] + field(5)
    rows = TOP_K * n + N_EXPERTS * TG
    tile_start = jnp.arange(rows // TG, dtype=jnp.int32) * TG
    tile_expert = jnp.minimum(jnp.sum((ends[None, :] <= tile_start[:, None]).astype(jnp.int32), axis=1),
                              N_EXPERTS - 1)
    tile_valid = jnp.clip(starts[tile_expert] + cnt[tile_expert] - tile_start, 0, TG).astype(jnp.int32)

    pad_lo = jnp.concatenate([starts + cnt, ends[-1:]]).astype(jnp.int32)
    pad_hi = jnp.concatenate([ends, jnp.full((1,), rows, ends.dtype)]).astype(jnp.int32)

    xs, w1b, w3b, w2b = _dispatch(h, pos1, pos2, pad_lo, pad_hi, rows, w1, w3, w2)
    y = _experts(xs, tile_expert, tile_valid, w1b, w3b, w2b)
    return _combine(x1, y, pos1, pos2, info, mod_l, seq)


def kernel(x, c, positions, ada_w, ada_b, norm_mix, norm_ffn, w_in, sgu_norm, sgu_w, sgu_b,
           q_lat_norm, kv_lat_norm, w_uq, w_ukv, q_norm, k_norm, w_out,
           ffn_w1, ffn_w3, ffn_w2, router_w, moe_w1, moe_w3, moe_w2):
    batch, seq, d = x.shape
    depth = ada_w.shape[0]
    mod = _modulation(c, ada_w, ada_b)
    cos_t, sin_t = _rope_tables(positions)
    x2 = x.reshape(batch * seq, d)
    for layer in range(depth):
        a, q, k, v = _mixer_pre(x2, mod[layer], norm_mix[layer], w_in[layer], sgu_norm[layer],
                                sgu_w[layer], sgu_b[layer], q_lat_norm[layer], kv_lat_norm[layer],
                                w_uq[layer], w_ukv[layer], q_norm[layer], k_norm[layer],
                                cos_t, sin_t, seq)
        o = _attention(q, k, v, batch, seq)
        i = layer // 2
        if layer % 2 == 0:
            x2 = _dense_layer_tail(x2, a, o, mod[layer], w_out[layer], norm_ffn[layer],
                                   ffn_w1[i], ffn_w3[i], ffn_w2[i], seq)
        else:
            x2 = _moe_layer_tail(x2, a, o, mod[layer], w_out[layer], norm_ffn[layer], router_w[i],
                                 moe_w1[i], moe_w3[i], moe_w2[i], seq)
    return x2.reshape(batch, seq, d)
```

```python
import functools
import math

import jax
import jax.numpy as jnp
from jax import lax
from jax.experimental import pallas as pl
from jax.experimental.pallas import tpu as pltpu

F32 = jnp.float32
BF16 = jnp.bfloat16

EPS = 1e-6
ROPE_BASE = 10000.0
SGU_GROUPS = 8
SGU_GROUP_DIM = 64
SGU_WIDTH = SGU_GROUPS * SGU_GROUP_DIM
CHUNK = 128
MLA_HEADS = 4
QK_NOPE = 128
QK_ROPE = 64
QK_HEAD = QK_NOPE + QK_ROPE
QK_PAD = 256
V_HEAD = 128
MLA_WIDTH = MLA_HEADS * V_HEAD
Q_LORA = 256
KV_LORA = 128
N_MOD = 6
N_EXPERTS = 8
TOP_K = 2
LANES = 128
D_IN_PAD = 2 * SGU_WIDTH + Q_LORA + KV_LORA + LANES

MIB = 1024 * 1024

TM_MIX = 1024
MIX_SUB = 512
TQ = 4096
ATTN_BAND = 512
TM_FFN = 512
TM_ROUTE = 512
TM_DISPATCH = 1024
TG = 256
TM_COMB = 512
FF_CHUNK = 2816


def _dot(a, b):
    return jnp.dot(a, b, preferred_element_type=F32)


def _gelu_tanh(x):
    k = -2.0 * math.sqrt(2.0 / math.pi) * math.log2(math.e)
    return x * (1.0 / (1.0 + jnp.exp2(x * (k + (k * 0.044715) * (x * x)))))


def _silu(x):
    return x * (1.0 / (1.0 + jnp.exp(-x)))


def _row_sum(v):
    return _dot(_fold_lane_tiles(v).astype(BF16), jnp.ones((LANES, LANES), BF16))


def _fold_lane_tiles(v):
    part = v[:, :LANES]
    for j in range(LANES, v.shape[1], LANES):
        part = part + v[:, j:j + LANES]
    return part


def _row_sum_pair(va, vb):
    parts = jnp.concatenate([_fold_lane_tiles(va), _fold_lane_tiles(vb)], axis=1).astype(BF16)
    r = lax.broadcasted_iota(jnp.int32, (2 * LANES, 2 * LANES), 0) // LANES
    c = lax.broadcasted_iota(jnp.int32, (2 * LANES, 2 * LANES), 1) // LANES
    both = _dot(parts, jnp.where(r == c, 1.0, 0.0).astype(BF16))
    return both[:, :LANES], both[:, LANES:]


def _lanes(v, width):
    return jnp.concatenate([v] * (width // LANES), axis=1)


def _scale_rows(x, s):
    return jnp.concatenate([x[:, j:j + LANES] * s for j in range(0, x.shape[1], LANES)], axis=1)


def _rms(x, width):
    rs = lax.rsqrt(_row_sum(x * x) * (1.0 / width) + EPS)
    return _scale_rows(x, rs)


def _params(sem, vmem_mib):
    return pltpu.CompilerParams(dimension_semantics=sem, vmem_limit_bytes=vmem_mib * MIB)


def _const_spec(shape):
    nd = len(shape)
    return pl.BlockSpec(shape, lambda *_: (0,) * nd)


def _mod_kernel(c_ref, w_ref, b_ref, o_ref):
    c = c_ref[...]
    ca = _silu(c)
    ca_hi = ca.astype(BF16)
    ca_lo = (ca - ca_hi.astype(F32)).astype(BF16)
    w = w_ref[...]
    w_hi = w.astype(BF16)
    w_lo = (w - w_hi.astype(F32)).astype(BF16)
    acc = _dot(ca_hi, w_hi) + _dot(ca_hi, w_lo) + _dot(ca_lo, w_hi)
    o_ref[...] = acc + b_ref[...]


def _modulation(c, ada_w, ada_b):
    L, D, W = ada_w.shape
    B = c.shape[0]
    rows = 16
    c_pad = jnp.zeros((rows, D), F32).at[:B].set(c)
    tn = 1536
    out = pl.pallas_call(
        _mod_kernel,
        grid=(L, W // tn),
        in_specs=[
            pl.BlockSpec((rows, D), lambda l, j: (0, 0)),
            pl.BlockSpec((None, D, tn), lambda l, j: (l, 0, j)),
            pl.BlockSpec((None, 1, tn), lambda l, j: (l, 0, j)),
        ],
        out_specs=pl.BlockSpec((None, rows, tn), lambda l, j: (l, 0, j)),
        out_shape=jax.ShapeDtypeStruct((L, rows, W), F32),
        compiler_params=_params(("parallel", "parallel"), 40),
        name="adaln_mod",
    )(c_pad, ada_w, ada_b.reshape(L, 1, W))
    return out[:, :B].reshape(L, B, N_MOD, D)


def _rope_kernel(pos_ref, inv_ref, cos_ref, sin_ref):
    pos = pos_ref[0].astype(F32)
    ang = inv_ref[...] * pos
    co = jnp.cos(ang)
    si = jnp.sin(ang)
    z = jnp.zeros((2 * co.shape[0], co.shape[1]), F32)
    cos_ref[...] = jnp.concatenate([co, co, z], axis=0).T
    sin_ref[...] = jnp.concatenate([-si, si, z], axis=0).T


def _rope_tables(positions):
    n = positions.size
    tn = min(2048, n)
    half = QK_ROPE // 2
    inv_freq = 1.0 / (ROPE_BASE ** (jnp.arange(0, QK_ROPE, 2, dtype=F32) / QK_ROPE))
    pos3 = positions.reshape(n // tn, 1, tn)
    return pl.pallas_call(
        _rope_kernel,
        grid=(n // tn,),
        in_specs=[
            pl.BlockSpec((1, 1, tn), lambda i: (i, 0, 0)),
            pl.BlockSpec((half, 1), lambda i: (0, 0)),
        ],
        out_specs=[pl.BlockSpec((tn, LANES), lambda i: (i, 0))] * 2,
        out_shape=[jax.ShapeDtypeStruct((n, LANES), F32)] * 2,
        compiler_params=_params(("parallel",), 32),
        name="rope_tables",
    )(pos3, inv_freq.reshape(half, 1))


def _rope_rotate(r, cos_t, sin_t, lane):
    partner = jnp.where(lane < QK_ROPE // 2,
                        pltpu.roll(r, LANES - QK_ROPE // 2, 1),
                        pltpu.roll(r, QK_ROPE // 2, 1))
    return r * cos_t + partner * sin_t


def _mixer_pre_kernel(x_ref, mod_ref, nmix_ref, win_ref, sgn_ref, sgw_ref, sgb_ref,
                      qln_ref, kvln_ref, wuq_ref, wuk_ref, wuv_ref, qn_ref, kn_ref,
                      cos_ref, sin_ref, a_ref, q_ref, k_ref, v_ref):
    for r0 in range(0, x_ref.shape[0], MIX_SUB):
        rows = pl.ds(r0, MIX_SUB)
        _mixer_pre_rows(x_ref.at[rows], mod_ref, nmix_ref, win_ref, sgn_ref, sgw_ref, sgb_ref,
                        qln_ref, kvln_ref, wuq_ref, wuk_ref, wuv_ref, qn_ref, kn_ref,
                        cos_ref.at[rows], sin_ref.at[rows],
                        a_ref.at[rows], q_ref.at[rows], k_ref.at[rows], v_ref.at[rows])


def _mixer_pre_rows(x_ref, mod_ref, nmix_ref, win_ref, sgn_ref, sgw_ref, sgb_ref,
                    qln_ref, kvln_ref, wuq_ref, wuk_ref, wuv_ref, qn_ref, kn_ref,
                    cos_ref, sin_ref, a_ref, q_ref, k_ref, v_ref):
    tm, d = x_ref.shape
    gain = nmix_ref[...] * (1.0 + mod_ref[1:2, :])
    h = _rms(x_ref[...], d) * gain + mod_ref[0:1, :]
    proj = _dot(h.astype(BF16), win_ref[...])

    o_zv = SGU_WIDTH
    o_cq = 2 * SGU_WIDTH
    o_ckv = o_cq + Q_LORA
    o_kr = o_ckv + KV_LORA

    u = _gelu_tanh(proj[:, :SGU_WIDTH])
    gv = _gelu_tanh(proj[:, o_zv:o_cq])
    cq = proj[:, o_cq:o_ckv]
    ckv = proj[:, o_ckv:o_kr]
    gv_sum, cq_ss = _row_sum_pair(gv, cq * cq)
    cen = gv - _lanes(gv_sum * (1.0 / SGU_WIDTH), SGU_WIDTH)
    cen_ss, ckv_ss = _row_sum_pair(cen * cen, ckv * ckv)
    vn = cen * _lanes(lax.rsqrt(cen_ss * (1.0 / SGU_WIDTH) + EPS), SGU_WIDTH) * sgn_ref[...]

    row = lax.broadcasted_iota(jnp.int32, (CHUNK, CHUNK), 0)
    col = lax.broadcasted_iota(jnp.int32, (CHUNK, CHUNK), 1)
    causal = col <= row
    lane = lax.broadcasted_iota(jnp.int32, (CHUNK, LANES), 1)
    low_half = lane < SGU_GROUP_DIM
    n_pairs = SGU_GROUPS // 2
    wcat = []
    for j in range(n_pairs):
        wa = jnp.where(causal, sgw_ref[2 * j], 0.0)
        wb = jnp.where(causal, sgw_ref[2 * j + 1], 0.0)
        wcat.append(jnp.concatenate([wa, wb], axis=1).astype(BF16))
    for c in range(tm // CHUNK):
        r0 = c * CHUNK
        for j in range(n_pairs):
            l0 = j * LANES
            vb = vn[r0:r0 + CHUNK, l0:l0 + LANES]
            rhs = jnp.concatenate([jnp.where(low_half, vb, 0.0),
                                   jnp.where(low_half, 0.0, vb)], axis=0).astype(BF16)
            s = _dot(wcat[j], rhs) + sgb_ref[:, l0:l0 + LANES]
            a_ref[r0:r0 + CHUNK, l0:l0 + LANES] = (u[r0:r0 + CHUNK, l0:l0 + LANES] * s).astype(BF16)

    lane_t = lax.broadcasted_iota(jnp.int32, (tm, LANES), 1)
    cos_t = cos_ref[...]
    sin_t = sin_ref[...]
    q_scale = QK_HEAD ** -0.5 * math.log2(math.e)

    cqn = _scale_rows(cq, lax.rsqrt(cq_ss * (1.0 / Q_LORA) + EPS)) * qln_ref[...]
    qf = _dot(cqn.astype(BF16), wuq_ref[...])
    ckvn = _scale_rows(ckv, lax.rsqrt(ckv_ss * (1.0 / KV_LORA) + EPS)) * kvln_ref[...]
    ckvb = ckvn.astype(BF16)
    kf = _dot(ckvb, wuk_ref[...])
    v_ref[...] = _dot(ckvb, wuv_ref[...]).astype(BF16)
    kr = proj[:, o_kr:o_kr + LANES]

    qn_lo = qn_ref[:, :QK_NOPE]
    qn_hi = qn_ref[:, QK_NOPE:]
    kn_lo = kn_ref[:, :QK_NOPE]
    kr_sq = kr * kr
    kr_rot = _rope_rotate(kr * kn_ref[:, QK_NOPE:], cos_t, sin_t, lane_t)
    for hd in range(MLA_HEADS):
        q_lo = qf[:, hd * QK_PAD:hd * QK_PAD + QK_NOPE]
        q_hi = qf[:, hd * QK_PAD + QK_NOPE:(hd + 1) * QK_PAD]
        k_lo = kf[:, hd * QK_NOPE:(hd + 1) * QK_NOPE]
        q_ss, k_ss = _row_sum_pair(q_lo * q_lo + q_hi * q_hi, k_lo * k_lo + kr_sq)
        rq = lax.rsqrt(q_ss * (1.0 / QK_HEAD) + EPS) * q_scale
        rk = lax.rsqrt(k_ss * (1.0 / QK_HEAD) + EPS)
        q_ref[:, hd * QK_PAD:hd * QK_PAD + QK_NOPE] = (q_lo * rq * qn_lo).astype(BF16)
        q_ref[:, hd * QK_PAD + QK_NOPE:(hd + 1) * QK_PAD] = _rope_rotate(
            q_hi * rq * qn_hi, cos_t, sin_t, lane_t).astype(BF16)
        k_ref[:, hd * QK_PAD:hd * QK_PAD + QK_NOPE] = (k_lo * rk * kn_lo).astype(BF16)
        k_ref[:, hd * QK_PAD + QK_NOPE:(hd + 1) * QK_PAD] = (kr_rot * rk).astype(BF16)


def _mixer_pre(x2, mod_l, norm_mix, w_in, sgu_norm, sgu_w, sgu_b, q_lat_norm, kv_lat_norm,
               w_uq, w_ukv, q_norm, k_norm, cos_t, sin_t, seq):
    n, d = x2.shape
    tm = TM_MIX
    per_b = seq // tm
    win_p = jnp.pad(w_in, ((0, 0), (0, D_IN_PAD - w_in.shape[1]))).astype(BF16)
    wuq_p = jnp.pad(w_uq.reshape(Q_LORA, MLA_HEADS, QK_HEAD),
                    ((0, 0), (0, 0), (0, QK_PAD - QK_HEAD))).reshape(Q_LORA, MLA_HEADS * QK_PAD).astype(BF16)
    wukv = w_ukv.reshape(KV_LORA, MLA_HEADS, QK_NOPE + V_HEAD)
    wuk = wukv[:, :, :QK_NOPE].reshape(KV_LORA, MLA_HEADS * QK_NOPE).astype(BF16)
    wuv = wukv[:, :, QK_NOPE:].reshape(KV_LORA, MLA_WIDTH).astype(BF16)
    qn_p = jnp.pad(q_norm, (0, QK_PAD - QK_HEAD)).reshape(1, QK_PAD)
    kn_p = jnp.pad(k_norm, (0, QK_PAD - QK_HEAD)).reshape(1, QK_PAD)
    sgb_full = jnp.repeat(sgu_b.T, SGU_GROUP_DIM, axis=1)

    tok = lambda i: (i, 0)
    in_specs = [
        pl.BlockSpec((tm, d), tok),
        pl.BlockSpec((None, N_MOD, d), lambda i: (i // per_b, 0, 0)),
        _const_spec((1, d)),
        _const_spec(win_p.shape),
        _const_spec((1, SGU_WIDTH)),
        _const_spec(sgu_w.shape),
        _const_spec(sgb_full.shape),
        _const_spec((1, Q_LORA)),
        _const_spec((1, KV_LORA)),
        _const_spec(wuq_p.shape),
        _const_spec(wuk.shape),
        _const_spec(wuv.shape),
        _const_spec((1, QK_PAD)),
        _const_spec((1, QK_PAD)),
        pl.BlockSpec((tm, LANES), tok),
        pl.BlockSpec((tm, LANES), tok),
    ]
    out_shape = [
        jax.ShapeDtypeStruct((n, SGU_WIDTH), BF16),
        jax.ShapeDtypeStruct((n, MLA_HEADS * QK_PAD), BF16),
        jax.ShapeDtypeStruct((n, MLA_HEADS * QK_PAD), BF16),
        jax.ShapeDtypeStruct((n, MLA_WIDTH), BF16),
    ]
    out_specs = [pl.BlockSpec((tm, s.shape[1]), tok) for s in out_shape]
    return pl.pallas_call(
        _mixer_pre_kernel,
        grid=(n // tm,),
        in_specs=in_specs,
        out_specs=out_specs,
        out_shape=out_shape,
        compiler_params=_params(("parallel",), 48),
        name="mixer_pre",
    )(x2, mod_l, norm_mix.reshape(1, d), win_p, sgu_norm.reshape(1, SGU_WIDTH), sgu_w, sgb_full,
      q_lat_norm.reshape(1, Q_LORA), kv_lat_norm.reshape(1, KV_LORA), wuq_p, wuk, wuv, qn_p, kn_p,
      cos_t, sin_t)


def _attn_kernel(q_ref, k_ref, v_ref, o_ref):
    tq = q_ref.shape[0]
    qi = pl.program_id(2)
    neg = jnp.finfo(F32).min

    def update(q, kb, vb, m, l, acc, mask_from):
        s = lax.dot_general(q, kb, (((1,), (1,)), ((), ())), preferred_element_type=F32)
        if mask_from is not None:
            row = lax.broadcasted_iota(jnp.int32, s.shape, 0)
            col = lax.broadcasted_iota(jnp.int32, s.shape, 1)
            s = jnp.where(col <= row + mask_from, s, neg)
        m_new = jnp.maximum(m, jnp.max(s, axis=-1, keepdims=True))
        alpha = jnp.exp2(m - m_new)
        p = jnp.exp2(s - m_new)
        l_new = alpha * l + jnp.sum(p, axis=-1, keepdims=True)
        acc_new = alpha * acc + _dot(p.astype(BF16), vb)
        return m_new, l_new, acc_new

    def full_step(ki, carry):
        k0 = pl.multiple_of(ki * tq, tq)
        return update(q_ref[...], k_ref[pl.ds(k0, tq), :], v_ref[pl.ds(k0, tq), :], *carry, None)

    init = (jnp.full((tq, 1), neg, F32), jnp.zeros((tq, 1), F32), jnp.zeros((tq, V_HEAD), F32))
    m, l, acc = lax.fori_loop(0, qi, full_step, init)
    k0 = pl.multiple_of(qi * tq, tq)
    for r in range(tq // ATTN_BAND):
        rows = slice(r * ATTN_BAND, (r + 1) * ATTN_BAND)
        nk = (r + 1) * ATTN_BAND
        _, lr, ar = update(q_ref[rows, :], k_ref[pl.ds(k0, nk), :], v_ref[pl.ds(k0, nk), :],
                           m[rows], l[rows], acc[rows], r * ATTN_BAND)
        o_ref[rows, :] = (ar / lr).astype(BF16)


def _attention(q, k, v, batch, seq):
    tq = min(TQ, seq)
    q3 = q.reshape(batch, seq, MLA_HEADS * QK_PAD)
    k3 = k.reshape(batch, seq, MLA_HEADS * QK_PAD)
    v3 = v.reshape(batch, seq, MLA_WIDTH)
    out = pl.pallas_call(
        _attn_kernel,
        grid=(batch, MLA_HEADS, seq // tq),
        in_specs=[
            pl.BlockSpec((None, tq, QK_PAD), lambda b, h, i: (b, i, h)),
            pl.BlockSpec((None, seq, QK_PAD), lambda b, h, i: (b, 0, h)),
            pl.BlockSpec((None, seq, V_HEAD), lambda b, h, i: (b, 0, h)),
        ],
        out_specs=pl.BlockSpec((None, tq, V_HEAD), lambda b, h, i: (b, i, h)),
        out_shape=jax.ShapeDtypeStruct((batch, seq, MLA_WIDTH), BF16),
        compiler_params=_params(("parallel", "parallel", "arbitrary"), 56),
        name="causal_attention",
    )(q3, k3, v3)
    return out.reshape(batch * seq, MLA_WIDTH)


def _mixer_post(x, a, o, mod_ref, wout_ref, nffn_ref):
    half = a.shape[1]
    y = _dot(a, wout_ref[:half, :]) + _dot(o, wout_ref[half:, :])
    x1 = x + mod_ref[2:3, :] * y
    gain = nffn_ref[...] * (1.0 + mod_ref[4:5, :])
    h = _rms(x1, x1.shape[1]) * gain + mod_ref[3:4, :]
    return x1, h


def _dense_ffn_kernel(x_ref, a_ref, o_ref, mod_ref, wout_ref, nffn_ref, w1_ref, w3_ref, w2_ref,
                      out_ref):
    x1, h = _mixer_post(x_ref[...], a_ref[...], o_ref[...], mod_ref, wout_ref, nffn_ref)
    hb = h.astype(BF16)
    dff = w1_ref.shape[1]
    acc = jnp.zeros(x1.shape, F32)
    for c0 in range(0, dff, FF_CHUNK):
        h1 = _dot(hb, w1_ref[:, c0:c0 + FF_CHUNK])
        h3 = _dot(hb, w3_ref[:, c0:c0 + FF_CHUNK])
        act = (_silu(h1) * h3).astype(BF16)
        acc = acc + _dot(act, w2_ref[c0:c0 + FF_CHUNK, :])
    out_ref[...] = x1 + mod_ref[5:6, :] * acc


def _dense_layer_tail(x2, a, o, mod_l, w_out, norm_ffn, w1, w3, w2, seq):
    n, d = x2.shape
    tm = TM_FFN
    per_b = seq // tm
    tok = lambda i: (i, 0)
    single = pl.Buffered(1)
    wspec = lambda shape: pl.BlockSpec(shape, lambda i: (0, 0), pipeline_mode=single)
    dff = w1.shape[1]
    return pl.pallas_call(
        _dense_ffn_kernel,
        grid=(n // tm,),
        in_specs=[
            pl.BlockSpec((tm, d), tok),
            pl.BlockSpec((tm, a.shape[1]), tok),
            pl.BlockSpec((tm, o.shape[1]), tok),
            pl.BlockSpec((None, N_MOD, d), lambda i: (i // per_b, 0, 0)),
            wspec((d, d)),
            _const_spec((1, d)),
            wspec((d, dff)),
            wspec((d, dff)),
            wspec((dff, d)),
        ],
        out_specs=pl.BlockSpec((tm, d), tok),
        out_shape=jax.ShapeDtypeStruct((n, d), F32),
        compiler_params=_params(("parallel",), 56),
        name="dense_ffn",
    )(x2, a, o, mod_l, w_out.astype(BF16), norm_ffn.reshape(1, d),
      w1.astype(BF16), w3.astype(BF16), w2.astype(BF16))


def _router_kernel(x_ref, a_ref, o_ref, mod_ref, wout_ref, nffn_ref, rw_ref,
                   x1_ref, h_ref, info_ref, cnt_ref, carry_ref):
    @pl.when(pl.program_id(0) == 0)
    def _():
        carry_ref[...] = jnp.zeros(carry_ref.shape, F32)

    x1, h = _mixer_post(x_ref[...], a_ref[...], o_ref[...], mod_ref, wout_ref, nffn_ref)
    x1_ref[...] = x1
    h_ref[...] = h
    tm = h.shape[0]

    h_hi = h.astype(BF16)
    h_lo = (h - h_hi.astype(F32)).astype(BF16)
    rw = rw_ref[...]
    rw_hi = rw.astype(BF16)
    rw_lo = (rw - rw_hi.astype(F32)).astype(BF16)
    logits = _dot(h_hi, rw_hi) + _dot(h_hi, rw_lo) + _dot(h_lo, rw_hi)
    lt = logits.T[:N_EXPERTS, :]

    eid = lax.broadcasted_iota(jnp.int32, lt.shape, 0)
    m1 = jnp.max(lt, axis=0, keepdims=True)
    i1 = jnp.min(jnp.where(lt == m1, eid, N_EXPERTS), axis=0, keepdims=True)
    rest = jnp.where(eid == i1, -jnp.inf, lt)
    m2 = jnp.max(rest, axis=0, keepdims=True)
    i2 = jnp.min(jnp.where(rest == m2, eid, N_EXPERTS), axis=0, keepdims=True)
    e2 = jnp.exp(m2 - m1)
    g1 = 1.0 / (1.0 + e2)
    g2 = e2 / (1.0 + e2)

    oh1 = (eid == i1).astype(F32)
    oh2 = (eid == i2).astype(F32)
    r_io = lax.broadcasted_iota(jnp.int32, (tm, tm), 0)
    c_io = lax.broadcasted_iota(jnp.int32, (tm, tm), 1)
    before = jnp.where(r_io < c_io, 1.0, 0.0).astype(BF16)
    cnt1 = jnp.sum(oh1, axis=1, keepdims=True)
    cnt2 = jnp.sum(oh2, axis=1, keepdims=True)
    base = carry_ref[:, 0:1]
    rank1_e = _dot(oh1.astype(BF16), before) + base
    rank2_e = _dot(oh2.astype(BF16), before) + base + cnt1
    rank1 = jnp.sum(oh1 * rank1_e, axis=0, keepdims=True)
    rank2 = jnp.sum(oh2 * rank2_e, axis=0, keepdims=True)
    total = base + cnt1 + cnt2
    carry_ref[...] = jnp.broadcast_to(total, carry_ref.shape)
    cnt_ref[...] = jnp.broadcast_to(total, cnt_ref.shape)

    zero = jnp.zeros_like(g1)
    info_ref[0] = jnp.concatenate(
        [i1.astype(F32), i2.astype(F32), g1, g2, rank1, rank2, zero, zero], axis=0)


def _router(x2, a, o, mod_l, w_out, norm_ffn, router_w, seq):
    n, d = x2.shape
    tm = TM_ROUTE
    per_b = seq // tm
    tok = lambda i: (i, 0)
    rw_pad = jnp.pad(router_w, ((0, 0), (0, LANES - N_EXPERTS)))
    return pl.pallas_call(
        _router_kernel,
        grid=(n // tm,),
        in_specs=[
            pl.BlockSpec((tm, d), tok),
            pl.BlockSpec((tm, a.shape[1]), tok),
            pl.BlockSpec((tm, o.shape[1]), tok),
            pl.BlockSpec((None, N_MOD, d), lambda i: (i // per_b, 0, 0)),
            _const_spec((d, d)),
            _const_spec((1, d)),
            _const_spec((d, LANES)),
        ],
        out_specs=[
            pl.BlockSpec((tm, d), tok),
            pl.BlockSpec((tm, d), tok),
            pl.BlockSpec((1, 8, tm), lambda i: (i, 0, 0)),
            pl.BlockSpec((N_EXPERTS, LANES), lambda i: (0, 0)),
        ],
        out_shape=[
            jax.ShapeDtypeStruct((n, d), F32),
            jax.ShapeDtypeStruct((n, d), F32),
            jax.ShapeDtypeStruct((n // tm, 8, tm), F32),
            jax.ShapeDtypeStruct((N_EXPERTS, LANES), F32),
        ],
        scratch_shapes=[pltpu.VMEM((N_EXPERTS, LANES), F32)],
        compiler_params=_params(("arbitrary",), 48),
        name="mixer_post_router",
    )(x2, a, o, mod_l, w_out.astype(BF16), norm_ffn.reshape(1, d), rw_pad)


def _dispatch_kernel(lo_ref, hi_ref, p1_ref, p2_ref, h_ref, w1_ref, w3_ref, w2_ref,
                     xs_hbm, w1b_ref, w3b_ref, w2b_ref, zbuf, sem, zsem):
    tm = p1_ref.shape[2]
    w1b_ref[...] = w1_ref[...].astype(BF16)
    w3b_ref[...] = w3_ref[...].astype(BF16)
    w2b_ref[...] = w2_ref[...].astype(BF16)

    @pl.when(pl.program_id(0) == 0)
    def _():
        zbuf[...] = jnp.zeros(zbuf.shape, F32)

        def fill(r, _):
            pltpu.make_async_copy(zbuf.at[pl.ds(0, 1), :], xs_hbm.at[pl.ds(r, 1), :], zsem).start()
            return 0

        for g in range(lo_ref.shape[0]):
            lax.fori_loop(lo_ref[g], hi_ref[g], fill, 0)
        slack = N_EXPERTS * TG
        pltpu.make_async_copy(xs_hbm.at[pl.ds(0, slack), :], xs_hbm.at[pl.ds(0, slack), :], zsem).wait()

    def issue(t, _):
        src = h_ref.at[pl.ds(t, 1), :]
        pltpu.make_async_copy(src, xs_hbm.at[pl.ds(p1_ref[0, 0, t], 1), :], sem).start(priority=0)
        pltpu.make_async_copy(src, xs_hbm.at[pl.ds(p2_ref[0, 0, t], 1), :], sem).start(priority=1)
        return 0

    lax.fori_loop(0, tm, issue, 0, unroll=8)
    for _ in range(TOP_K):
        pltpu.make_async_copy(h_ref, xs_hbm.at[pl.ds(0, tm), :], sem).wait()


def _dispatch(h, pos1, pos2, pad_lo, pad_hi, rows, w1, w3, w2):
    n, d = h.shape
    tm = TM_DISPATCH
    steps = n // tm
    idx_spec = pl.BlockSpec((1, 1, tm), lambda i, lo, hi: (i, 0, 0), memory_space=pltpu.SMEM)
    flat = [w.reshape(-1, w.shape[-1]) for w in (w1, w3, w2)]
    slab = lambda w: pl.BlockSpec((w.shape[0] // steps, w.shape[1]), lambda i, lo, hi: (i, 0))
    grid_spec = pltpu.PrefetchScalarGridSpec(
        num_scalar_prefetch=2,
        grid=(steps,),
        in_specs=[idx_spec, idx_spec, pl.BlockSpec((tm, d), lambda i, lo, hi: (i, 0))]
        + [slab(w) for w in flat],
        out_specs=[pl.BlockSpec(memory_space=pl.ANY)] + [slab(w) for w in flat],
        scratch_shapes=[pltpu.VMEM((8, d), F32), pltpu.SemaphoreType.DMA(()),
                        pltpu.SemaphoreType.DMA(())],
    )
    xs, w1b, w3b, w2b = pl.pallas_call(
        _dispatch_kernel,
        grid_spec=grid_spec,
        out_shape=[jax.ShapeDtypeStruct((rows, d), F32)]
        + [jax.ShapeDtypeStruct(w.shape, BF16) for w in flat],
        compiler_params=pltpu.CompilerParams(dimension_semantics=("arbitrary",),
                                             vmem_limit_bytes=48 * MIB, has_side_effects=True),
        name="moe_dispatch",
    )(pad_lo, pad_hi, pos1.reshape(steps, 1, tm), pos2.reshape(steps, 1, tm), h, *flat)
    return xs, w1b.reshape(w1.shape), w3b.reshape(w3.shape), w2b.reshape(w2.shape)


def _expert_kernel(te_ref, tv_ref, xs_ref, w1_ref, w3_ref, w2_ref, y_ref):
    i = pl.program_id(0)
    valid = tv_ref[i]

    @pl.when(valid > 0)
    def _():
        xb = xs_ref[...].astype(BF16)
        dff = w1_ref.shape[1]
        acc = jnp.zeros(xs_ref.shape, F32)
        for c0 in range(0, dff, FF_CHUNK):
            h1 = _dot(xb, w1_ref[:, c0:c0 + FF_CHUNK])
            h3 = _dot(xb, w3_ref[:, c0:c0 + FF_CHUNK])
            act = (_silu(h1) * h3).astype(BF16)
            acc = acc + _dot(act, w2_ref[c0:c0 + FF_CHUNK, :])
        y_ref[...] = acc

    @pl.when(valid <= 0)
    def _():
        y_ref[...] = jnp.zeros(y_ref.shape, F32)


def _experts(xs, tile_expert, tile_valid, w1, w3, w2):
    rows, d = xs.shape
    tg = TG
    dff = w1.shape[2]
    grid_spec = pltpu.PrefetchScalarGridSpec(
        num_scalar_prefetch=2,
        grid=(rows // tg,),
        in_specs=[
            pl.BlockSpec((tg, d), lambda i, te, tv: (i, 0)),
            pl.BlockSpec((None, d, dff), lambda i, te, tv: (te[i], 0, 0)),
            pl.BlockSpec((None, d, dff), lambda i, te, tv: (te[i], 0, 0)),
            pl.BlockSpec((None, dff, d), lambda i, te, tv: (te[i], 0, 0)),
        ],
        out_specs=pl.BlockSpec((tg, d), lambda i, te, tv: (i, 0)),
    )
    return pl.pallas_call(
        _expert_kernel,
        grid_spec=grid_spec,
        out_shape=jax.ShapeDtypeStruct((rows, d), F32),
        compiler_params=_params(("arbitrary",), 56),
        name="moe_experts",
    )(tile_expert, tile_valid, xs, w1, w3, w2)


def _combine_kernel(p1_cur, p2_cur, p1_nxt, p2_nxt, x1_ref, info_ref, mod_ref, y_hbm, out_ref,
                    buf1, buf2, sems):
    i = pl.program_id(0)
    tm = x1_ref.shape[0]
    slot = lax.rem(i, 2)

    def issue(p1_ref, p2_ref, s):
        def body(t, _):
            pltpu.make_async_copy(y_hbm.at[pl.ds(p1_ref[0, 0, t], 1), :],
                                  buf1.at[s, pl.ds(t, 1), :], sems.at[s]).start(priority=0)
            pltpu.make_async_copy(y_hbm.at[pl.ds(p2_ref[0, 0, t], 1), :],
                                  buf2.at[s, pl.ds(t, 1), :], sems.at[s]).start(priority=1)
            return 0

        lax.fori_loop(0, tm, body, 0, unroll=8)

    @pl.when(i == 0)
    def _():
        issue(p1_cur, p2_cur, 0)

    @pl.when(i + 1 < pl.num_programs(0))
    def _():
        issue(p1_nxt, p2_nxt, 1 - slot)

    pltpu.make_async_copy(y_hbm.at[pl.ds(0, tm), :], buf1.at[slot], sems.at[slot]).wait()
    pltpu.make_async_copy(y_hbm.at[pl.ds(0, tm), :], buf2.at[slot], sems.at[slot]).wait()
    info = info_ref[0]
    pad = jnp.zeros((LANES - info.shape[0], tm), F32)
    cols = jnp.concatenate([info, pad], axis=0).T
    y = cols[:, 2:3] * buf1[slot] + cols[:, 3:4] * buf2[slot]
    out_ref[...] = x1_ref[...] + mod_ref[5:6, :] * y


def _combine(x1, y, pos1, pos2, info, mod_l, seq):
    n, d = x1.shape
    tm = TM_COMB
    per_b = seq // tm
    nb = n // tm
    tok = lambda i: (i, 0)
    cur = pl.BlockSpec((1, 1, tm), lambda i: (i, 0, 0), memory_space=pltpu.SMEM)
    nxt = pl.BlockSpec((1, 1, tm), lambda i: (jnp.minimum(i + 1, nb - 1), 0, 0), memory_space=pltpu.SMEM)
    per_info = info.shape[2] // tm
    p1 = pos1.reshape(nb, 1, tm)
    p2 = pos2.reshape(nb, 1, tm)
    return pl.pallas_call(
        _combine_kernel,
        grid=(nb,),
        in_specs=[
            cur, cur, nxt, nxt,
            pl.BlockSpec((tm, d), tok),
            pl.BlockSpec((1, info.shape[1], tm), lambda i: (i // per_info, 0, i % per_info)),
            pl.BlockSpec((None, N_MOD, d), lambda i: (i // per_b, 0, 0)),
            pl.BlockSpec(memory_space=pl.ANY),
        ],
        out_specs=pl.BlockSpec((tm, d), tok),
        out_shape=jax.ShapeDtypeStruct((n, d), F32),
        scratch_shapes=[pltpu.VMEM((2, tm, d), F32), pltpu.VMEM((2, tm, d), F32),
                        pltpu.SemaphoreType.DMA((2,))],
        compiler_params=_params(("arbitrary",), 40),
        name="moe_combine",
    )(p1, p2, p1, p2, x1, info, mod_l, y)


def _moe_layer_tail(x2, a, o, mod_l, w_out, norm_ffn, router_w, w1, w3, w2, seq):
    n, d = x2.shape
    x1, h, info, counts = _router(x2, a, o, mod_l, w_out, norm_ffn, router_w, seq)

    field = lambda r: info[:, r, :].reshape(n).astype(jnp.int32)
    cnt = counts[:, 0].astype(jnp.int32)
    padded = ((cnt + TG - 1) // TG) * TG
    ends = jnp.cumsum(padded)
    starts = ends - padded
    pos1 = starts[field(0)] + field(4)
    pos2 = starts[field(1)] + field(5)
    rows = TOP_K * n + N_EXPERTS * TG
    tile_start = jnp.arange(rows // TG, dtype=jnp.int32) * TG
    tile_expert = jnp.minimum(jnp.sum((ends[None, :] <= tile_start[:, None]).astype(jnp.int32), axis=1),
                              N_EXPERTS - 1)
    tile_valid = jnp.clip(starts[tile_expert] + cnt[tile_expert] - tile_start, 0, TG).astype(jnp.int32)

    pad_lo = jnp.concatenate([starts + cnt, ends[-1:]]).astype(jnp.int32)
    pad_hi = jnp.concatenate([ends, jnp.full((1,), rows, ends.dtype)]).astype(jnp.int32)

    xs, w1b, w3b, w2b = _dispatch(h, pos1, pos2, pad_lo, pad_hi, rows, w1, w3, w2)
    y = _experts(xs, tile_expert, tile_valid, w1b, w3b, w2b)
    return _combine(x1, y, pos1, pos2, info, mod_l, seq)


def kernel(x, c, positions, ada_w, ada_b, norm_mix, norm_ffn, w_in, sgu_norm, sgu_w, sgu_b,
           q_lat_norm, kv_lat_norm, w_uq, w_ukv, q_norm, k_norm, w_out,
           ffn_w1, ffn_w3, ffn_w2, router_w, moe_w1, moe_w3, moe_w2):
    batch, seq, d = x.shape
    depth = ada_w.shape[0]
    mod = _modulation(c, ada_w, ada_b)
    cos_t, sin_t = _rope_tables(positions)
    x2 = x.reshape(batch * seq, d)
    for layer in range(depth):
        a, q, k, v = _mixer_pre(x2, mod[layer], norm_mix[layer], w_in[layer], sgu_norm[layer],
                                sgu_w[layer], sgu_b[layer], q_lat_norm[layer], kv_lat_norm[layer],
                                w_uq[layer], w_ukv[layer], q_norm[layer], k_norm[layer],
                                cos_t, sin_t, seq)
        o = _attention(q, k, v, batch, seq)
        i = layer // 2
        if layer % 2 == 0:
            x2 = _dense_layer_tail(x2, a, o, mod[layer], w_out[layer], norm_ffn[layer],
                                   ffn_w1[i], ffn_w3[i], ffn_w2[i], seq)
        else:
            x2 = _moe_layer_tail(x2, a, o, mod[layer], w_out[layer], norm_ffn[layer], router_w[i],
                                 moe_w1[i], moe_w3[i], moe_w2[i], seq)
    return x2.reshape(batch, seq, d)
```

```python
import functools
import math

import jax
import jax.numpy as jnp
from jax import lax
from jax.experimental import pallas as pl
from jax.experimental.pallas import tpu as pltpu

F32 = jnp.float32
BF16 = jnp.bfloat16

EPS = 1e-6
ROPE_BASE = 10000.0
SGU_GROUPS = 8
SGU_GROUP_DIM = 64
SGU_WIDTH = SGU_GROUPS * SGU_GROUP_DIM
CHUNK = 128
MLA_HEADS = 4
QK_NOPE = 128
QK_ROPE = 64
QK_HEAD = QK_NOPE + QK_ROPE
QK_PAD = 256
V_HEAD = 128
MLA_WIDTH = MLA_HEADS * V_HEAD
Q_LORA = 256
KV_LORA = 128
N_MOD = 6
N_EXPERTS = 8
TOP_K = 2
LANES = 128
D_IN_PAD = 2 * SGU_WIDTH + Q_LORA + KV_LORA + LANES

MIB = 1024 * 1024

TM_MIX = 1024
MIX_SUB = 512
TQ = 4096
ATTN_BAND = 512
TM_FFN = 512
TM_ROUTE = 512
TM_DISPATCH = 1024
TG = 256
TM_COMB = 512
FF_CHUNK = 2816


def _dot(a, b):
    return jnp.dot(a, b, preferred_element_type=F32)


def _gelu_tanh(x):
    k = -2.0 * math.sqrt(2.0 / math.pi) * math.log2(math.e)
    return x * (1.0 / (1.0 + jnp.exp2(x * (k + (k * 0.044715) * (x * x)))))


def _silu(x):
    return x * (1.0 / (1.0 + jnp.exp(-x)))


def _row_sum(v):
    return _dot(_fold_lane_tiles(v).astype(BF16), jnp.ones((LANES, LANES), BF16))


def _fold_lane_tiles(v):
    part = v[:, :LANES]
    for j in range(LANES, v.shape[1], LANES):
        part = part + v[:, j:j + LANES]
    return part


def _row_sum_pair(va, vb):
    parts = jnp.concatenate([_fold_lane_tiles(va), _fold_lane_tiles(vb)], axis=1).astype(BF16)
    r = lax.broadcasted_iota(jnp.int32, (2 * LANES, 2 * LANES), 0) // LANES
    c = lax.broadcasted_iota(jnp.int32, (2 * LANES, 2 * LANES), 1) // LANES
    both = _dot(parts, jnp.where(r == c, 1.0, 0.0).astype(BF16))
    return both[:, :LANES], both[:, LANES:]


def _lanes(v, width):
    return jnp.concatenate([v] * (width // LANES), axis=1)


def _scale_rows(x, s):
    return jnp.concatenate([x[:, j:j + LANES] * s for j in range(0, x.shape[1], LANES)], axis=1)


def _rms(x, width):
    rs = lax.rsqrt(_row_sum(x * x) * (1.0 / width) + EPS)
    return _scale_rows(x, rs)


def _params(sem, vmem_mib):
    return pltpu.CompilerParams(dimension_semantics=sem, vmem_limit_bytes=vmem_mib * MIB)


def _const_spec(shape):
    nd = len(shape)
    return pl.BlockSpec(shape, lambda *_: (0,) * nd)


def _mod_kernel(c_ref, w_ref, b_ref, o_ref):
    c = c_ref[...]
    ca = _silu(c)
    ca_hi = ca.astype(BF16)
    ca_lo = (ca - ca_hi.astype(F32)).astype(BF16)
    w = w_ref[...]
    w_hi = w.astype(BF16)
    w_lo = (w - w_hi.astype(F32)).astype(BF16)
    acc = _dot(ca_hi, w_hi) + _dot(ca_hi, w_lo) + _dot(ca_lo, w_hi)
    o_ref[...] = acc + b_ref[...]


def _modulation(c, ada_w, ada_b):
    L, D, W = ada_w.shape
    B = c.shape[0]
    rows = 16
    c_pad = jnp.zeros((rows, D), F32).at[:B].set(c)
    tn = 1536
    out = pl.pallas_call(
        _mod_kernel,
        grid=(L, W // tn),
        in_specs=[
            pl.BlockSpec((rows, D), lambda l, j: (0, 0)),
            pl.BlockSpec((None, D, tn), lambda l, j: (l, 0, j)),
            pl.BlockSpec((None, 1, tn), lambda l, j: (l, 0, j)),
        ],
        out_specs=pl.BlockSpec((None, rows, tn), lambda l, j: (l, 0, j)),
        out_shape=jax.ShapeDtypeStruct((L, rows, W), F32),
        compiler_params=_params(("parallel", "parallel"), 40),
        name="adaln_mod",
    )(c_pad, ada_w, ada_b.reshape(L, 1, W))
    return out[:, :B].reshape(L, B, N_MOD, D)


def _rope_kernel(pos_ref, inv_ref, cos_ref, sin_ref):
    pos = pos_ref[0].astype(F32)
    ang = inv_ref[...] * pos
    co = jnp.cos(ang)
    si = jnp.sin(ang)
    z = jnp.zeros((2 * co.shape[0], co.shape[1]), F32)
    cos_ref[...] = jnp.concatenate([co, co, z], axis=0).T
    sin_ref[...] = jnp.concatenate([-si, si, z], axis=0).T


def _rope_tables(positions):
    n = positions.size
    tn = min(2048, n)
    half = QK_ROPE // 2
    inv_freq = 1.0 / (ROPE_BASE ** (jnp.arange(0, QK_ROPE, 2, dtype=F32) / QK_ROPE))
    pos3 = positions.reshape(n // tn, 1, tn)
    return pl.pallas_call(
        _rope_kernel,
        grid=(n // tn,),
        in_specs=[
            pl.BlockSpec((1, 1, tn), lambda i: (i, 0, 0)),
            pl.BlockSpec((half, 1), lambda i: (0, 0)),
        ],
        out_specs=[pl.BlockSpec((tn, LANES), lambda i: (i, 0))] * 2,
        out_shape=[jax.ShapeDtypeStruct((n, LANES), F32)] * 2,
        compiler_params=_params(("parallel",), 32),
        name="rope_tables",
    )(pos3, inv_freq.reshape(half, 1))


def _rope_rotate(r, cos_t, sin_t, lane):
    partner = jnp.where(lane < QK_ROPE // 2,
                        pltpu.roll(r, LANES - QK_ROPE // 2, 1),
                        pltpu.roll(r, QK_ROPE // 2, 1))
    return r * cos_t + partner * sin_t


def _mixer_pre_kernel(x_ref, mod_ref, nmix_ref, win_ref, sgn_ref, sgw_ref, sgb_ref,
                      qln_ref, kvln_ref, wuq_ref, wuk_ref, wuv_ref, qn_ref, kn_ref,
                      cos_ref, sin_ref, a_ref, q_ref, k_ref, v_ref):
    for r0 in range(0, x_ref.shape[0], MIX_SUB):
        rows = pl.ds(r0, MIX_SUB)
        _mixer_pre_rows(x_ref.at[rows], mod_ref, nmix_ref, win_ref, sgn_ref, sgw_ref, sgb_ref,
                        qln_ref, kvln_ref, wuq_ref, wuk_ref, wuv_ref, qn_ref, kn_ref,
                        cos_ref.at[rows], sin_ref.at[rows],
                        a_ref.at[rows], q_ref.at[rows], k_ref.at[rows], v_ref.at[rows])


def _mixer_pre_rows(x_ref, mod_ref, nmix_ref, win_ref, sgn_ref, sgw_ref, sgb_ref,
                    qln_ref, kvln_ref, wuq_ref, wuk_ref, wuv_ref, qn_ref, kn_ref,
                    cos_ref, sin_ref, a_ref, q_ref, k_ref, v_ref):
    tm, d = x_ref.shape
    gain = nmix_ref[...] * (1.0 + mod_ref[1:2, :])
    h = _rms(x_ref[...], d) * gain + mod_ref[0:1, :]
    proj = _dot(h.astype(BF16), win_ref[...])

    o_zv = SGU_WIDTH
    o_cq = 2 * SGU_WIDTH
    o_ckv = o_cq + Q_LORA
    o_kr = o_ckv + KV_LORA

    u = _gelu_tanh(proj[:, :SGU_WIDTH])
    gv = _gelu_tanh(proj[:, o_zv:o_cq])
    cq = proj[:, o_cq:o_ckv]
    ckv = proj[:, o_ckv:o_kr]
    gv_sum, cq_ss = _row_sum_pair(gv, cq * cq)
    cen = gv - _lanes(gv_sum * (1.0 / SGU_WIDTH), SGU_WIDTH)
    cen_ss, ckv_ss = _row_sum_pair(cen * cen, ckv * ckv)
    vn = cen * _lanes(lax.rsqrt(cen_ss * (1.0 / SGU_WIDTH) + EPS), SGU_WIDTH) * sgn_ref[...]

    row = lax.broadcasted_iota(jnp.int32, (CHUNK, CHUNK), 0)
    col = lax.broadcasted_iota(jnp.int32, (CHUNK, CHUNK), 1)
    causal = col <= row
    lane = lax.broadcasted_iota(jnp.int32, (CHUNK, LANES), 1)
    low_half = lane < SGU_GROUP_DIM
    n_pairs = SGU_GROUPS // 2
    wcat = []
    for j in range(n_pairs):
        wa = jnp.where(causal, sgw_ref[2 * j], 0.0)
        wb = jnp.where(causal, sgw_ref[2 * j + 1], 0.0)
        wcat.append(jnp.concatenate([wa, wb], axis=1).astype(BF16))
    for c in range(tm // CHUNK):
        r0 = c * CHUNK
        for j in range(n_pairs):
            l0 = j * LANES
            vb = vn[r0:r0 + CHUNK, l0:l0 + LANES]
            rhs = jnp.concatenate([jnp.where(low_half, vb, 0.0),
                                   jnp.where(low_half, 0.0, vb)], axis=0).astype(BF16)
            s = _dot(wcat[j], rhs) + sgb_ref[:, l0:l0 + LANES]
            a_ref[r0:r0 + CHUNK, l0:l0 + LANES] = (u[r0:r0 + CHUNK, l0:l0 + LANES] * s).astype(BF16)

    lane_t = lax.broadcasted_iota(jnp.int32, (tm, LANES), 1)
    cos_t = cos_ref[...]
    sin_t = sin_ref[...]
    q_scale = QK_HEAD ** -0.5 * math.log2(math.e)

    cqn = _scale_rows(cq, lax.rsqrt(cq_ss * (1.0 / Q_LORA) + EPS)) * qln_ref[...]
    qf = _dot(cqn.astype(BF16), wuq_ref[...])
    ckvn = _scale_rows(ckv, lax.rsqrt(ckv_ss * (1.0 / KV_LORA) + EPS)) * kvln_ref[...]
    ckvb = ckvn.astype(BF16)
    kf = _dot(ckvb, wuk_ref[...])
    v_ref[...] = _dot(ckvb, wuv_ref[...]).astype(BF16)
    kr = proj[:, o_kr:o_kr + LANES]

    qn_lo = qn_ref[:, :QK_NOPE]
    qn_hi = qn_ref[:, QK_NOPE:]
    kn_lo = kn_ref[:, :QK_NOPE]
    kr_sq = kr * kr
    kr_rot = _rope_rotate(kr * kn_ref[:, QK_NOPE:], cos_t, sin_t, lane_t)
    for hd in range(MLA_HEADS):
        q_lo = qf[:, hd * QK_PAD:hd * QK_PAD + QK_NOPE]
        q_hi = qf[:, hd * QK_PAD + QK_NOPE:(hd + 1) * QK_PAD]
        k_lo = kf[:, hd * QK_NOPE:(hd + 1) * QK_NOPE]
        q_ss, k_ss = _row_sum_pair(q_lo * q_lo + q_hi * q_hi, k_lo * k_lo + kr_sq)
        rq = lax.rsqrt(q_ss * (1.0 / QK_HEAD) + EPS) * q_scale
        rk = lax.rsqrt(k_ss * (1.0 / QK_HEAD) + EPS)
        q_ref[:, hd * QK_PAD:hd * QK_PAD + QK_NOPE] = (q_lo * rq * qn_lo).astype(BF16)
        q_ref[:, hd * QK_PAD + QK_NOPE:(hd + 1) * QK_PAD] = _rope_rotate(
            q_hi * rq * qn_hi, cos_t, sin_t, lane_t).astype(BF16)
        k_ref[:, hd * QK_PAD:hd * QK_PAD + QK_NOPE] = (k_lo * rk * kn_lo).astype(BF16)
        k_ref[:, hd * QK_PAD + QK_NOPE:(hd + 1) * QK_PAD] = (kr_rot * rk).astype(BF16)


def _mixer_pre(x2, mod_l, norm_mix, w_in, sgu_norm, sgu_w, sgu_b, q_lat_norm, kv_lat_norm,
               w_uq, w_ukv, q_norm, k_norm, cos_t, sin_t, seq):
    n, d = x2.shape
    tm = TM_MIX
    per_b = seq // tm
    win_p = jnp.pad(w_in, ((0, 0), (0, D_IN_PAD - w_in.shape[1]))).astype(BF16)
    wuq_p = jnp.pad(w_uq.reshape(Q_LORA, MLA_HEADS, QK_HEAD),
                    ((0, 0), (0, 0), (0, QK_PAD - QK_HEAD))).reshape(Q_LORA, MLA_HEADS * QK_PAD).astype(BF16)
    wukv = w_ukv.reshape(KV_LORA, MLA_HEADS, QK_NOPE + V_HEAD)
    wuk = wukv[:, :, :QK_NOPE].reshape(KV_LORA, MLA_HEADS * QK_NOPE).astype(BF16)
    wuv = wukv[:, :, QK_NOPE:].reshape(KV_LORA, MLA_WIDTH).astype(BF16)
    qn_p = jnp.pad(q_norm, (0, QK_PAD - QK_HEAD)).reshape(1, QK_PAD)
    kn_p = jnp.pad(k_norm, (0, QK_PAD - QK_HEAD)).reshape(1, QK_PAD)
    sgb_full = jnp.repeat(sgu_b.T, SGU_GROUP_DIM, axis=1)

    tok = lambda i: (i, 0)
    in_specs = [
        pl.BlockSpec((tm, d), tok),
        pl.BlockSpec((None, N_MOD, d), lambda i: (i // per_b, 0, 0)),
        _const_spec((1, d)),
        _const_spec(win_p.shape),
        _const_spec((1, SGU_WIDTH)),
        _const_spec(sgu_w.shape),
        _const_spec(sgb_full.shape),
        _const_spec((1, Q_LORA)),
        _const_spec((1, KV_LORA)),
        _const_spec(wuq_p.shape),
        _const_spec(wuk.shape),
        _const_spec(wuv.shape),
        _const_spec((1, QK_PAD)),
        _const_spec((1, QK_PAD)),
        pl.BlockSpec((tm, LANES), tok),
        pl.BlockSpec((tm, LANES), tok),
    ]
    out_shape = [
        jax.ShapeDtypeStruct((n, SGU_WIDTH), BF16),
        jax.ShapeDtypeStruct((n, MLA_HEADS * QK_PAD), BF16),
        jax.ShapeDtypeStruct((n, MLA_HEADS * QK_PAD), BF16),
        jax.ShapeDtypeStruct((n, MLA_WIDTH), BF16),
    ]
    out_specs = [pl.BlockSpec((tm, s.shape[1]), tok) for s in out_shape]
    return pl.pallas_call(
        _mixer_pre_kernel,
        grid=(n // tm,),
        in_specs=in_specs,
        out_specs=out_specs,
        out_shape=out_shape,
        compiler_params=_params(("parallel",), 48),
        name="mixer_pre",
    )(x2, mod_l, norm_mix.reshape(1, d), win_p, sgu_norm.reshape(1, SGU_WIDTH), sgu_w, sgb_full,
      q_lat_norm.reshape(1, Q_LORA), kv_lat_norm.reshape(1, KV_LORA), wuq_p, wuk, wuv, qn_p, kn_p,
      cos_t, sin_t)


def _attn_kernel(q_ref, k_ref, v_ref, o_ref):
    tq = q_ref.shape[0]
    qi = pl.program_id(2)
    neg = jnp.finfo(F32).min

    def update(q, kb, vb, m, l, acc, mask_from):
        s = lax.dot_general(q, kb, (((1,), (1,)), ((), ())), preferred_element_type=F32)
        if mask_from is not None:
            row = lax.broadcasted_iota(jnp.int32, s.shape, 0)
            col = lax.broadcasted_iota(jnp.int32, s.shape, 1)
            s = jnp.where(col <= row + mask_from, s, neg)
        m_new = jnp.maximum(m, jnp.max(s, axis=-1, keepdims=True))
        alpha = jnp.exp2(m - m_new)
        p = jnp.exp2(s - m_new)
        l_new = alpha * l + jnp.sum(p, axis=-1, keepdims=True)
        acc_new = alpha * acc + _dot(p.astype(BF16), vb)
        return m_new, l_new, acc_new

    def full_step(ki, carry):
        k0 = pl.multiple_of(ki * tq, tq)
        return update(q_ref[...], k_ref[pl.ds(k0, tq), :], v_ref[pl.ds(k0, tq), :], *carry, None)

    init = (jnp.full((tq, 1), neg, F32), jnp.zeros((tq, 1), F32), jnp.zeros((tq, V_HEAD), F32))
    m, l, acc = lax.fori_loop(0, qi, full_step, init)
    k0 = pl.multiple_of(qi * tq, tq)
    for r in range(tq // ATTN_BAND):
        rows = slice(r * ATTN_BAND, (r + 1) * ATTN_BAND)
        nk = (r + 1) * ATTN_BAND
        _, lr, ar = update(q_ref[rows, :], k_ref[pl.ds(k0, nk), :], v_ref[pl.ds(k0, nk), :],
                           m[rows], l[rows], acc[rows], r * ATTN_BAND)
        o_ref[rows, :] = (ar / lr).astype(BF16)


def _attention(q, k, v, batch, seq):
    tq = min(TQ, seq)
    q3 = q.reshape(batch, seq, MLA_HEADS * QK_PAD)
    k3 = k.reshape(batch, seq, MLA_HEADS * QK_PAD)
    v3 = v.reshape(batch, seq, MLA_WIDTH)
    out = pl.pallas_call(
        _attn_kernel,
        grid=(batch, MLA_HEADS, seq // tq),
        in_specs=[
            pl.BlockSpec((None, tq, QK_PAD), lambda b, h, i: (b, i, h)),
            pl.BlockSpec((None, seq, QK_PAD), lambda b, h, i: (b, 0, h)),
            pl.BlockSpec((None, seq, V_HEAD), lambda b, h, i: (b, 0, h)),
        ],
        out_specs=pl.BlockSpec((None, tq, V_HEAD), lambda b, h, i: (b, i, h)),
        out_shape=jax.ShapeDtypeStruct((batch, seq, MLA_WIDTH), BF16),
        compiler_params=_params(("parallel", "parallel", "arbitrary"), 56),
        name="causal_attention",
    )(q3, k3, v3)
    return out.reshape(batch * seq, MLA_WIDTH)


def _mixer_post(x, a, o, mod_ref, wout_ref, nffn_ref):
    half = a.shape[1]
    y = _dot(a, wout_ref[:half, :]) + _dot(o, wout_ref[half:, :])
    x1 = x + mod_ref[2:3, :] * y
    gain = nffn_ref[...] * (1.0 + mod_ref[4:5, :])
    h = _rms(x1, x1.shape[1]) * gain + mod_ref[3:4, :]
    return x1, h


def _dense_ffn_kernel(x_ref, a_ref, o_ref, mod_ref, wout_ref, nffn_ref, w1_ref, w3_ref, w2_ref,
                      out_ref):
    x1, h = _mixer_post(x_ref[...], a_ref[...], o_ref[...], mod_ref, wout_ref, nffn_ref)
    hb = h.astype(BF16)
    dff = w1_ref.shape[1]
    acc = jnp.zeros(x1.shape, F32)
    for c0 in range(0, dff, FF_CHUNK):
        h1 = _dot(hb, w1_ref[:, c0:c0 + FF_CHUNK])
        h3 = _dot(hb, w3_ref[:, c0:c0 + FF_CHUNK])
        act = (_silu(h1) * h3).astype(BF16)
        acc = acc + _dot(act, w2_ref[c0:c0 + FF_CHUNK, :])
    out_ref[...] = x1 + mod_ref[5:6, :] * acc


def _dense_layer_tail(x2, a, o, mod_l, w_out, norm_ffn, w1, w3, w2, seq):
    n, d = x2.shape
    tm = TM_FFN
    per_b = seq // tm
    tok = lambda i: (i, 0)
    single = pl.Buffered(1)
    wspec = lambda shape: pl.BlockSpec(shape, lambda i: (0, 0), pipeline_mode=single)
    dff = w1.shape[1]
    return pl.pallas_call(
        _dense_ffn_kernel,
        grid=(n // tm,),
        in_specs=[
            pl.BlockSpec((tm, d), tok),
            pl.BlockSpec((tm, a.shape[1]), tok),
            pl.BlockSpec((tm, o.shape[1]), tok),
            pl.BlockSpec((None, N_MOD, d), lambda i: (i // per_b, 0, 0)),
            wspec((d, d)),
            _const_spec((1, d)),
            wspec((d, dff)),
            wspec((d, dff)),
            wspec((dff, d)),
        ],
        out_specs=pl.BlockSpec((tm, d), tok),
        out_shape=jax.ShapeDtypeStruct((n, d), F32),
        compiler_params=_params(("parallel",), 56),
        name="dense_ffn",
    )(x2, a, o, mod_l, w_out.astype(BF16), norm_ffn.reshape(1, d),
      w1.astype(BF16), w3.astype(BF16), w2.astype(BF16))


def _router_kernel(x_ref, a_ref, o_ref, mod_ref, wout_ref, nffn_ref, rw_ref,
                   x1_ref, h_ref, info_ref, cnt_ref, carry_ref):
    @pl.when(pl.program_id(0) == 0)
    def _():
        carry_ref[...] = jnp.zeros(carry_ref.shape, F32)

    x1, h = _mixer_post(x_ref[...], a_ref[...], o_ref[...], mod_ref, wout_ref, nffn_ref)
    x1_ref[...] = x1
    h_ref[...] = h
    tm = h.shape[0]

    h_hi = h.astype(BF16)
    h_lo = (h - h_hi.astype(F32)).astype(BF16)
    both = _dot(h_hi, rw_ref[...])
    logits = both[:, :LANES] + both[:, LANES:] + _dot(h_lo, rw_ref[:, :LANES])
    lt = logits.T[:N_EXPERTS, :]

    eid = lax.broadcasted_iota(jnp.int32, lt.shape, 0)
    m1 = jnp.max(lt, axis=0, keepdims=True)
    i1 = jnp.min(jnp.where(lt == m1, eid, N_EXPERTS), axis=0, keepdims=True)
    rest = jnp.where(eid == i1, -jnp.inf, lt)
    m2 = jnp.max(rest, axis=0, keepdims=True)
    i2 = jnp.min(jnp.where(rest == m2, eid, N_EXPERTS), axis=0, keepdims=True)
    e2 = jnp.exp(m2 - m1)
    g1 = 1.0 / (1.0 + e2)
    g2 = e2 / (1.0 + e2)

    oh1 = (eid == i1).astype(F32)
    oh2 = (eid == i2).astype(F32)
    r_io = lax.broadcasted_iota(jnp.int32, (tm, tm), 0)
    c_io = lax.broadcasted_iota(jnp.int32, (tm, tm), 1)
    before = jnp.where(r_io < c_io, 1.0, 0.0).astype(BF16)
    cnt1 = jnp.sum(oh1, axis=1, keepdims=True)
    cnt2 = jnp.sum(oh2, axis=1, keepdims=True)
    base = carry_ref[:, 0:1]
    rank1_e = _dot(oh1.astype(BF16), before) + base
    rank2_e = _dot(oh2.astype(BF16), before) + base + cnt1
    rank1 = jnp.sum(oh1 * rank1_e, axis=0, keepdims=True)
    rank2 = jnp.sum(oh2 * rank2_e, axis=0, keepdims=True)
    total = base + cnt1 + cnt2
    carry_ref[...] = jnp.broadcast_to(total, carry_ref.shape)
    cnt_ref[...] = jnp.broadcast_to(total, cnt_ref.shape)

    zero = jnp.zeros_like(g1)
    info_ref[0] = jnp.concatenate(
        [i1.astype(F32), i2.astype(F32), g1, g2, rank1, rank2, zero, zero], axis=0)


def _router(x2, a, o, mod_l, w_out, norm_ffn, router_w, seq):
    n, d = x2.shape
    tm = TM_ROUTE
    per_b = seq // tm
    tok = lambda i: (i, 0)
    rw_pad = jnp.pad(router_w, ((0, 0), (0, LANES - N_EXPERTS)))
    rw_hi = rw_pad.astype(BF16)
    rw_split = jnp.concatenate([rw_hi, (rw_pad - rw_hi.astype(F32)).astype(BF16)], axis=1)
    return pl.pallas_call(
        _router_kernel,
        grid=(n // tm,),
        in_specs=[
            pl.BlockSpec((tm, d), tok),
            pl.BlockSpec((tm, a.shape[1]), tok),
            pl.BlockSpec((tm, o.shape[1]), tok),
            pl.BlockSpec((None, N_MOD, d), lambda i: (i // per_b, 0, 0)),
            _const_spec((d, d)),
            _const_spec((1, d)),
            _const_spec((d, 2 * LANES)),
        ],
        out_specs=[
            pl.BlockSpec((tm, d), tok),
            pl.BlockSpec((tm, d), tok),
            pl.BlockSpec((1, 8, tm), lambda i: (i, 0, 0)),
            pl.BlockSpec((N_EXPERTS, LANES), lambda i: (0, 0)),
        ],
        out_shape=[
            jax.ShapeDtypeStruct((n, d), F32),
            jax.ShapeDtypeStruct((n, d), F32),
            jax.ShapeDtypeStruct((n // tm, 8, tm), F32),
            jax.ShapeDtypeStruct((N_EXPERTS, LANES), F32),
        ],
        scratch_shapes=[pltpu.VMEM((N_EXPERTS, LANES), F32)],
        compiler_params=_params(("arbitrary",), 48),
        name="mixer_post_router",
    )(x2, a, o, mod_l, w_out.astype(BF16), norm_ffn.reshape(1, d), rw_split)


def _dispatch_kernel(lo_ref, hi_ref, p1_ref, p2_ref, h_ref, w1_ref, w3_ref, w2_ref,
                     xs_hbm, w1b_ref, w3b_ref, w2b_ref, zbuf, sem, zsem):
    tm = p1_ref.shape[2]
    w1b_ref[...] = w1_ref[...].astype(BF16)
    w3b_ref[...] = w3_ref[...].astype(BF16)
    w2b_ref[...] = w2_ref[...].astype(BF16)

    @pl.when(pl.program_id(0) == 0)
    def _():
        zbuf[...] = jnp.zeros(zbuf.shape, F32)

        def fill(r, _):
            pltpu.make_async_copy(zbuf.at[pl.ds(0, 1), :], xs_hbm.at[pl.ds(r, 1), :], zsem).start()
            return 0

        for g in range(lo_ref.shape[0]):
            lax.fori_loop(lo_ref[g], hi_ref[g], fill, 0)
        slack = N_EXPERTS * TG
        pltpu.make_async_copy(xs_hbm.at[pl.ds(0, slack), :], xs_hbm.at[pl.ds(0, slack), :], zsem).wait()

    def issue(t, _):
        src = h_ref.at[pl.ds(t, 1), :]
        pltpu.make_async_copy(src, xs_hbm.at[pl.ds(p1_ref[0, 0, t], 1), :], sem).start(priority=0)
        pltpu.make_async_copy(src, xs_hbm.at[pl.ds(p2_ref[0, 0, t], 1), :], sem).start(priority=1)
        return 0

    lax.fori_loop(0, tm, issue, 0, unroll=8)
    for _ in range(TOP_K):
        pltpu.make_async_copy(h_ref, xs_hbm.at[pl.ds(0, tm), :], sem).wait()


def _dispatch(h, pos1, pos2, pad_lo, pad_hi, rows, w1, w3, w2):
    n, d = h.shape
    tm = TM_DISPATCH
    steps = n // tm
    idx_spec = pl.BlockSpec((1, 1, tm), lambda i, lo, hi: (i, 0, 0), memory_space=pltpu.SMEM)
    flat = [w.reshape(-1, w.shape[-1]) for w in (w1, w3, w2)]
    slab = lambda w: pl.BlockSpec((w.shape[0] // steps, w.shape[1]), lambda i, lo, hi: (i, 0))
    grid_spec = pltpu.PrefetchScalarGridSpec(
        num_scalar_prefetch=2,
        grid=(steps,),
        in_specs=[idx_spec, idx_spec, pl.BlockSpec((tm, d), lambda i, lo, hi: (i, 0))]
        + [slab(w) for w in flat],
        out_specs=[pl.BlockSpec(memory_space=pl.ANY)] + [slab(w) for w in flat],
        scratch_shapes=[pltpu.VMEM((8, d), F32), pltpu.SemaphoreType.DMA(()),
                        pltpu.SemaphoreType.DMA(())],
    )
    xs, w1b, w3b, w2b = pl.pallas_call(
        _dispatch_kernel,
        grid_spec=grid_spec,
        out_shape=[jax.ShapeDtypeStruct((rows, d), F32)]
        + [jax.ShapeDtypeStruct(w.shape, BF16) for w in flat],
        compiler_params=pltpu.CompilerParams(dimension_semantics=("arbitrary",),
                                             vmem_limit_bytes=48 * MIB, has_side_effects=True),
        name="moe_dispatch",
    )(pad_lo, pad_hi, pos1.reshape(steps, 1, tm), pos2.reshape(steps, 1, tm), h, *flat)
    return xs, w1b.reshape(w1.shape), w3b.reshape(w3.shape), w2b.reshape(w2.shape)


def _expert_kernel(te_ref, tv_ref, xs_ref, w1_ref, w3_ref, w2_ref, y_ref):
    i = pl.program_id(0)
    valid = tv_ref[i]

    @pl.when(valid > 0)
    def _():
        xb = xs_ref[...].astype(BF16)
        dff = w1_ref.shape[1]
        acc = jnp.zeros(xs_ref.shape, F32)
        for c0 in range(0, dff, FF_CHUNK):
            h1 = _dot(xb, w1_ref[:, c0:c0 + FF_CHUNK])
            h3 = _dot(xb, w3_ref[:, c0:c0 + FF_CHUNK])
            act = (_silu(h1) * h3).astype(BF16)
            acc = acc + _dot(act, w2_ref[c0:c0 + FF_CHUNK, :])
        y_ref[...] = acc

    @pl.when(valid <= 0)
    def _():
        y_ref[...] = jnp.zeros(y_ref.shape, F32)


def _experts(xs, tile_expert, tile_valid, w1, w3, w2):
    rows, d = xs.shape
    tg = TG
    dff = w1.shape[2]
    grid_spec = pltpu.PrefetchScalarGridSpec(
        num_scalar_prefetch=2,
        grid=(rows // tg,),
        in_specs=[
            pl.BlockSpec((tg, d), lambda i, te, tv: (i, 0)),
            pl.BlockSpec((None, d, dff), lambda i, te, tv: (te[i], 0, 0)),
            pl.BlockSpec((None, d, dff), lambda i, te, tv: (te[i], 0, 0)),
            pl.BlockSpec((None, dff, d), lambda i, te, tv: (te[i], 0, 0)),
        ],
        out_specs=pl.BlockSpec((tg, d), lambda i, te, tv: (i, 0)),
    )
    return pl.pallas_call(
        _expert_kernel,
        grid_spec=grid_spec,
        out_shape=jax.ShapeDtypeStruct((rows, d), F32),
        compiler_params=_params(("arbitrary",), 56),
        name="moe_experts",
    )(tile_expert, tile_valid, xs, w1, w3, w2)


def _combine_kernel(p1_cur, p2_cur, p1_nxt, p2_nxt, x1_ref, info_ref, mod_ref, y_hbm, out_ref,
                    buf1, buf2, sems):
    i = pl.program_id(0)
    tm = x1_ref.shape[0]
    slot = lax.rem(i, 2)

    def issue(p1_ref, p2_ref, s):
        def body(t, _):
            pltpu.make_async_copy(y_hbm.at[pl.ds(p1_ref[0, 0, t], 1), :],
                                  buf1.at[s, pl.ds(t, 1), :], sems.at[s]).start(priority=0)
            pltpu.make_async_copy(y_hbm.at[pl.ds(p2_ref[0, 0, t], 1), :],
                                  buf2.at[s, pl.ds(t, 1), :], sems.at[s]).start(priority=1)
            return 0

        lax.fori_loop(0, tm, body, 0, unroll=8)

    @pl.when(i == 0)
    def _():
        issue(p1_cur, p2_cur, 0)

    @pl.when(i + 1 < pl.num_programs(0))
    def _():
        issue(p1_nxt, p2_nxt, 1 - slot)

    pltpu.make_async_copy(y_hbm.at[pl.ds(0, tm), :], buf1.at[slot], sems.at[slot]).wait()
    pltpu.make_async_copy(y_hbm.at[pl.ds(0, tm), :], buf2.at[slot], sems.at[slot]).wait()
    info = info_ref[0]
    pad = jnp.zeros((LANES - info.shape[0], tm), F32)
    cols = jnp.concatenate([info, pad], axis=0).T
    y = cols[:, 2:3] * buf1[slot] + cols[:, 3:4] * buf2[slot]
    out_ref[...] = x1_ref[...] + mod_ref[5:6, :] * y


def _combine(x1, y, pos1, pos2, info, mod_l, seq):
    n, d = x1.shape
    tm = TM_COMB
    per_b = seq // tm
    nb = n // tm
    tok = lambda i: (i, 0)
    cur = pl.BlockSpec((1, 1, tm), lambda i: (i, 0, 0), memory_space=pltpu.SMEM)
    nxt = pl.BlockSpec((1, 1, tm), lambda i: (jnp.minimum(i + 1, nb - 1), 0, 0), memory_space=pltpu.SMEM)
    per_info = info.shape[2] // tm
    p1 = pos1.reshape(nb, 1, tm)
    p2 = pos2.reshape(nb, 1, tm)
    return pl.pallas_call(
        _combine_kernel,
        grid=(nb,),
        in_specs=[
            cur, cur, nxt, nxt,
            pl.BlockSpec((tm, d), tok),
            pl.BlockSpec((1, info.shape[1], tm), lambda i: (i // per_info, 0, i % per_info)),
            pl.BlockSpec((None, N_MOD, d), lambda i: (i // per_b, 0, 0)),
            pl.BlockSpec(memory_space=pl.ANY),
        ],
        out_specs=pl.BlockSpec((tm, d), tok),
        out_shape=jax.ShapeDtypeStruct((n, d), F32),
        scratch_shapes=[pltpu.VMEM((2, tm, d), F32), pltpu.VMEM((2, tm, d), F32),
                        pltpu.SemaphoreType.DMA((2,))],
        compiler_params=_params(("arbitrary",), 40),
        name="moe_combine",
    )(p1, p2, p1, p2, x1, info, mod_l, y)


def _moe_layer_tail(x2, a, o, mod_l, w_out, norm_ffn, router_w, w1, w3, w2, seq):
    n, d = x2.shape
    x1, h, info, counts = _router(x2, a, o, mod_l, w_out, norm_ffn, router_w, seq)

    field = lambda r: info[:, r, :].reshape(n).astype(jnp.int32)
    cnt = counts[:, 0].astype(jnp.int32)
    padded = ((cnt + TG - 1) // TG) * TG
    ends = jnp.cumsum(padded)
    starts = ends - padded
    pos1 = starts[field(0)] + field(4)
    pos2 = starts[field(1)] + field(5)
    rows = TOP_K * n + N_EXPERTS * TG
    tile_start = jnp.arange(rows // TG, dtype=jnp.int32) * TG
    tile_expert = jnp.minimum(jnp.sum((ends[None, :] <= tile_start[:, None]).astype(jnp.int32), axis=1),
                              N_EXPERTS - 1)
    tile_valid = jnp.clip(starts[tile_expert] + cnt[tile_expert] - tile_start, 0, TG).astype(jnp.int32)

    pad_lo = jnp.concatenate([starts + cnt, ends[-1:]]).astype(jnp.int32)
    pad_hi = jnp.concatenate([ends, jnp.full((1,), rows, ends.dtype)]).astype(jnp.int32)

    xs, w1b, w3b, w2b = _dispatch(h, pos1, pos2, pad_lo, pad_hi, rows, w1, w3, w2)
    y = _experts(xs, tile_expert, tile_valid, w1b, w3b, w2b)
    return _combine(x1, y, pos1, pos2, info, mod_l, seq)


def kernel(x, c, positions, ada_w, ada_b, norm_mix, norm_ffn, w_in, sgu_norm, sgu_w, sgu_b,
           q_lat_norm, kv_lat_norm, w_uq, w_ukv, q_norm, k_norm, w_out,
           ffn_w1, ffn_w3, ffn_w2, router_w, moe_w1, moe_w3, moe_w2):
    batch, seq, d = x.shape
    depth = ada_w.shape[0]
    mod = _modulation(c, ada_w, ada_b)
    cos_t, sin_t = _rope_tables(positions)
    x2 = x.reshape(batch * seq, d)
    for layer in range(depth):
        a, q, k, v = _mixer_pre(x2, mod[layer], norm_mix[layer], w_in[layer], sgu_norm[layer],
                                sgu_w[layer], sgu_b[layer], q_lat_norm[layer], kv_lat_norm[layer],
                                w_uq[layer], w_ukv[layer], q_norm[layer], k_norm[layer],
                                cos_t, sin_t, seq)
        o = _attention(q, k, v, batch, seq)
        i = layer // 2
        if layer % 2 == 0:
            x2 = _dense_layer_tail(x2, a, o, mod[layer], w_out[layer], norm_ffn[layer],
                                   ffn_w1[i], ffn_w3[i], ffn_w2[i], seq)
        else:
            x2 = _moe_layer_tail(x2, a, o, mod[layer], w_out[layer], norm_ffn[layer], router_w[i],
                                 moe_w1[i], moe_w3[i], moe_w2[i], seq)
    return x2.reshape(batch, seq, d)
```

```python
import functools
import math

import jax
import jax.numpy as jnp
from jax import lax
from jax.experimental import pallas as pl
from jax.experimental.pallas import tpu as pltpu

F32 = jnp.float32
BF16 = jnp.bfloat16

EPS = 1e-6
ROPE_BASE = 10000.0
SGU_GROUPS = 8
SGU_GROUP_DIM = 64
SGU_WIDTH = SGU_GROUPS * SGU_GROUP_DIM
CHUNK = 128
MLA_HEADS = 4
QK_NOPE = 128
QK_ROPE = 64
QK_HEAD = QK_NOPE + QK_ROPE
QK_PAD = 256
V_HEAD = 128
MLA_WIDTH = MLA_HEADS * V_HEAD
Q_LORA = 256
KV_LORA = 128
N_MOD = 6
N_EXPERTS = 8
TOP_K = 2
LANES = 128
D_IN_PAD = 2 * SGU_WIDTH + Q_LORA + KV_LORA + LANES

MIB = 1024 * 1024

TM_MIX = 2048
MIX_SUB = 512
TQ = 4096
ATTN_BAND = 512
TM_FFN = 512
TM_ROUTE = 512
TM_DISPATCH = 1024
TG = 256
TM_COMB = 512
FF_CHUNK = 2816


def _dot(a, b):
    return jnp.dot(a, b, preferred_element_type=F32)


def _gelu_tanh(x):
    k = -2.0 * math.sqrt(2.0 / math.pi) * math.log2(math.e)
    return x * (1.0 / (1.0 + jnp.exp2(x * (k + (k * 0.044715) * (x * x)))))


def _silu(x):
    return x * (1.0 / (1.0 + jnp.exp(-x)))


def _row_sum(v):
    return _dot(_fold_lane_tiles(v).astype(BF16), jnp.ones((LANES, LANES), BF16))


def _fold_lane_tiles(v):
    part = v[:, :LANES]
    for j in range(LANES, v.shape[1], LANES):
        part = part + v[:, j:j + LANES]
    return part


def _row_sum_pair(va, vb):
    parts = jnp.concatenate([_fold_lane_tiles(va), _fold_lane_tiles(vb)], axis=1).astype(BF16)
    r = lax.broadcasted_iota(jnp.int32, (2 * LANES, 2 * LANES), 0) // LANES
    c = lax.broadcasted_iota(jnp.int32, (2 * LANES, 2 * LANES), 1) // LANES
    both = _dot(parts, jnp.where(r == c, 1.0, 0.0).astype(BF16))
    return both[:, :LANES], both[:, LANES:]


def _lanes(v, width):
    return jnp.concatenate([v] * (width // LANES), axis=1)


def _scale_rows(x, s):
    return jnp.concatenate([x[:, j:j + LANES] * s for j in range(0, x.shape[1], LANES)], axis=1)


def _rms(x, width):
    rs = lax.rsqrt(_row_sum(x * x) * (1.0 / width) + EPS)
    return _scale_rows(x, rs)


def _params(sem, vmem_mib):
    return pltpu.CompilerParams(dimension_semantics=sem, vmem_limit_bytes=vmem_mib * MIB)


def _const_spec(shape):
    nd = len(shape)
    return pl.BlockSpec(shape, lambda *_: (0,) * nd)


def _mod_kernel(c_ref, w_ref, b_ref, o_ref):
    c = c_ref[...]
    ca = _silu(c)
    ca_hi = ca.astype(BF16)
    ca_lo = (ca - ca_hi.astype(F32)).astype(BF16)
    w = w_ref[...]
    w_hi = w.astype(BF16)
    w_lo = (w - w_hi.astype(F32)).astype(BF16)
    acc = _dot(ca_hi, w_hi) + _dot(ca_hi, w_lo) + _dot(ca_lo, w_hi)
    o_ref[...] = acc + b_ref[...]


def _modulation(c, ada_w, ada_b):
    L, D, W = ada_w.shape
    B = c.shape[0]
    rows = 16
    c_pad = jnp.zeros((rows, D), F32).at[:B].set(c)
    tn = 1536
    out = pl.pallas_call(
        _mod_kernel,
        grid=(L, W // tn),
        in_specs=[
            pl.BlockSpec((rows, D), lambda l, j: (0, 0)),
            pl.BlockSpec((None, D, tn), lambda l, j: (l, 0, j)),
            pl.BlockSpec((None, 1, tn), lambda l, j: (l, 0, j)),
        ],
        out_specs=pl.BlockSpec((None, rows, tn), lambda l, j: (l, 0, j)),
        out_shape=jax.ShapeDtypeStruct((L, rows, W), F32),
        compiler_params=_params(("parallel", "parallel"), 40),
        name="adaln_mod",
    )(c_pad, ada_w, ada_b.reshape(L, 1, W))
    return out[:, :B].reshape(L, B, N_MOD, D)


def _rope_kernel(pos_ref, inv_ref, cos_ref, sin_ref):
    pos = pos_ref[0].astype(F32)
    ang = inv_ref[...] * pos
    co = jnp.cos(ang)
    si = jnp.sin(ang)
    z = jnp.zeros((2 * co.shape[0], co.shape[1]), F32)
    cos_ref[...] = jnp.concatenate([co, co, z], axis=0).T
    sin_ref[...] = jnp.concatenate([-si, si, z], axis=0).T


def _rope_tables(positions):
    n = positions.size
    tn = min(2048, n)
    half = QK_ROPE // 2
    inv_freq = 1.0 / (ROPE_BASE ** (jnp.arange(0, QK_ROPE, 2, dtype=F32) / QK_ROPE))
    pos3 = positions.reshape(n // tn, 1, tn)
    return pl.pallas_call(
        _rope_kernel,
        grid=(n // tn,),
        in_specs=[
            pl.BlockSpec((1, 1, tn), lambda i: (i, 0, 0)),
            pl.BlockSpec((half, 1), lambda i: (0, 0)),
        ],
        out_specs=[pl.BlockSpec((tn, LANES), lambda i: (i, 0))] * 2,
        out_shape=[jax.ShapeDtypeStruct((n, LANES), F32)] * 2,
        compiler_params=_params(("parallel",), 32),
        name="rope_tables",
    )(pos3, inv_freq.reshape(half, 1))


def _rope_rotate(r, cos_t, sin_t, lane):
    partner = jnp.where(lane < QK_ROPE // 2,
                        pltpu.roll(r, LANES - QK_ROPE // 2, 1),
                        pltpu.roll(r, QK_ROPE // 2, 1))
    return r * cos_t + partner * sin_t


def _mixer_pre_kernel(x_ref, mod_ref, nmix_ref, win_ref, sgn_ref, sgw_ref, sgb_ref,
                      qln_ref, kvln_ref, wuq_ref, wuk_ref, wuv_ref, qn_ref, kn_ref,
                      cos_ref, sin_ref, a_ref, q_ref, k_ref, v_ref):
    for r0 in range(0, x_ref.shape[0], MIX_SUB):
        rows = pl.ds(r0, MIX_SUB)
        _mixer_pre_rows(x_ref.at[rows], mod_ref, nmix_ref, win_ref, sgn_ref, sgw_ref, sgb_ref,
                        qln_ref, kvln_ref, wuq_ref, wuk_ref, wuv_ref, qn_ref, kn_ref,
                        cos_ref.at[rows], sin_ref.at[rows],
                        a_ref.at[rows], q_ref.at[rows], k_ref.at[rows], v_ref.at[rows])


def _mixer_pre_rows(x_ref, mod_ref, nmix_ref, win_ref, sgn_ref, sgw_ref, sgb_ref,
                    qln_ref, kvln_ref, wuq_ref, wuk_ref, wuv_ref, qn_ref, kn_ref,
                    cos_ref, sin_ref, a_ref, q_ref, k_ref, v_ref):
    tm, d = x_ref.shape
    gain = nmix_ref[...] * (1.0 + mod_ref[1:2, :])
    h = _rms(x_ref[...], d) * gain + mod_ref[0:1, :]
    proj = _dot(h.astype(BF16), win_ref[...])

    o_zv = SGU_WIDTH
    o_cq = 2 * SGU_WIDTH
    o_ckv = o_cq + Q_LORA
    o_kr = o_ckv + KV_LORA

    u = _gelu_tanh(proj[:, :SGU_WIDTH])
    gv = _gelu_tanh(proj[:, o_zv:o_cq])
    cq = proj[:, o_cq:o_ckv]
    ckv = proj[:, o_ckv:o_kr]
    gv_sum, cq_ss = _row_sum_pair(gv, cq * cq)
    cen = gv - _lanes(gv_sum * (1.0 / SGU_WIDTH), SGU_WIDTH)
    cen_ss, ckv_ss = _row_sum_pair(cen * cen, ckv * ckv)
    vn = cen * _lanes(lax.rsqrt(cen_ss * (1.0 / SGU_WIDTH) + EPS), SGU_WIDTH) * sgn_ref[...]

    row = lax.broadcasted_iota(jnp.int32, (CHUNK, CHUNK), 0)
    col = lax.broadcasted_iota(jnp.int32, (CHUNK, CHUNK), 1)
    causal = col <= row
    lane = lax.broadcasted_iota(jnp.int32, (CHUNK, LANES), 1)
    low_half = lane < SGU_GROUP_DIM
    n_pairs = SGU_GROUPS // 2
    wcat = []
    for j in range(n_pairs):
        wa = jnp.where(causal, sgw_ref[2 * j], 0.0)
        wb = jnp.where(causal, sgw_ref[2 * j + 1], 0.0)
        wcat.append(jnp.concatenate([wa, wb], axis=1).astype(BF16))
    n_chunks = tm // CHUNK
    for j in range(n_pairs):
        l0 = j * LANES
        pieces = []
        for c in range(n_chunks):
            vb = vn[c * CHUNK:(c + 1) * CHUNK, l0:l0 + LANES]
            pieces.append(jnp.concatenate([jnp.where(low_half, vb, 0.0),
                                           jnp.where(low_half, 0.0, vb)], axis=0))
        s_all = _dot(wcat[j], jnp.concatenate(pieces, axis=1).astype(BF16))
        for c in range(n_chunks):
            r0 = c * CHUNK
            s = s_all[:, c * LANES:(c + 1) * LANES] + sgb_ref[:, l0:l0 + LANES]
            a_ref[r0:r0 + CHUNK, l0:l0 + LANES] = (u[r0:r0 + CHUNK, l0:l0 + LANES] * s).astype(BF16)

    lane_t = lax.broadcasted_iota(jnp.int32, (tm, LANES), 1)
    cos_t = cos_ref[...]
    sin_t = sin_ref[...]
    q_scale = QK_HEAD ** -0.5 * math.log2(math.e)

    cqn = _scale_rows(cq, lax.rsqrt(cq_ss * (1.0 / Q_LORA) + EPS)) * qln_ref[...]
    qf = _dot(cqn.astype(BF16), wuq_ref[...])
    ckvn = _scale_rows(ckv, lax.rsqrt(ckv_ss * (1.0 / KV_LORA) + EPS)) * kvln_ref[...]
    ckvb = ckvn.astype(BF16)
    kf = _dot(ckvb, wuk_ref[...])
    v_ref[...] = _dot(ckvb, wuv_ref[...]).astype(BF16)
    kr = proj[:, o_kr:o_kr + LANES]

    qn_lo = qn_ref[:, :QK_NOPE]
    qn_hi = qn_ref[:, QK_NOPE:]
    kn_lo = kn_ref[:, :QK_NOPE]
    kr_sq = kr * kr
    kr_rot = _rope_rotate(kr * kn_ref[:, QK_NOPE:], cos_t, sin_t, lane_t)
    for hd in range(MLA_HEADS):
        q_lo = qf[:, hd * QK_PAD:hd * QK_PAD + QK_NOPE]
        q_hi = qf[:, hd * QK_PAD + QK_NOPE:(hd + 1) * QK_PAD]
        k_lo = kf[:, hd * QK_NOPE:(hd + 1) * QK_NOPE]
        q_ss, k_ss = _row_sum_pair(q_lo * q_lo + q_hi * q_hi, k_lo * k_lo + kr_sq)
        rq = lax.rsqrt(q_ss * (1.0 / QK_HEAD) + EPS) * q_scale
        rk = lax.rsqrt(k_ss * (1.0 / QK_HEAD) + EPS)
        q_ref[:, hd * QK_PAD:hd * QK_PAD + QK_NOPE] = (q_lo * rq * qn_lo).astype(BF16)
        q_ref[:, hd * QK_PAD + QK_NOPE:(hd + 1) * QK_PAD] = _rope_rotate(
            q_hi * rq * qn_hi, cos_t, sin_t, lane_t).astype(BF16)
        k_ref[:, hd * QK_PAD:hd * QK_PAD + QK_NOPE] = (k_lo * rk * kn_lo).astype(BF16)
        k_ref[:, hd * QK_PAD + QK_NOPE:(hd + 1) * QK_PAD] = (kr_rot * rk).astype(BF16)


def _mixer_pre(x2, mod_l, norm_mix, w_in, sgu_norm, sgu_w, sgu_b, q_lat_norm, kv_lat_norm,
               w_uq, w_ukv, q_norm, k_norm, cos_t, sin_t, seq):
    n, d = x2.shape
    tm = min(TM_MIX, seq)
    per_b = seq // tm
    win_p = jnp.pad(w_in, ((0, 0), (0, D_IN_PAD - w_in.shape[1]))).astype(BF16)
    wuq_p = jnp.pad(w_uq.reshape(Q_LORA, MLA_HEADS, QK_HEAD),
                    ((0, 0), (0, 0), (0, QK_PAD - QK_HEAD))).reshape(Q_LORA, MLA_HEADS * QK_PAD).astype(BF16)
    wukv = w_ukv.reshape(KV_LORA, MLA_HEADS, QK_NOPE + V_HEAD)
    wuk = wukv[:, :, :QK_NOPE].reshape(KV_LORA, MLA_HEADS * QK_NOPE).astype(BF16)
    wuv = wukv[:, :, QK_NOPE:].reshape(KV_LORA, MLA_WIDTH).astype(BF16)
    qn_p = jnp.pad(q_norm, (0, QK_PAD - QK_HEAD)).reshape(1, QK_PAD)
    kn_p = jnp.pad(k_norm, (0, QK_PAD - QK_HEAD)).reshape(1, QK_PAD)
    sgb_full = jnp.repeat(sgu_b.T, SGU_GROUP_DIM, axis=1)

    tok = lambda i: (i, 0)
    in_specs = [
        pl.BlockSpec((tm, d), tok),
        pl.BlockSpec((None, N_MOD, d), lambda i: (i // per_b, 0, 0)),
        _const_spec((1, d)),
        _const_spec(win_p.shape),
        _const_spec((1, SGU_WIDTH)),
        _const_spec(sgu_w.shape),
        _const_spec(sgb_full.shape),
        _const_spec((1, Q_LORA)),
        _const_spec((1, KV_LORA)),
        _const_spec(wuq_p.shape),
        _const_spec(wuk.shape),
        _const_spec(wuv.shape),
        _const_spec((1, QK_PAD)),
        _const_spec((1, QK_PAD)),
        pl.BlockSpec((tm, LANES), tok),
        pl.BlockSpec((tm, LANES), tok),
    ]
    out_shape = [
        jax.ShapeDtypeStruct((n, SGU_WIDTH), BF16),
        jax.ShapeDtypeStruct((n, MLA_HEADS * QK_PAD), BF16),
        jax.ShapeDtypeStruct((n, MLA_HEADS * QK_PAD), BF16),
        jax.ShapeDtypeStruct((n, MLA_WIDTH), BF16),
    ]
    out_specs = [pl.BlockSpec((tm, s.shape[1]), tok) for s in out_shape]
    return pl.pallas_call(
        _mixer_pre_kernel,
        grid=(n // tm,),
        in_specs=in_specs,
        out_specs=out_specs,
        out_shape=out_shape,
        compiler_params=_params(("parallel",), 56),
        name="mixer_pre",
    )(x2, mod_l, norm_mix.reshape(1, d), win_p, sgu_norm.reshape(1, SGU_WIDTH), sgu_w, sgb_full,
      q_lat_norm.reshape(1, Q_LORA), kv_lat_norm.reshape(1, KV_LORA), wuq_p, wuk, wuv, qn_p, kn_p,
      cos_t, sin_t)


def _attn_kernel(q_ref, k_ref, v_ref, o_ref):
    tq = q_ref.shape[0]
    qi = pl.program_id(2)
    neg = jnp.finfo(F32).min

    def update(q, kb, vb, m, l, acc, mask_from):
        s = lax.dot_general(q, kb, (((1,), (1,)), ((), ())), preferred_element_type=F32)
        if mask_from is not None:
            row = lax.broadcasted_iota(jnp.int32, s.shape, 0)
            col = lax.broadcasted_iota(jnp.int32, s.shape, 1)
            s = jnp.where(col <= row + mask_from, s, neg)
        m_new = jnp.maximum(m, jnp.max(s, axis=-1, keepdims=True))
        alpha = jnp.exp2(m - m_new)
        p = jnp.exp2(s - m_new)
        l_new = alpha * l + jnp.sum(p, axis=-1, keepdims=True)
        acc_new = alpha * acc + _dot(p.astype(BF16), vb)
        return m_new, l_new, acc_new

    def full_step(ki, carry):
        k0 = pl.multiple_of(ki * tq, tq)
        return update(q_ref[...], k_ref[pl.ds(k0, tq), :], v_ref[pl.ds(k0, tq), :], *carry, None)

    init = (jnp.full((tq, 1), neg, F32), jnp.zeros((tq, 1), F32), jnp.zeros((tq, V_HEAD), F32))
    m, l, acc = lax.fori_loop(0, qi, full_step, init)
    k0 = pl.multiple_of(qi * tq, tq)
    for r in range(tq // ATTN_BAND):
        rows = slice(r * ATTN_BAND, (r + 1) * ATTN_BAND)
        nk = (r + 1) * ATTN_BAND
        _, lr, ar = update(q_ref[rows, :], k_ref[pl.ds(k0, nk), :], v_ref[pl.ds(k0, nk), :],
                           m[rows], l[rows], acc[rows], r * ATTN_BAND)
        o_ref[rows, :] = (ar / lr).astype(BF16)


def _attention(q, k, v, batch, seq):
    tq = min(TQ, seq)
    q3 = q.reshape(batch, seq, MLA_HEADS * QK_PAD)
    k3 = k.reshape(batch, seq, MLA_HEADS * QK_PAD)
    v3 = v.reshape(batch, seq, MLA_WIDTH)
    out = pl.pallas_call(
        _attn_kernel,
        grid=(batch, MLA_HEADS, seq // tq),
        in_specs=[
            pl.BlockSpec((None, tq, QK_PAD), lambda b, h, i: (b, i, h)),
            pl.BlockSpec((None, seq, QK_PAD), lambda b, h, i: (b, 0, h)),
            pl.BlockSpec((None, seq, V_HEAD), lambda b, h, i: (b, 0, h)),
        ],
        out_specs=pl.BlockSpec((None, tq, V_HEAD), lambda b, h, i: (b, i, h)),
        out_shape=jax.ShapeDtypeStruct((batch, seq, MLA_WIDTH), BF16),
        compiler_params=_params(("parallel", "parallel", "arbitrary"), 56),
        name="causal_attention",
    )(q3, k3, v3)
    return out.reshape(batch * seq, MLA_WIDTH)


def _mixer_post(x, a, o, mod_ref, wout_ref, nffn_ref):
    half = a.shape[1]
    y = _dot(a, wout_ref[:half, :]) + _dot(o, wout_ref[half:, :])
    x1 = x + mod_ref[2:3, :] * y
    gain = nffn_ref[...] * (1.0 + mod_ref[4:5, :])
    h = _rms(x1, x1.shape[1]) * gain + mod_ref[3:4, :]
    return x1, h


def _dense_ffn_kernel(x_ref, a_ref, o_ref, mod_ref, wout_ref, nffn_ref, w1_ref, w3_ref, w2_ref,
                      out_ref):
    x1, h = _mixer_post(x_ref[...], a_ref[...], o_ref[...], mod_ref, wout_ref, nffn_ref)
    hb = h.astype(BF16)
    dff = w1_ref.shape[1]
    acc = jnp.zeros(x1.shape, F32)
    for c0 in range(0, dff, FF_CHUNK):
        h1 = _dot(hb, w1_ref[:, c0:c0 + FF_CHUNK])
        h3 = _dot(hb, w3_ref[:, c0:c0 + FF_CHUNK])
        act = (_silu(h1) * h3).astype(BF16)
        acc = acc + _dot(act, w2_ref[c0:c0 + FF_CHUNK, :])
    out_ref[...] = x1 + mod_ref[5:6, :] * acc


def _dense_layer_tail(x2, a, o, mod_l, w_out, norm_ffn, w1, w3, w2, seq):
    n, d = x2.shape
    tm = TM_FFN
    per_b = seq // tm
    tok = lambda i: (i, 0)
    single = pl.Buffered(1)
    wspec = lambda shape: pl.BlockSpec(shape, lambda i: (0, 0), pipeline_mode=single)
    dff = w1.shape[1]
    return pl.pallas_call(
        _dense_ffn_kernel,
        grid=(n // tm,),
        in_specs=[
            pl.BlockSpec((tm, d), tok),
            pl.BlockSpec((tm, a.shape[1]), tok),
            pl.BlockSpec((tm, o.shape[1]), tok),
            pl.BlockSpec((None, N_MOD, d), lambda i: (i // per_b, 0, 0)),
            wspec((d, d)),
            _const_spec((1, d)),
            wspec((d, dff)),
            wspec((d, dff)),
            wspec((dff, d)),
        ],
        out_specs=pl.BlockSpec((tm, d), tok),
        out_shape=jax.ShapeDtypeStruct((n, d), F32),
        compiler_params=_params(("parallel",), 56),
        name="dense_ffn",
    )(x2, a, o, mod_l, w_out.astype(BF16), norm_ffn.reshape(1, d),
      w1.astype(BF16), w3.astype(BF16), w2.astype(BF16))


def _router_kernel(x_ref, a_ref, o_ref, mod_ref, wout_ref, nffn_ref, rw_ref,
                   x1_ref, h_ref, info_ref, cnt_ref, carry_ref):
    @pl.when(pl.program_id(0) == 0)
    def _():
        carry_ref[...] = jnp.zeros(carry_ref.shape, F32)

    x1, h = _mixer_post(x_ref[...], a_ref[...], o_ref[...], mod_ref, wout_ref, nffn_ref)
    x1_ref[...] = x1
    h_ref[...] = h
    tm = h.shape[0]

    h_hi = h.astype(BF16)
    h_lo = (h - h_hi.astype(F32)).astype(BF16)
    both = _dot(h_hi, rw_ref[...])
    logits = both[:, :LANES] + both[:, LANES:] + _dot(h_lo, rw_ref[:, :LANES])
    lt = logits.T[:N_EXPERTS, :]

    eid = lax.broadcasted_iota(jnp.int32, lt.shape, 0)
    m1 = jnp.max(lt, axis=0, keepdims=True)
    i1 = jnp.min(jnp.where(lt == m1, eid, N_EXPERTS), axis=0, keepdims=True)
    rest = jnp.where(eid == i1, -jnp.inf, lt)
    m2 = jnp.max(rest, axis=0, keepdims=True)
    i2 = jnp.min(jnp.where(rest == m2, eid, N_EXPERTS), axis=0, keepdims=True)
    e2 = jnp.exp(m2 - m1)
    g1 = 1.0 / (1.0 + e2)
    g2 = e2 / (1.0 + e2)

    oh1 = (eid == i1).astype(F32)
    oh2 = (eid == i2).astype(F32)
    r_io = lax.broadcasted_iota(jnp.int32, (tm, tm), 0)
    c_io = lax.broadcasted_iota(jnp.int32, (tm, tm), 1)
    before = jnp.where(r_io < c_io, 1.0, 0.0).astype(BF16)
    cnt1 = jnp.sum(oh1, axis=1, keepdims=True)
    cnt2 = jnp.sum(oh2, axis=1, keepdims=True)
    base = carry_ref[:, 0:1]
    rank1_e = _dot(oh1.astype(BF16), before) + base
    rank2_e = _dot(oh2.astype(BF16), before) + base + cnt1
    rank1 = jnp.sum(oh1 * rank1_e, axis=0, keepdims=True)
    rank2 = jnp.sum(oh2 * rank2_e, axis=0, keepdims=True)
    total = base + cnt1 + cnt2
    carry_ref[...] = jnp.broadcast_to(total, carry_ref.shape)
    cnt_ref[...] = jnp.broadcast_to(total, cnt_ref.shape)

    zero = jnp.zeros_like(g1)
    info_ref[0] = jnp.concatenate(
        [i1.astype(F32), i2.astype(F32), g1, g2, rank1, rank2, zero, zero], axis=0)


def _router(x2, a, o, mod_l, w_out, norm_ffn, router_w, seq):
    n, d = x2.shape
    tm = TM_ROUTE
    per_b = seq // tm
    tok = lambda i: (i, 0)
    rw_pad = jnp.pad(router_w, ((0, 0), (0, LANES - N_EXPERTS)))
    rw_hi = rw_pad.astype(BF16)
    rw_split = jnp.concatenate([rw_hi, (rw_pad - rw_hi.astype(F32)).astype(BF16)], axis=1)
    return pl.pallas_call(
        _router_kernel,
        grid=(n // tm,),
        in_specs=[
            pl.BlockSpec((tm, d), tok),
            pl.BlockSpec((tm, a.shape[1]), tok),
            pl.BlockSpec((tm, o.shape[1]), tok),
            pl.BlockSpec((None, N_MOD, d), lambda i: (i // per_b, 0, 0)),
            _const_spec((d, d)),
            _const_spec((1, d)),
            _const_spec((d, 2 * LANES)),
        ],
        out_specs=[
            pl.BlockSpec((tm, d), tok),
            pl.BlockSpec((tm, d), tok),
            pl.BlockSpec((1, 8, tm), lambda i: (i, 0, 0)),
            pl.BlockSpec((N_EXPERTS, LANES), lambda i: (0, 0)),
        ],
        out_shape=[
            jax.ShapeDtypeStruct((n, d), F32),
            jax.ShapeDtypeStruct((n, d), F32),
            jax.ShapeDtypeStruct((n // tm, 8, tm), F32),
            jax.ShapeDtypeStruct((N_EXPERTS, LANES), F32),
        ],
        scratch_shapes=[pltpu.VMEM((N_EXPERTS, LANES), F32)],
        compiler_params=_params(("arbitrary",), 48),
        name="mixer_post_router",
    )(x2, a, o, mod_l, w_out.astype(BF16), norm_ffn.reshape(1, d), rw_split)


def _dispatch_kernel(lo_ref, hi_ref, p1_ref, p2_ref, h_ref, w1_ref, w3_ref, w2_ref,
                     xs_hbm, w1b_ref, w3b_ref, w2b_ref, zbuf, sem, zsem):
    tm = p1_ref.shape[2]
    w1b_ref[...] = w1_ref[...].astype(BF16)
    w3b_ref[...] = w3_ref[...].astype(BF16)
    w2b_ref[...] = w2_ref[...].astype(BF16)

    @pl.when(pl.program_id(0) == 0)
    def _():
        zbuf[...] = jnp.zeros(zbuf.shape, F32)

        def fill(r, _):
            pltpu.make_async_copy(zbuf.at[pl.ds(0, 1), :], xs_hbm.at[pl.ds(r, 1), :], zsem).start()
            return 0

        for g in range(lo_ref.shape[0]):
            lax.fori_loop(lo_ref[g], hi_ref[g], fill, 0)
        slack = N_EXPERTS * TG
        pltpu.make_async_copy(xs_hbm.at[pl.ds(0, slack), :], xs_hbm.at[pl.ds(0, slack), :], zsem).wait()

    def issue(t, _):
        src = h_ref.at[pl.ds(t, 1), :]
        pltpu.make_async_copy(src, xs_hbm.at[pl.ds(p1_ref[0, 0, t], 1), :], sem).start(priority=0)
        pltpu.make_async_copy(src, xs_hbm.at[pl.ds(p2_ref[0, 0, t], 1), :], sem).start(priority=1)
        return 0

    lax.fori_loop(0, tm, issue, 0, unroll=8)
    for _ in range(TOP_K):
        pltpu.make_async_copy(h_ref, xs_hbm.at[pl.ds(0, tm), :], sem).wait()


def _dispatch(h, pos1, pos2, pad_lo, pad_hi, rows, w1, w3, w2):
    n, d = h.shape
    tm = TM_DISPATCH
    steps = n // tm
    idx_spec = pl.BlockSpec((1, 1, tm), lambda i, lo, hi: (i, 0, 0), memory_space=pltpu.SMEM)
    flat = [w.reshape(-1, w.shape[-1]) for w in (w1, w3, w2)]
    slab = lambda w: pl.BlockSpec((w.shape[0] // steps, w.shape[1]), lambda i, lo, hi: (i, 0))
    grid_spec = pltpu.PrefetchScalarGridSpec(
        num_scalar_prefetch=2,
        grid=(steps,),
        in_specs=[idx_spec, idx_spec, pl.BlockSpec((tm, d), lambda i, lo, hi: (i, 0))]
        + [slab(w) for w in flat],
        out_specs=[pl.BlockSpec(memory_space=pl.ANY)] + [slab(w) for w in flat],
        scratch_shapes=[pltpu.VMEM((8, d), F32), pltpu.SemaphoreType.DMA(()),
                        pltpu.SemaphoreType.DMA(())],
    )
    xs, w1b, w3b, w2b = pl.pallas_call(
        _dispatch_kernel,
        grid_spec=grid_spec,
        out_shape=[jax.ShapeDtypeStruct((rows, d), F32)]
        + [jax.ShapeDtypeStruct(w.shape, BF16) for w in flat],
        compiler_params=pltpu.CompilerParams(dimension_semantics=("arbitrary",),
                                             vmem_limit_bytes=48 * MIB, has_side_effects=True),
        name="moe_dispatch",
    )(pad_lo, pad_hi, pos1.reshape(steps, 1, tm), pos2.reshape(steps, 1, tm), h, *flat)
    return xs, w1b.reshape(w1.shape), w3b.reshape(w3.shape), w2b.reshape(w2.shape)


def _expert_kernel(te_ref, tv_ref, xs_ref, w1_ref, w3_ref, w2_ref, y_ref):
    i = pl.program_id(0)
    valid = tv_ref[i]

    @pl.when(valid > 0)
    def _():
        xb = xs_ref[...].astype(BF16)
        dff = w1_ref.shape[1]
        acc = jnp.zeros(xs_ref.shape, F32)
        for c0 in range(0, dff, FF_CHUNK):
            h1 = _dot(xb, w1_ref[:, c0:c0 + FF_CHUNK])
            h3 = _dot(xb, w3_ref[:, c0:c0 + FF_CHUNK])
            act = (_silu(h1) * h3).astype(BF16)
            acc = acc + _dot(act, w2_ref[c0:c0 + FF_CHUNK, :])
        y_ref[...] = acc

    @pl.when(valid <= 0)
    def _():
        y_ref[...] = jnp.zeros(y_ref.shape, F32)


def _experts(xs, tile_expert, tile_valid, w1, w3, w2):
    rows, d = xs.shape
    tg = TG
    dff = w1.shape[2]
    grid_spec = pltpu.PrefetchScalarGridSpec(
        num_scalar_prefetch=2,
        grid=(rows // tg,),
        in_specs=[
            pl.BlockSpec((tg, d), lambda i, te, tv: (i, 0)),
            pl.BlockSpec((None, d, dff), lambda i, te, tv: (te[i], 0, 0)),
            pl.BlockSpec((None, d, dff), lambda i, te, tv: (te[i], 0, 0)),
            pl.BlockSpec((None, dff, d), lambda i, te, tv: (te[i], 0, 0)),
        ],
        out_specs=pl.BlockSpec((tg, d), lambda i, te, tv: (i, 0)),
    )
    return pl.pallas_call(
        _expert_kernel,
        grid_spec=grid_spec,
        out_shape=jax.ShapeDtypeStruct((rows, d), F32),
        compiler_params=_params(("arbitrary",), 56),
        name="moe_experts",
    )(tile_expert, tile_valid, xs, w1, w3, w2)


def _combine_kernel(p1_cur, p2_cur, p1_nxt, p2_nxt, x1_ref, info_ref, mod_ref, y_hbm, out_ref,
                    buf1, buf2, sems):
    i = pl.program_id(0)
    tm = x1_ref.shape[0]
    slot = lax.rem(i, 2)

    def issue(p1_ref, p2_ref, s):
        def body(t, _):
            pltpu.make_async_copy(y_hbm.at[pl.ds(p1_ref[0, 0, t], 1), :],
                                  buf1.at[s, pl.ds(t, 1), :], sems.at[s]).start(priority=0)
            pltpu.make_async_copy(y_hbm.at[pl.ds(p2_ref[0, 0, t], 1), :],
                                  buf2.at[s, pl.ds(t, 1), :], sems.at[s]).start(priority=1)
            return 0

        lax.fori_loop(0, tm, body, 0, unroll=8)

    @pl.when(i == 0)
    def _():
        issue(p1_cur, p2_cur, 0)

    @pl.when(i + 1 < pl.num_programs(0))
    def _():
        issue(p1_nxt, p2_nxt, 1 - slot)

    pltpu.make_async_copy(y_hbm.at[pl.ds(0, tm), :], buf1.at[slot], sems.at[slot]).wait()
    pltpu.make_async_copy(y_hbm.at[pl.ds(0, tm), :], buf2.at[slot], sems.at[slot]).wait()
    info = info_ref[0]
    pad = jnp.zeros((LANES - info.shape[0], tm), F32)
    cols = jnp.concatenate([info, pad], axis=0).T
    y = cols[:, 2:3] * buf1[slot] + cols[:, 3:4] * buf2[slot]
    out_ref[...] = x1_ref[...] + mod_ref[5:6, :] * y


def _combine(x1, y, pos1, pos2, info, mod_l, seq):
    n, d = x1.shape
    tm = TM_COMB
    per_b = seq // tm
    nb = n // tm
    tok = lambda i: (i, 0)
    cur = pl.BlockSpec((1, 1, tm), lambda i: (i, 0, 0), memory_space=pltpu.SMEM)
    nxt = pl.BlockSpec((1, 1, tm), lambda i: (jnp.minimum(i + 1, nb - 1), 0, 0), memory_space=pltpu.SMEM)
    per_info = info.shape[2] // tm
    p1 = pos1.reshape(nb, 1, tm)
    p2 = pos2.reshape(nb, 1, tm)
    return pl.pallas_call(
        _combine_kernel,
        grid=(nb,),
        in_specs=[
            cur, cur, nxt, nxt,
            pl.BlockSpec((tm, d), tok),
            pl.BlockSpec((1, info.shape[1], tm), lambda i: (i // per_info, 0, i % per_info)),
            pl.BlockSpec((None, N_MOD, d), lambda i: (i // per_b, 0, 0)),
            pl.BlockSpec(memory_space=pl.ANY),
        ],
        out_specs=pl.BlockSpec((tm, d), tok),
        out_shape=jax.ShapeDtypeStruct((n, d), F32),
        scratch_shapes=[pltpu.VMEM((2, tm, d), F32), pltpu.VMEM((2, tm, d), F32),
                        pltpu.SemaphoreType.DMA((2,))],
        compiler_params=_params(("arbitrary",), 40),
        name="moe_combine",
    )(p1, p2, p1, p2, x1, info, mod_l, y)


def _moe_layer_tail(x2, a, o, mod_l, w_out, norm_ffn, router_w, w1, w3, w2, seq):
    n, d = x2.shape
    x1, h, info, counts = _router(x2, a, o, mod_l, w_out, norm_ffn, router_w, seq)

    field = lambda r: info[:, r, :].reshape(n).astype(jnp.int32)
    cnt = counts[:, 0].astype(jnp.int32)
    padded = ((cnt + TG - 1) // TG) * TG
    ends = jnp.cumsum(padded)
    starts = ends - padded
    pos1 = starts[field(0)] + field(4)
    pos2 = starts[field(1)] + field(5)
    rows = TOP_K * n + N_EXPERTS * TG
    tile_start = jnp.arange(rows // TG, dtype=jnp.int32) * TG
    tile_expert = jnp.minimum(jnp.sum((ends[None, :] <= tile_start[:, None]).astype(jnp.int32), axis=1),
                              N_EXPERTS - 1)
    tile_valid = jnp.clip(starts[tile_expert] + cnt[tile_expert] - tile_start, 0, TG).astype(jnp.int32)

    pad_lo = jnp.concatenate([starts + cnt, ends[-1:]]).astype(jnp.int32)
    pad_hi = jnp.concatenate([ends, jnp.full((1,), rows, ends.dtype)]).astype(jnp.int32)

    xs, w1b, w3b, w2b = _dispatch(h, pos1, pos2, pad_lo, pad_hi, rows, w1, w3, w2)
    y = _experts(xs, tile_expert, tile_valid, w1b, w3b, w2b)
    return _combine(x1, y, pos1, pos2, info, mod_l, seq)


def kernel(x, c, positions, ada_w, ada_b, norm_mix, norm_ffn, w_in, sgu_norm, sgu_w, sgu_b,
           q_lat_norm, kv_lat_norm, w_uq, w_ukv, q_norm, k_norm, w_out,
           ffn_w1, ffn_w3, ffn_w2, router_w, moe_w1, moe_w3, moe_w2):
    batch, seq, d = x.shape
    depth = ada_w.shape[0]
    mod = _modulation(c, ada_w, ada_b)
    cos_t, sin_t = _rope_tables(positions)
    x2 = x.reshape(batch * seq, d)
    for layer in range(depth):
        a, q, k, v = _mixer_pre(x2, mod[layer], norm_mix[layer], w_in[layer], sgu_norm[layer],
                                sgu_w[layer], sgu_b[layer], q_lat_norm[layer], kv_lat_norm[layer],
                                w_uq[layer], w_ukv[layer], q_norm[layer], k_norm[layer],
                                cos_t, sin_t, seq)
        o = _attention(q, k, v, batch, seq)
        i = layer // 2
        if layer % 2 == 0:
            x2 = _dense_layer_tail(x2, a, o, mod[layer], w_out[layer], norm_ffn[layer],
                                   ffn_w1[i], ffn_w3[i], ffn_w2[i], seq)
        else:
            x2 = _moe_layer_tail(x2, a, o, mod[layer], w_out[layer], norm_ffn[layer], router_w[i],
                                 moe_w1[i], moe_w3[i], moe_w2[i], seq)
    return x2.reshape(batch, seq, d)
```
